```python
import jax
import jax.numpy as jnp
from jax import lax
import numpy as np

D_MODEL = 2048
BATCH = 8
SEQ = 2048
DEPTH = 1

N_MEM = 256
EPS = 1e-6
ROPE_THETA = 10000.0
MAX_POS_OFFSET = 4096

HEAD_DIM = 128
DIL_PATTERNS = ((128, 1), (512, 4), (2048, 16))
N_DIL_GROUPS = len(DIL_PATTERNS)
HEADS_PER_GROUP = 4
A_HEADS = N_DIL_GROUPS * HEADS_PER_GROUP
A_WIDTH = A_HEADS * HEAD_DIM
A_OUT = HEADS_PER_GROUP * HEAD_DIM
SW_BLOCK = 128

HG_KDIM = 128
HG_HEADS = D_MODEL // HG_KDIM
HG_VDIM = D_MODEL // HG_HEADS
HG_WIDTH = HG_HEADS * HG_VDIM
HG_CHUNK = 64

X_HEADS = 4
X_HEAD_DIM = 3 * D_MODEL // 16
X_WIDTH = X_HEADS * X_HEAD_DIM

N_BRANCH = 3
SPLIT_SIZES = (A_WIDTH, A_WIDTH, A_WIDTH,
               HG_HEADS * HG_KDIM, HG_HEADS * HG_KDIM, HG_WIDTH, HG_WIDTH,
               X_WIDTH, N_BRANCH * D_MODEL)
SPLIT_POINTS = tuple(int(v) for v in np.cumsum(SPLIT_SIZES)[:-1])
IN_WIDTH = int(sum(SPLIT_SIZES))

N_EXPERTS = 64
TOP_K = 8
EXPERT_FF = D_MODEL // 4
SHARED_FF = D_MODEL // 4
ROUTED_SCALE = 2.5
MOE_BLOCK = 128

kernel_name = 'hybrid_dilated_hgrn2_memxattn_moe_block'


def _rms_norm(x, gain):
    xf = x.astype(jnp.float32)
    y = xf * lax.rsqrt(jnp.mean(xf * xf, axis=-1, keepdims=True) + EPS)
    return (y * gain.astype(jnp.float32)).astype(x.dtype)


def _rope(x, positions):
    half = x.shape[-1] // 2
    inv_freq = ROPE_THETA ** (-jnp.arange(half, dtype=jnp.float32) / half)
    ang = positions.astype(jnp.float32)[:, :, None, None] * inv_freq
    cos, sin = jnp.cos(ang), jnp.sin(ang)
    xf = x.astype(jnp.float32)
    x1, x2 = xf[..., :half], xf[..., half:]
    return jnp.concatenate([x1 * cos - x2 * sin, x2 * cos + x1 * sin], axis=-1).astype(x.dtype)


def _dilated_group_attention(q, k, v, window, dilation):
    B, S, H, dh = q.shape
    n_back = window // dilation
    L = S // dilation
    nb = -(-L // SW_BLOCK)
    Lp = nb * SW_BLOCK

    def to_blocks(t):
        t = t.reshape(B, L, dilation, H, dh).transpose(0, 2, 3, 1, 4)
        t = jnp.pad(t, ((0, 0), (0, 0), (0, 0), (0, Lp - L), (0, 0)))
        return t.reshape(B, dilation, H, nb, SW_BLOCK, dh)

    def with_prev(t):
        prev = jnp.pad(t, ((0, 0), (0, 0), (0, 0), (1, 0), (0, 0), (0, 0)))[:, :, :, :nb]
        return jnp.concatenate([prev, t], axis=4)

    qb = to_blocks(q)
    kk = with_prev(to_blocks(k))
    vv = with_prev(to_blocks(v))
    s = jnp.einsum('bzhnqc,bzhnkc->bzhnqk', qb, kk,
                   preferred_element_type=jnp.float32) * (dh ** -0.5)
    i = jnp.arange(SW_BLOCK)[:, None]
    j = jnp.arange(2 * SW_BLOCK)[None, :]
    rel = SW_BLOCK + i - j
    band = (rel >= 0) & (rel <= n_back)
    has_prev = (jnp.arange(nb)[:, None, None] > 0) | (j[None] >= SW_BLOCK)
    mask = band[None] & has_prev
    s = jnp.where(mask, s, -jnp.inf)
    m = jnp.max(s, axis=-1, keepdims=True)
    p = jnp.exp(s - m)
    den = jnp.sum(p, axis=-1, keepdims=True)
    o = jnp.einsum('bzhnqk,bzhnkc->bzhnqc', (p / den).astype(v.dtype), vv,
                   preferred_element_type=jnp.float32)
    lse = (m + jnp.log(den))[..., 0]
    o = o.reshape(B, dilation, H, Lp, dh)[:, :, :, :L].transpose(0, 3, 1, 2, 4).reshape(B, S, H, dh)
    lse = lse.reshape(B, dilation, H, Lp)[:, :, :, :L].transpose(0, 3, 1, 2).reshape(B, S, H)
    return o, lse


def _hgrn2(q, f_logit, i_in, lb):
    B, S, H, K = q.shape
    V = i_in.shape[-1]
    C = HG_CHUNK
    nc = S // C
    qf = jax.nn.silu(q.astype(jnp.float32))
    fg = lb + (1.0 - lb) * jax.nn.sigmoid(f_logit.astype(jnp.float32))
    logf = jnp.log(fg)
    kf = 1.0 - fg

    def chunks(t):
        return t.reshape(B, nc, C, H, t.shape[-1]).transpose(1, 0, 3, 2, 4)

    causal = jnp.tril(jnp.ones((C, C), dtype=bool))[:, :, None]

    def step(state, xs):
        qc, kc, vc, lc = xs
        G = jnp.cumsum(lc, axis=2)
        diff = G[:, :, :, None, :] - G[:, :, None, :, :]
        decay = jnp.exp(jnp.where(causal, diff, -jnp.inf))
        A = jnp.einsum('bhtk,bhsk,bhtsk->bhts', qc, kc, decay)
        o = (jnp.einsum('bhts,bhsv->bhtv', A, vc)
             + jnp.einsum('bhtk,bhkv->bhtv', qc * jnp.exp(G), state))
        g_last = G[:, :, -1:, :]
        state = (jnp.exp(g_last[:, :, 0, :])[..., None] * state
                 + jnp.einsum('bhsk,bhsv->bhkv', kc * jnp.exp(g_last - G), vc))
        return state, o

    state0 = jnp.zeros((B, H, K, V), jnp.float32)
    _, o = lax.scan(step, state0,
                    (chunks(qf), chunks(kf), chunks(i_in.astype(jnp.float32)), chunks(logf)))
    return o.transpose(1, 0, 3, 2, 4).reshape(B, S, H, V)


def _memory_cross_attention(q, mem_n, w_mem_kv, q_gain, k_gain):
    B, M, _ = mem_n.shape
    kv = (mem_n @ w_mem_kv).reshape(B, M, 2, X_HEADS, X_HEAD_DIM)
    k = _rms_norm(kv[:, :, 0], k_gain)
    v = kv[:, :, 1]
    q = _rms_norm(q, q_gain)
    s = jnp.einsum('bshc,bmhc->bhsm', q, k,
                   preferred_element_type=jnp.float32) * (X_HEAD_DIM ** -0.5)
    p = jax.nn.softmax(s, axis=-1)
    return jnp.einsum('bhsm,bmhc->bshc', p.astype(v.dtype), v)


def _swiglu(x, wg, wu, wd):
    return (jax.nn.silu(x @ wg) * (x @ wu)) @ wd


def _moe(hn, w_router, router_bias, w_gate, w_up, w_down, ws_gate, ws_up, ws_down):
    B, S, D = hn.shape
    N = B * S
    NK = N * TOP_K
    xt = hn.reshape(N, D)
    scores = jax.nn.sigmoid(xt.astype(jnp.float32) @ w_router.astype(jnp.float32))
    _, idx = lax.top_k(scores + router_bias.astype(jnp.float32), TOP_K)
    w = jnp.take_along_axis(scores, idx, axis=-1)
    w = w / jnp.sum(w, axis=-1, keepdims=True) * ROUTED_SCALE

    flat_e = idx.reshape(-1)
    flat_tok = jnp.arange(NK, dtype=jnp.int32) // TOP_K
    flat_w = w.reshape(-1)
    order = jnp.argsort(flat_e)
    se = flat_e[order]
    counts = jnp.bincount(flat_e, length=N_EXPERTS)
    padded = (counts + MOE_BLOCK - 1) // MOE_BLOCK * MOE_BLOCK
    pends = jnp.cumsum(padded)
    pstart = pends - padded
    start = jnp.cumsum(counts) - counts
    dest = pstart[se] + jnp.arange(NK) - start[se]
    n_blocks = -(-(NK + N_EXPERTS * (MOE_BLOCK - 1)) // MOE_BLOCK)
    P = n_blocks * MOE_BLOCK
    slot_tok = jnp.full((P,), N, jnp.int32).at[dest].set(flat_tok[order])
    slot_w = jnp.zeros((P,), jnp.float32).at[dest].set(flat_w[order])
    block_e = jnp.clip(jnp.searchsorted(pends, jnp.arange(n_blocks) * MOE_BLOCK, side='right'),
                       0, N_EXPERTS - 1)
    xpad = jnp.concatenate([xt, jnp.zeros((1, D), xt.dtype)], axis=0)

    def run_block(args):
        tok, wt, e = args
        xb = xpad[tok]
        yb = _swiglu(xb, w_gate[e], w_up[e], w_down[e])
        return yb * wt[:, None].astype(yb.dtype)

    yb = lax.map(run_block, (slot_tok.reshape(n_blocks, MOE_BLOCK),
                             slot_w.reshape(n_blocks, MOE_BLOCK), block_e))
    routed = jnp.zeros((N + 1, D), yb.dtype).at[slot_tok].add(yb.reshape(P, D))[:N]
    shared = _swiglu(xt, ws_gate, ws_up, ws_down)
    return (routed + shared).reshape(B, S, D)


def setup_inputs(seed: int = 0) -> dict:
    key = jax.random.key(seed)
    ks = jax.random.split(key, 26)
    f32 = jnp.float32

    def normal(k, shape, scale):
        return jax.random.normal(k, shape, f32) * scale

    def gain(k, shape):
        return 1.0 + 0.05 * jax.random.normal(k, shape, f32)

    D = D_MODEL
    x = normal(ks[0], (BATCH, SEQ, D), 1.0)
    mem = normal(ks[1], (BATCH, N_MEM, D), 1.0)
    offset = jax.random.randint(ks[2], (BATCH, 1), 0, MAX_POS_OFFSET, dtype=jnp.int32)
    positions = (offset + jnp.arange(SEQ, dtype=jnp.int32)[None, :]).astype(jnp.int32)
    return {
        'x': x,
        'mem': mem,
        'positions': positions,
        'mix_norm': gain(ks[3], (DEPTH, D)),
        'w_in': normal(ks[4], (DEPTH, D, IN_WIDTH), D ** -0.5),
        'a_q_gain': gain(ks[5], (DEPTH, HEAD_DIM)),
        'a_k_gain': gain(ks[6], (DEPTH, HEAD_DIM)),
        'hg_lower_bounds': normal(ks[7], (DEPTH + 1, HG_HEADS * HG_KDIM), 1.0),
        'hg_out_gain': gain(ks[8], (DEPTH, HG_VDIM)),
        'x_q_gain': gain(ks[9], (DEPTH, X_HEAD_DIM)),
        'x_k_gain': gain(ks[10], (DEPTH, X_HEAD_DIM)),
        'mem_norm': gain(ks[11], (DEPTH, D)),
        'w_mem_kv': normal(ks[12], (DEPTH, D, 2 * X_WIDTH), D ** -0.5),
        'w_branch_a': normal(ks[13], (DEPTH, A_OUT, D), A_OUT ** -0.5),
        'w_branch_b': normal(ks[14], (DEPTH, HG_WIDTH, D), HG_WIDTH ** -0.5),
        'w_branch_c': normal(ks[15], (DEPTH, X_WIDTH, D), X_WIDTH ** -0.5),
        'w_out': normal(ks[16], (DEPTH, D, D), D ** -0.5),
        'ffn_norm': gain(ks[17], (DEPTH, D)),
        'w_router': normal(ks[18], (DEPTH, D, N_EXPERTS), D ** -0.5),
        'router_bias': normal(ks[19], (DEPTH, N_EXPERTS), 0.01),
        'w_exp_gate': normal(ks[20], (DEPTH, N_EXPERTS, D, EXPERT_FF), D ** -0.5),
        'w_exp_up': normal(ks[21], (DEPTH, N_EXPERTS, D, EXPERT_FF), D ** -0.5),
        'w_exp_down': normal(ks[22], (DEPTH, N_EXPERTS, EXPERT_FF, D), EXPERT_FF ** -0.5),
        'w_sh_gate': normal(ks[23], (DEPTH, D, SHARED_FF), D ** -0.5),
        'w_sh_up': normal(ks[24], (DEPTH, D, SHARED_FF), D ** -0.5),
        'w_sh_down': normal(ks[25], (DEPTH, SHARED_FF, D), SHARED_FF ** -0.5),
    }


def reference(x, mem, positions, mix_norm, w_in, a_q_gain, a_k_gain, hg_lower_bounds,
              hg_out_gain, x_q_gain, x_k_gain, mem_norm, w_mem_kv, w_branch_a, w_branch_b,
              w_branch_c, w_out, ffn_norm, w_router, router_bias, w_exp_gate, w_exp_up,
              w_exp_down, w_sh_gate, w_sh_up, w_sh_down):
    B, S, D = x.shape
    lower = jnp.cumsum(jax.nn.softmax(hg_lower_bounds.astype(jnp.float32), axis=0), axis=0)
    h = x
    for layer in range(DEPTH):
        xn = _rms_norm(h, mix_norm[layer])
        proj = xn @ w_in[layer]
        aq, ak, av, hq, hf, hi, hgate, xq, gates = jnp.split(proj, SPLIT_POINTS, axis=-1)

        aq = _rope(_rms_norm(aq.reshape(B, S, A_HEADS, HEAD_DIM), a_q_gain[layer]), positions)
        ak = _rope(_rms_norm(ak.reshape(B, S, A_HEADS, HEAD_DIM), a_k_gain[layer]), positions)
        av = av.reshape(B, S, A_HEADS, HEAD_DIM)
        outs, lses = [], []
        for g, (window, dilation) in enumerate(DIL_PATTERNS):
            sl = slice(g * HEADS_PER_GROUP, (g + 1) * HEADS_PER_GROUP)
            o, l = _dilated_group_attention(aq[:, :, sl], ak[:, :, sl], av[:, :, sl], window, dilation)
            outs.append(o)
            lses.append(l)
        alpha = jax.nn.softmax(jnp.stack(lses, axis=0), axis=0)
        ya = jnp.sum(alpha[..., None] * jnp.stack(outs, axis=0), axis=0)
        ya = ya.reshape(B, S, A_OUT).astype(x.dtype)

        ob = _hgrn2(hq.reshape(B, S, HG_HEADS, HG_KDIM), hf.reshape(B, S, HG_HEADS, HG_KDIM),
                    hi.reshape(B, S, HG_HEADS, HG_VDIM), lower[layer].reshape(HG_HEADS, HG_KDIM))
        yb = _rms_norm(ob, hg_out_gain[layer]) * jax.nn.silu(
            hgate.reshape(B, S, HG_HEADS, HG_VDIM).astype(jnp.float32))
        yb = yb.reshape(B, S, HG_WIDTH).astype(x.dtype)

        mem_n = _rms_norm(mem, mem_norm[layer])
        yc = _memory_cross_attention(xq.reshape(B, S, X_HEADS, X_HEAD_DIM), mem_n, w_mem_kv[layer],
                                     x_q_gain[layer], x_k_gain[layer]).reshape(B, S, X_WIDTH)

        gt = jax.nn.sigmoid(gates.astype(jnp.float32)).reshape(B, S, N_BRANCH, D).astype(x.dtype)
        merged = (gt[:, :, 0] * (ya @ w_branch_a[layer])
                  + gt[:, :, 1] * (yb @ w_branch_b[layer])
                  + gt[:, :, 2] * (yc @ w_branch_c[layer]))
        h = h + merged @ w_out[layer]

        hn = _rms_norm(h, ffn_norm[layer])
        h = h + _moe(hn, w_router[layer], router_bias[layer], w_exp_gate[layer], w_exp_up[layer],
                     w_exp_down[layer], w_sh_gate[layer], w_sh_up[layer], w_sh_down[layer])
    return h
```

```python
import functools
import math

import jax
import jax.numpy as jnp
from jax import lax
from jax.experimental import pallas as pl
from jax.experimental.pallas import tpu as pltpu

F32, BF16, I32 = jnp.float32, jnp.bfloat16, jnp.int32

EPS = 1e-6
ROPE_THETA = 10000.0
HEAD_DIM = 128
DILATIONS = (1, 4, 16)
DIL_BACK = 128
HEADS_PER_GROUP = 4
A_WIDTH = len(DILATIONS) * HEADS_PER_GROUP * HEAD_DIM
A_OUT = HEADS_PER_GROUP * HEAD_DIM
HG_DIM = 128
HG_CHUNK = 64
HG_STRIP = 16
X_HEADS = 4
X_HEAD_DIM = 384
X_WIDTH = X_HEADS * X_HEAD_DIM
N_EXPERTS = 64
TOP_K = 8
ROUTED_SCALE = 2.5
MOE_BLOCK = 256
LANES = 128
VMEM_LIMIT = 56 * 1024 * 1024

NT = (((1,), (1,)), ((), ()))
TN = (((0,), (0,)), ((), ()))


def _params(*sem):
    return pltpu.CompilerParams(dimension_semantics=sem, vmem_limit_bytes=VMEM_LIMIT)


def _rms(x, gain):
    return x * lax.rsqrt(jnp.mean(x * x, axis=-1, keepdims=True) + EPS) * gain


def _silu(x):
    return x * jax.nn.sigmoid(x)


def _bdot(a, b):
    return jnp.dot(a.astype(BF16), b.astype(BF16), preferred_element_type=F32)


def _norm_kernel(x_ref, g_ref, o_ref):
    o_ref[...] = _rms(x_ref[...].astype(F32), g_ref[...]).astype(o_ref.dtype)


def _rms_norm_rows(x, gain, tm):
    m, d = x.shape
    return pl.pallas_call(
        _norm_kernel,
        out_shape=jax.ShapeDtypeStruct((m, d), BF16),
        grid=(m // tm,),
        in_specs=[pl.BlockSpec((tm, d), lambda i: (i, 0)), pl.BlockSpec((1, d), lambda i: (0, 0))],
        out_specs=pl.BlockSpec((tm, d), lambda i: (i, 0)),
        compiler_params=_params("parallel"),
        name="rms_norm_rows",
    )(x, gain.reshape(1, d))


def _rope_table_kernel(pos_ref, invf_ref, cos_ref, sin_ref):
    ang = pos_ref[...].astype(F32) * invf_ref[...]
    lane = lax.broadcasted_iota(I32, ang.shape, 1)
    sin = jnp.sin(ang)
    cos_ref[...] = jnp.cos(ang)
    sin_ref[...] = jnp.where(lane < HEAD_DIM // 2, -sin, sin)


def _rope_tables(positions, tm):
    n = positions.size
    half = HEAD_DIM // 2
    inv_freq = ROPE_THETA ** (-jnp.arange(half, dtype=F32) / half)
    invf = jnp.concatenate([inv_freq, inv_freq]).reshape(1, HEAD_DIM)
    return pl.pallas_call(
        _rope_table_kernel,
        out_shape=[jax.ShapeDtypeStruct((n, HEAD_DIM), F32)] * 2,
        grid=(n // tm,),
        in_specs=[pl.BlockSpec((tm, 1), lambda i: (i, 0)), pl.BlockSpec((1, HEAD_DIM), lambda i: (0, 0))],
        out_specs=[pl.BlockSpec((tm, HEAD_DIM), lambda i: (i, 0))] * 2,
        compiler_params=_params("parallel"),
        name="rope_tables",
    )(positions.reshape(n, 1), invf)


def _proj(a, w, epilogue, *, n0, n, tn, tm, out_dtypes, extras=(), name):
    m, k = a.shape
    assert n0 % tn == 0 and n % tn == 0 and m % tm == 0, (n0, n, tn, m, tm)
    j0 = n0 // tn
    n_extra = len(extras)

    def kern(a_ref, w_ref, *refs):
        acc = jnp.dot(a_ref[...], w_ref[...], preferred_element_type=F32)
        epilogue(acc, refs[:n_extra], refs[n_extra:])

    in_specs = [pl.BlockSpec((tm, k), lambda i, j: (i, 0)),
                pl.BlockSpec((k, tn), lambda i, j: (0, j + j0))]
    in_specs += [pl.BlockSpec(bs, im) for _, bs, im in extras]
    return pl.pallas_call(
        kern,
        out_shape=[jax.ShapeDtypeStruct((m, n), dt) for dt in out_dtypes],
        grid=(m // tm, n // tn),
        in_specs=in_specs,
        out_specs=[pl.BlockSpec((tm, tn), lambda i, j: (i, j)) for _ in out_dtypes],
        compiler_params=_params("parallel", "arbitrary"),
        name=name,
    )(a, w, *[e[0] for e in extras])


def _ep_plain(acc, ins, outs):
    outs[0][...] = acc.astype(outs[0].dtype)


def _ep_silu(acc, ins, outs):
    outs[0][...] = _silu(acc).astype(outs[0].dtype)


def _ep_sigmoid(acc, ins, outs):
    outs[0][...] = jax.nn.sigmoid(acc).astype(outs[0].dtype)


def _ep_qk(scale, acc, ins, outs):
    cos, sin, gain = ins[0][...], ins[1][...], ins[2][...]
    for h in range(acc.shape[1] // HEAD_DIM):
        sl = slice(h * HEAD_DIM, (h + 1) * HEAD_DIM)
        y = _rms(acc[:, sl], gain)
        y = y * cos + pltpu.roll(y, HEAD_DIM // 2, axis=1) * sin
        outs[0][:, sl] = (y * scale).astype(outs[0].dtype)


def _ep_headnorm(width, scale, acc, ins, outs):
    gain = ins[0][...]
    for h in range(acc.shape[1] // width):
        sl = slice(h * width, (h + 1) * width)
        outs[0][:, sl] = (_rms(acc[:, sl], gain) * scale).astype(outs[0].dtype)


def _ep_log_forget(layer, acc, ins, outs):
    raw = ins[0][...]
    mx = jnp.max(raw, axis=0, keepdims=True)
    ex = jnp.exp(raw - mx)
    lb = jnp.sum(ex[:layer + 1], axis=0, keepdims=True) / jnp.sum(ex, axis=0, keepdims=True)
    outs[0][...] = jnp.log(lb + (1.0 - lb) * jax.nn.sigmoid(acc))


def _dil_attn_kernel(*refs, seq):
    o_ref = refs[-1]
    groups = [refs[3 * g:3 * g + 3] for g in range(len(DILATIONS))]
    blk = DIL_BACK

    def body(qi, carry):
        t0 = pl.multiple_of(qi * blk, blk)
        outs, lses = [], []
        for (q_ref, k_ref, v_ref), d in zip(groups, DILATIONS):
            width = min(blk * (d + 1), seq)
            ks = pl.multiple_of(jnp.clip(t0 + blk - width, 0, seq - width), blk)
            q = q_ref[pl.ds(t0, blk), :]
            k = k_ref[pl.ds(ks, width), :]
            v = v_ref[pl.ds(ks, width), :]
            s = lax.dot_general(q, k, NT, preferred_element_type=F32)
            rel = ((t0 - ks) + lax.broadcasted_iota(I32, s.shape, 0)
                   - lax.broadcasted_iota(I32, s.shape, 1))
            mask = (rel >= 0) & (rel <= DIL_BACK * d) & ((rel & (d - 1)) == 0)
            s = jnp.where(mask, s, -jnp.inf)
            m = jnp.max(s, axis=-1, keepdims=True)
            p = jnp.exp(s - m)
            den = jnp.sum(p, axis=-1, keepdims=True)
            outs.append(jnp.dot((p / den).astype(BF16), v, preferred_element_type=F32))
            lses.append(m + jnp.log(den))
        top = functools.reduce(jnp.maximum, lses)
        ws = [jnp.exp(l - top) for l in lses]
        ya = sum(w * o for w, o in zip(ws, outs)) / sum(ws)
        o_ref[pl.ds(t0, blk), :] = ya.astype(o_ref.dtype)
        return carry

    lax.fori_loop(0, seq // blk, body, 0)


def _dilated_attention(q, k, v):
    b, s, _ = q.shape
    in_specs, args = [], []
    for g in range(len(DILATIONS)):
        for t in (q, k, v):
            in_specs.append(pl.BlockSpec((None, s, HEAD_DIM),
                                         lambda bi, h, g=g: (bi, 0, g * HEADS_PER_GROUP + h)))
            args.append(t)
    return pl.pallas_call(
        functools.partial(_dil_attn_kernel, seq=s),
        out_shape=jax.ShapeDtypeStruct((b, s, A_OUT), BF16),
        grid=(b, HEADS_PER_GROUP),
        in_specs=in_specs,
        out_specs=pl.BlockSpec((None, s, HEAD_DIM), lambda bi, h: (bi, 0, h)),
        compiler_params=_params("parallel", "parallel"),
        name="dilated_attention",
    )(*args)


def _hgrn_kernel(q_ref, lf_ref, v_ref, gate_ref, gain_ref, o_ref, st_ref, *, seq):
    C, R = HG_CHUNK, HG_STRIP
    st_ref[...] = jnp.zeros_like(st_ref)
    gain = gain_ref[...]
    row = lax.broadcasted_iota(I32, (C, HG_DIM), 0)
    row_r = lax.broadcasted_iota(I32, (R, 1), 0)

    def body(c, carry):
        r0 = pl.multiple_of(c * C, C)
        lf = lf_ref[pl.ds(r0, C), :]
        G = lf
        step = 1
        while step < C:
            G = G + jnp.where(row >= step, pltpu.roll(G, step, axis=0), 0.0)
            step *= 2
        kf = 1.0 - jnp.exp(lf)
        qf = q_ref[pl.ds(r0, C), :].astype(F32)
        v = v_ref[pl.ds(r0, C), :]
        vf = v.astype(F32)
        st = st_ref[...]
        o_inter = lax.dot_general((qf * jnp.exp(G)).astype(BF16), st.astype(BF16), NT,
                                  preferred_element_type=F32)
        strips = []
        for i in range(C // R):
            lo = i * R
            Gi, gi, qi = G[lo:lo + R], G[lo:lo + 1], qf[lo:lo + R]
            oi = o_inter[lo:lo + R]
            if i > 0:
                qs = (qi * jnp.exp(Gi - gi)).astype(BF16)
                ks = (kf[:lo] * jnp.exp(gi - G[:lo])).astype(BF16)
                a = lax.dot_general(qs, ks, NT, preferred_element_type=F32)
                oi = oi + jnp.dot(a.astype(BF16), v[:lo], preferred_element_type=F32)
            for s in range(R):
                sa = lo + s
                e = jnp.exp(jnp.minimum(Gi - G[sa:sa + 1], 0.0))
                a = jnp.sum(qi * kf[sa:sa + 1] * e, axis=-1, keepdims=True)
                oi = oi + jnp.where(row_r >= s, a, 0.0) * vf[sa:sa + 1]
            strips.append(oi)
        o = jnp.concatenate(strips, axis=0)
        y = _rms(o, gain) * _silu(gate_ref[pl.ds(r0, C), :].astype(F32))
        o_ref[pl.ds(r0, C), :] = y.astype(o_ref.dtype)
        g_last = G[C - 1:C]
        kd = (kf * jnp.exp(g_last - G)).astype(BF16)
        st_ref[...] = st * jnp.exp(g_last) + lax.dot_general(v, kd, TN, preferred_element_type=F32)
        return carry

    lax.fori_loop(0, seq // C, body, 0)


def _hgrn2(q, logf, v, gate, out_gain):
    b, s, width = q.shape
    spec = pl.BlockSpec((None, s, HG_DIM), lambda bi, h: (bi, 0, h))
    return pl.pallas_call(
        functools.partial(_hgrn_kernel, seq=s),
        out_shape=jax.ShapeDtypeStruct((b, s, width), BF16),
        grid=(b, width // HG_DIM),
        in_specs=[spec, spec, spec, spec, pl.BlockSpec((1, HG_DIM), lambda bi, h: (0, 0))],
        out_specs=spec,
        scratch_shapes=[pltpu.VMEM((HG_DIM, HG_DIM), F32)],
        compiler_params=_params("parallel", "parallel"),
        name="hgrn2",
    )(q, logf, v, gate, out_gain.reshape(1, HG_DIM))


def _xattn_kernel(q_ref, k_ref, v_ref, o_ref):
    s = lax.dot_general(q_ref[...], k_ref[...], NT, preferred_element_type=F32)
    p = jnp.exp(s - jnp.max(s, axis=-1, keepdims=True))
    p = p / jnp.sum(p, axis=-1, keepdims=True)
    o_ref[...] = jnp.dot(p.astype(BF16), v_ref[...], preferred_element_type=F32).astype(o_ref.dtype)


def _memory_attention(q, k, v, ts):
    b, s, _ = q.shape
    m = k.shape[1]
    return pl.pallas_call(
        _xattn_kernel,
        out_shape=jax.ShapeDtypeStruct((b, s, X_WIDTH), BF16),
        grid=(b, X_HEADS, s // ts),
        in_specs=[pl.BlockSpec((None, ts, X_HEAD_DIM), lambda bi, h, si: (bi, si, h)),
                  pl.BlockSpec((None, m, X_HEAD_DIM), lambda bi, h, si: (bi, 0, h)),
                  pl.BlockSpec((None, m, X_HEAD_DIM), lambda bi, h, si: (bi, 0, h))],
        out_specs=pl.BlockSpec((None, ts, X_HEAD_DIM), lambda bi, h, si: (bi, si, h)),
        compiler_params=_params("parallel", "parallel", "parallel"),
        name="memory_attention",
    )(q, k, v)


def _merge_kernel(ya_ref, yb_ref, yc_ref, wa_ref, wb_ref, wc_ref, ga_ref, gb_ref, gc_ref, o_ref):
    acc = ga_ref[...].astype(F32) * jnp.dot(ya_ref[...], wa_ref[...], preferred_element_type=F32)
    acc += gb_ref[...].astype(F32) * jnp.dot(yb_ref[...], wb_ref[...], preferred_element_type=F32)
    acc += gc_ref[...].astype(F32) * jnp.dot(yc_ref[...], wc_ref[...], preferred_element_type=F32)
    o_ref[...] = acc.astype(o_ref.dtype)


def _merge_branches(ya, yb, yc, wa, wb, wc, gates, tm, tn):
    n, d = yb.shape[0], wa.shape[1]
    nj = d // tn
    row = lambda a: pl.BlockSpec((tm, a.shape[1]), lambda i, j: (i, 0))
    col = lambda w: pl.BlockSpec((w.shape[0], tn), lambda i, j: (0, j))
    gate = lambda br: pl.BlockSpec((tm, tn), lambda i, j: (i, br * nj + j))
    return pl.pallas_call(
        _merge_kernel,
        out_shape=jax.ShapeDtypeStruct((n, d), BF16),
        grid=(n // tm, nj),
        in_specs=[row(ya), row(yb), row(yc), col(wa), col(wb), col(wc), gate(0), gate(1), gate(2)],
        out_specs=pl.BlockSpec((tm, tn), lambda i, j: (i, j)),
        compiler_params=_params("parallel", "arbitrary"),
        name="merge_branches",
    )(ya, yb, yc, wa, wb, wc, gates, gates, gates)


def _outproj_kernel(m_ref, w_ref, x_ref, g_ref, wr_ref, h_ref, hn_ref, lg_ref):
    h = x_ref[...] + jnp.dot(m_ref[...], w_ref[...], preferred_element_type=F32)
    hn = _rms(h, g_ref[...])
    h_ref[...] = h
    hn_ref[...] = hn
    lg_ref[...] = _bdot(hn, wr_ref[...])


def _out_projection(merged, w_out, x, ffn_gain, w_router, tm):
    n, d = x.shape
    e = w_router.shape[1]
    row = pl.BlockSpec((tm, d), lambda i: (i, 0))
    const = lambda a: pl.BlockSpec(a.shape, lambda i: (0, 0))
    g = ffn_gain.reshape(1, d)
    return pl.pallas_call(
        _outproj_kernel,
        out_shape=[jax.ShapeDtypeStruct((n, d), F32), jax.ShapeDtypeStruct((n, d), F32),
                   jax.ShapeDtypeStruct((n, e), F32)],
        grid=(n // tm,),
        in_specs=[row, const(w_out), row, const(g), const(w_router)],
        out_specs=[row, row, pl.BlockSpec((tm, e), lambda i: (i, 0))],
        compiler_params=_params("parallel"),
        name="out_projection",
    )(merged, w_out, x, g, w_router)


def _route_kernel(lg_ref, bias_ref, idx_ref, wt_ref, rank_ref, cnt_ref, carry_ref):
    i = pl.program_id(0)

    @pl.when(i == 0)
    def _():
        carry_ref[...] = jnp.zeros_like(carry_ref)

    scores = jax.nn.sigmoid(lg_ref[...])
    tm, e = scores.shape
    sel = scores + bias_ref[...]
    lane = lax.broadcasted_iota(I32, (tm, e), 1)
    hots, ws = [], []
    for _ in range(TOP_K):
        mx = jnp.max(sel, axis=-1, keepdims=True)
        ix = jnp.min(jnp.where(sel == mx, lane, e), axis=-1, keepdims=True)
        hot = lane == ix
        hots.append(hot)
        ws.append(jnp.sum(jnp.where(hot, scores, 0.0), axis=-1, keepdims=True))
        sel = jnp.where(hot, -jnp.inf, sel)
    chosen = functools.reduce(jnp.logical_or, hots).astype(F32)
    r = lax.broadcasted_iota(I32, (tm, tm), 0)
    c = lax.broadcasted_iota(I32, (tm, tm), 1)
    before = (c < r).astype(BF16)
    ahead = jnp.dot(before, chosen.astype(BF16), preferred_element_type=F32) + carry_ref[...]
    carry_ref[...] += jnp.sum(chosen, axis=0, keepdims=True)
    cnt_ref[...] = carry_ref[...].astype(I32)

    norm = ROUTED_SCALE / sum(ws)
    out_lane = lax.broadcasted_iota(I32, (tm, LANES), 1)
    idx_o = jnp.zeros((tm, LANES), I32)
    wt_o = jnp.zeros((tm, LANES), F32)
    rk_o = jnp.zeros((tm, LANES), I32)
    for j in range(TOP_K):
        ix = jnp.sum(jnp.where(hots[j], lane, 0), axis=-1, keepdims=True)
        rk = jnp.sum(jnp.where(hots[j], ahead, 0.0), axis=-1, keepdims=True).astype(I32)
        idx_o = jnp.where(out_lane == j, ix, idx_o)
        wt_o = jnp.where(out_lane == j, ws[j] * norm, wt_o)
        rk_o = jnp.where(out_lane == j, rk, rk_o)
    idx_ref[...] = idx_o
    wt_ref[...] = wt_o
    rank_ref[...] = rk_o


def _route(logits, bias, tm):
    n, e = logits.shape
    row = pl.BlockSpec((tm, LANES), lambda i: (i, 0))
    return pl.pallas_call(
        _route_kernel,
        out_shape=[jax.ShapeDtypeStruct((n, LANES), I32), jax.ShapeDtypeStruct((n, LANES), F32),
                   jax.ShapeDtypeStruct((n, LANES), I32), jax.ShapeDtypeStruct((1, e), I32)],
        grid=(n // tm,),
        in_specs=[pl.BlockSpec((tm, e), lambda i: (i, 0)), pl.BlockSpec((1, e), lambda i: (0, 0))],
        out_specs=[row, row, row, pl.BlockSpec((1, e), lambda i: (0, 0))],
        scratch_shapes=[pltpu.VMEM((1, e), F32)],
        compiler_params=_params("arbitrary"),
        name="route",
    )(logits, bias.reshape(1, e))


def _dispatch_kernel(pos_ref, hn_ref, zero_ref, xs_ref, sem):
    del zero_ref
    tm = hn_ref.shape[0]

    def row_copy(r, slot):
        return pltpu.make_async_copy(hn_ref.at[pl.ds(r, 1), :], xs_ref.at[pl.ds(slot, 1), :], sem)

    def issue(r, carry):
        for j in range(TOP_K):
            row_copy(r, pos_ref[r * TOP_K + j]).start()
        return carry

    lax.fori_loop(0, tm, issue, 0)

    def drain(r, carry):
        for j in range(TOP_K):
            row_copy(r, 0).wait()
        return carry

    lax.fori_loop(0, tm, drain, 0)


def _dispatch(hn, pos_flat, n_slots, tm):
    n, d = hn.shape
    return pl.pallas_call(
        _dispatch_kernel,
        out_shape=jax.ShapeDtypeStruct((n_slots, d), F32),
        grid=(n // tm,),
        in_specs=[pl.BlockSpec((tm * TOP_K,), lambda i: (i,), memory_space=pltpu.SMEM),
                  pl.BlockSpec((tm, d), lambda i: (i, 0)),
                  pl.BlockSpec(memory_space=pl.ANY)],
        out_specs=pl.BlockSpec(memory_space=pl.ANY),
        scratch_shapes=[pltpu.SemaphoreType.DMA],
        input_output_aliases={2: 0},
        compiler_params=pltpu.CompilerParams(dimension_semantics=("arbitrary",),
                                             vmem_limit_bytes=VMEM_LIMIT, has_side_effects=True),
        name="moe_dispatch",
    )(pos_flat, hn, jnp.zeros((n_slots, d), F32))


def _expert_kernel(be_ref, nu_ref, x_ref, wg_ref, wu_ref, wd_ref, y_ref):
    del be_ref

    @pl.when(pl.program_id(0) < nu_ref[0])
    def _():
        x = x_ref[...].astype(BF16)
        g = jnp.dot(x, wg_ref[...], preferred_element_type=F32)
        u = jnp.dot(x, wu_ref[...], preferred_element_type=F32)
        y_ref[...] = _bdot(_silu(g) * u, wd_ref[...])


def _experts(xs, block_expert, n_used, wg, wu, wd):
    p, d = xs.shape
    ff = wg.shape[2]
    blk = MOE_BLOCK
    used = lambda b, be, nu: (jnp.minimum(b, nu[0] - 1), 0)
    return pl.pallas_call(
        _expert_kernel,
        out_shape=jax.ShapeDtypeStruct((p, d), F32),
        grid_spec=pltpu.PrefetchScalarGridSpec(
            num_scalar_prefetch=2,
            grid=(p // blk,),
            in_specs=[pl.BlockSpec((blk, d), used),
                      pl.BlockSpec((None, d, ff), lambda b, be, nu: (be[b], 0, 0)),
                      pl.BlockSpec((None, d, ff), lambda b, be, nu: (be[b], 0, 0)),
                      pl.BlockSpec((None, ff, d), lambda b, be, nu: (be[b], 0, 0))],
            out_specs=pl.BlockSpec((blk, d), used)),
        compiler_params=_params("arbitrary"),
        name="moe_experts",
    )(block_expert, n_used, xs, wg, wu, wd)


def _combine_kernel(pos_ref, h_ref, hn_ref, wt_ref, sg_ref, su_ref, sd_ref, y_ref, o_ref, buf, sem):
    tm = h_ref.shape[0]

    def row_copy(r, j, slot):
        return pltpu.make_async_copy(y_ref.at[pl.ds(slot, 1), :], buf.at[j, pl.ds(r, 1), :], sem)

    def issue(r, carry):
        for j in range(TOP_K):
            row_copy(r, j, pos_ref[r * TOP_K + j]).start()
        return carry

    lax.fori_loop(0, tm, issue, 0)

    hn = hn_ref[...].astype(BF16)
    g = jnp.dot(hn, sg_ref[...], preferred_element_type=F32)
    u = jnp.dot(hn, su_ref[...], preferred_element_type=F32)
    acc = h_ref[...] + _bdot(_silu(g) * u, sd_ref[...])

    def drain(r, carry):
        for j in range(TOP_K):
            row_copy(r, j, 0).wait()
        return carry

    lax.fori_loop(0, tm, drain, 0)

    wt = wt_ref[...]
    for j in range(TOP_K):
        acc = acc + wt[:, j:j + 1] * buf[j]
    o_ref[...] = acc


def _combine(h, hn, wts, pos_flat, y, sg, su, sd, tm):
    n, d = h.shape
    row = pl.BlockSpec((tm, d), lambda i: (i, 0))
    const = lambda a: pl.BlockSpec(a.shape, lambda i: (0, 0))
    return pl.pallas_call(
        _combine_kernel,
        out_shape=jax.ShapeDtypeStruct((n, d), F32),
        grid=(n // tm,),
        in_specs=[pl.BlockSpec((tm * TOP_K,), lambda i: (i,), memory_space=pltpu.SMEM),
                  row, row, pl.BlockSpec((tm, LANES), lambda i: (i, 0)),
                  const(sg), const(su), const(sd), pl.BlockSpec(memory_space=pl.ANY)],
        out_specs=row,
        scratch_shapes=[pltpu.VMEM((TOP_K, tm, d), F32), pltpu.SemaphoreType.DMA],
        compiler_params=_params("arbitrary"),
        name="moe_combine",
    )(pos_flat, h, hn, wts, sg, su, sd, y)


def _tile(n, want):
    t = min(n, want)
    assert n % t == 0, (n, want)
    return t


def _layer(layer, h, mem, positions, mix_norm, w_in, a_q_gain, a_k_gain, lower_raw, hg_out_gain, x_q_gain,
           x_k_gain, mem_norm, w_mem_kv, w_branch_a, w_branch_b, w_branch_c, w_out, ffn_norm, w_router,
           router_bias, w_exp_gate, w_exp_up, w_exp_down, w_sh_gate, w_sh_up, w_sh_down):
    b, s, d = h.shape
    n = b * s
    x2 = h.reshape(n, d)
    tm = _tile(n, 1024)
    bf = lambda w: w.astype(BF16)

    xn = _rms_norm_rows(x2, mix_norm, _tile(n, 512))
    cos, sin = _rope_tables(positions, _tile(n, 1024))
    w_in = bf(w_in)
    rope_in = lambda gain: ((cos, (tm, HEAD_DIM), lambda i, j: (i, 0)),
                            (sin, (tm, HEAD_DIM), lambda i, j: (i, 0)),
                            (gain.reshape(1, HEAD_DIM), (1, HEAD_DIM), lambda i, j: (0, 0)))
    proj = functools.partial(_proj, xn, w_in, tm=tm)
    hg_w = lower_raw.shape[1]
    off = 0
    aq, = proj(functools.partial(_ep_qk, HEAD_DIM ** -0.5), n0=off, n=A_WIDTH, tn=512,
               out_dtypes=[BF16], extras=rope_in(a_q_gain), name="proj_aq")
    off += A_WIDTH
    ak, = proj(functools.partial(_ep_qk, 1.0), n0=off, n=A_WIDTH, tn=512,
               out_dtypes=[BF16], extras=rope_in(a_k_gain), name="proj_ak")
    off += A_WIDTH
    av, = proj(_ep_plain, n0=off, n=A_WIDTH, tn=512, out_dtypes=[BF16], name="proj_av")
    off += A_WIDTH
    hq, = proj(_ep_silu, n0=off, n=hg_w, tn=512, out_dtypes=[BF16], name="proj_hq")
    off += hg_w
    logf, = proj(functools.partial(_ep_log_forget, layer), n0=off, n=hg_w, tn=512, out_dtypes=[F32],
                 extras=((lower_raw, (lower_raw.shape[0], 512), lambda i, j: (0, j)),), name="proj_hf")
    off += hg_w
    hiv, = proj(_ep_plain, n0=off, n=hg_w, tn=512, out_dtypes=[BF16], name="proj_hi")
    off += hg_w
    hgate, = proj(_ep_plain, n0=off, n=hg_w, tn=512, out_dtypes=[BF16], name="proj_hgate")
    off += hg_w
    xq, = _proj(xn, w_in[:, off:off + X_WIDTH], functools.partial(_ep_headnorm, X_HEAD_DIM, X_HEAD_DIM ** -0.5),
                n0=0, n=X_WIDTH, tn=2 * X_HEAD_DIM, tm=tm, out_dtypes=[BF16],
                extras=((x_q_gain.reshape(1, X_HEAD_DIM), (1, X_HEAD_DIM), lambda i, j: (0, 0)),),
                name="proj_xq")
    off += X_WIDTH
    gates, = proj(_ep_sigmoid, n0=off, n=3 * d, tn=512, out_dtypes=[BF16], name="proj_gates")

    r3 = lambda t: t.reshape(b, s, t.shape[1])
    ya = _dilated_attention(r3(aq), r3(ak), r3(av)).reshape(n, A_OUT)
    yb = _hgrn2(r3(hq), r3(logf), r3(hiv), r3(hgate), hg_out_gain).reshape(n, hg_w)
    nm = mem.shape[0] * mem.shape[1]
    mem_n = _rms_norm_rows(mem.reshape(nm, d), mem_norm, _tile(nm, 512))
    w_kv = bf(w_mem_kv)
    tmm = _tile(nm, 1024)
    kn, = _proj(mem_n, w_kv, functools.partial(_ep_headnorm, X_HEAD_DIM, 1.0), n0=0, n=X_WIDTH,
                tn=2 * X_HEAD_DIM, tm=tmm, out_dtypes=[BF16],
                extras=((x_k_gain.reshape(1, X_HEAD_DIM), (1, X_HEAD_DIM), lambda i, j: (0, 0)),),
                name="proj_mem_k")
    vm, = _proj(mem_n, w_kv, _ep_plain, n0=X_WIDTH, n=X_WIDTH, tn=2 * X_HEAD_DIM, tm=tmm,
                out_dtypes=[BF16], name="proj_mem_v")
    rm = lambda t: t.reshape(b, mem.shape[1], X_WIDTH)
    yc = _memory_attention(r3(xq), rm(kn), rm(vm), _tile(s, 512)).reshape(n, X_WIDTH)

    merged = _merge_branches(ya, yb, yc, bf(w_branch_a), bf(w_branch_b), bf(w_branch_c), gates, tm, 512)
    h1, hn, logits = _out_projection(merged, bf(w_out), x2, ffn_norm, bf(w_router), _tile(n, 256))

    tr = _tile(n, 256)
    idx, wts, rank, counts = _route(logits, router_bias, tr)
    blk = MOE_BLOCK
    counts = counts[0]
    padded = (counts + blk - 1) // blk * blk
    ends = jnp.cumsum(padded)
    starts = ends - padded
    pos = (starts[idx[:, :TOP_K]] + rank[:, :TOP_K]).reshape(-1).astype(I32)
    n_blocks = -(-(n * TOP_K + N_EXPERTS * (blk - 1)) // blk)
    n_used = (ends[-1] // blk).astype(I32)
    blocks = jnp.minimum(jnp.arange(n_blocks, dtype=I32), n_used - 1)
    block_expert = jnp.clip(jnp.searchsorted(ends, blocks * blk, side="right"), 0, N_EXPERTS - 1)
    xs = _dispatch(hn, pos, n_blocks * blk, tr)
    y = _experts(xs, block_expert.astype(I32), n_used.reshape(1), bf(w_exp_gate), bf(w_exp_up),
                 bf(w_exp_down))
    out = _combine(h1, hn, wts, pos, y, bf(w_sh_gate), bf(w_sh_up), bf(w_sh_down), _tile(n, 128))
    return out.reshape(b, s, d)


def kernel(x, mem, positions, mix_norm, w_in, a_q_gain, a_k_gain, hg_lower_bounds, hg_out_gain, x_q_gain, x_k_gain, mem_norm, w_mem_kv, w_branch_a, w_branch_b, w_branch_c, w_out, ffn_norm, w_router, router_bias, w_exp_gate, w_exp_up, w_exp_down, w_sh_gate, w_sh_up, w_sh_down):
    h = x
    for layer in range(w_in.shape[0]):
        h = _layer(layer, h, mem, positions, mix_norm[layer], w_in[layer], a_q_gain[layer], a_k_gain[layer],
                   hg_lower_bounds, hg_out_gain[layer], x_q_gain[layer], x_k_gain[layer], mem_norm[layer],
                   w_mem_kv[layer], w_branch_a[layer], w_branch_b[layer], w_branch_c[layer], w_out[layer],
                   ffn_norm[layer], w_router[layer], router_bias[layer], w_exp_gate[layer],
                   w_exp_up[layer], w_exp_down[layer], w_sh_gate[layer], w_sh_up[layer], w_sh_down[layer])
    return h
```

```python
import functools
import math

import jax
import jax.numpy as jnp
from jax import lax
from jax.experimental import pallas as pl
from jax.experimental.pallas import tpu as pltpu

F32, BF16, I32 = jnp.float32, jnp.bfloat16, jnp.int32

EPS = 1e-6
ROPE_THETA = 10000.0
HEAD_DIM = 128
DILATIONS = (1, 4, 16)
DIL_BACK = 128
HEADS_PER_GROUP = 4
A_WIDTH = len(DILATIONS) * HEADS_PER_GROUP * HEAD_DIM
A_OUT = HEADS_PER_GROUP * HEAD_DIM
HG_DIM = 128
HG_CHUNK = 64
HG_STRIP = 8
HG_HEADS_PER_STEP = 4
X_HEADS = 4
X_HEAD_DIM = 384
X_WIDTH = X_HEADS * X_HEAD_DIM
N_EXPERTS = 64
TOP_K = 8
ROUTED_SCALE = 2.5
MOE_BLOCK = 256
LANES = 128
VMEM_LIMIT = 56 * 1024 * 1024

NT = (((1,), (1,)), ((), ()))
TN = (((0,), (0,)), ((), ()))


def _params(*sem):
    return pltpu.CompilerParams(dimension_semantics=sem, vmem_limit_bytes=VMEM_LIMIT)


def _rms(x, gain):
    return x * lax.rsqrt(jnp.mean(x * x, axis=-1, keepdims=True) + EPS) * gain


def _silu(x):
    return x * jax.nn.sigmoid(x)


def _bdot(a, b):
    return jnp.dot(a.astype(BF16), b.astype(BF16), preferred_element_type=F32)


def _norm_kernel(x_ref, g_ref, o_ref):
    o_ref[...] = _rms(x_ref[...].astype(F32), g_ref[...]).astype(o_ref.dtype)


def _rms_norm_rows(x, gain, tm):
    m, d = x.shape
    return pl.pallas_call(
        _norm_kernel,
        out_shape=jax.ShapeDtypeStruct((m, d), BF16),
        grid=(m // tm,),
        in_specs=[pl.BlockSpec((tm, d), lambda i: (i, 0)), pl.BlockSpec((1, d), lambda i: (0, 0))],
        out_specs=pl.BlockSpec((tm, d), lambda i: (i, 0)),
        compiler_params=_params("parallel"),
        name="rms_norm_rows",
    )(x, gain.reshape(1, d))


def _rope_table_kernel(pos_ref, invf_ref, cos_ref, sin_ref):
    ang = pos_ref[...].astype(F32) * invf_ref[...]
    lane = lax.broadcasted_iota(I32, ang.shape, 1)
    sin = jnp.sin(ang)
    cos_ref[...] = jnp.cos(ang)
    sin_ref[...] = jnp.where(lane < HEAD_DIM // 2, -sin, sin)


def _rope_tables(positions, tm):
    n = positions.size
    half = HEAD_DIM // 2
    inv_freq = ROPE_THETA ** (-jnp.arange(half, dtype=F32) / half)
    invf = jnp.concatenate([inv_freq, inv_freq]).reshape(1, HEAD_DIM)
    return pl.pallas_call(
        _rope_table_kernel,
        out_shape=[jax.ShapeDtypeStruct((n, HEAD_DIM), F32)] * 2,
        grid=(n // tm,),
        in_specs=[pl.BlockSpec((tm, 1), lambda i: (i, 0)), pl.BlockSpec((1, HEAD_DIM), lambda i: (0, 0))],
        out_specs=[pl.BlockSpec((tm, HEAD_DIM), lambda i: (i, 0))] * 2,
        compiler_params=_params("parallel"),
        name="rope_tables",
    )(positions.reshape(n, 1), invf)


def _proj(a, w, epilogue, *, n0, n, tn, tm, out_dtypes, extras=(), name):
    m, k = a.shape
    assert n0 % tn == 0 and n % tn == 0 and m % tm == 0, (n0, n, tn, m, tm)
    j0 = n0 // tn
    n_extra = len(extras)

    def kern(a_ref, w_ref, *refs):
        acc = jnp.dot(a_ref[...], w_ref[...], preferred_element_type=F32)
        epilogue(acc, refs[:n_extra], refs[n_extra:])

    in_specs = [pl.BlockSpec((tm, k), lambda i, j: (i, 0)),
                pl.BlockSpec((k, tn), lambda i, j: (0, j + j0))]
    in_specs += [pl.BlockSpec(bs, im) for _, bs, im in extras]
    return pl.pallas_call(
        kern,
        out_shape=[jax.ShapeDtypeStruct((m, n), dt) for dt in out_dtypes],
        grid=(m // tm, n // tn),
        in_specs=in_specs,
        out_specs=[pl.BlockSpec((tm, tn), lambda i, j: (i, j)) for _ in out_dtypes],
        compiler_params=_params("parallel", "arbitrary"),
        name=name,
    )(a, w, *[e[0] for e in extras])


def _ep_plain(acc, ins, outs):
    outs[0][...] = acc.astype(outs[0].dtype)


def _ep_silu(acc, ins, outs):
    outs[0][...] = _silu(acc).astype(outs[0].dtype)


def _ep_sigmoid(acc, ins, outs):
    outs[0][...] = jax.nn.sigmoid(acc).astype(outs[0].dtype)


def _ep_qk(scale, acc, ins, outs):
    cos, sin, gain = ins[0][...], ins[1][...], ins[2][...]
    for h in range(acc.shape[1] // HEAD_DIM):
        sl = slice(h * HEAD_DIM, (h + 1) * HEAD_DIM)
        y = _rms(acc[:, sl], gain)
        y = y * cos + pltpu.roll(y, HEAD_DIM // 2, axis=1) * sin
        outs[0][:, sl] = (y * scale).astype(outs[0].dtype)


def _ep_headnorm(width, scale, acc, ins, outs):
    gain = ins[0][...]
    for h in range(acc.shape[1] // width):
        sl = slice(h * width, (h + 1) * width)
        outs[0][:, sl] = (_rms(acc[:, sl], gain) * scale).astype(outs[0].dtype)


def _ep_log_forget(layer, acc, ins, outs):
    raw = ins[0][...]
    mx = jnp.max(raw, axis=0, keepdims=True)
    ex = jnp.exp(raw - mx)
    lb = jnp.sum(ex[:layer + 1], axis=0, keepdims=True) / jnp.sum(ex, axis=0, keepdims=True)
    outs[0][...] = jnp.log(lb + (1.0 - lb) * jax.nn.sigmoid(acc))


def _dil_attn_kernel(*refs, seq):
    o_ref = refs[-1]
    groups = [refs[3 * g:3 * g + 3] for g in range(len(DILATIONS))]
    blk = DIL_BACK

    def body(qi, carry):
        t0 = pl.multiple_of(qi * blk, blk)
        outs, lses = [], []
        for (q_ref, k_ref, v_ref), d in zip(groups, DILATIONS):
            width = min(blk * (d + 1), seq)
            ks = pl.multiple_of(jnp.clip(t0 + blk - width, 0, seq - width), blk)
            q = q_ref[pl.ds(t0, blk), :]
            k = k_ref[pl.ds(ks, width), :]
            v = v_ref[pl.ds(ks, width), :]
            s = lax.dot_general(q, k, NT, preferred_element_type=F32)
            rel = ((t0 - ks) + lax.broadcasted_iota(I32, s.shape, 0)
                   - lax.broadcasted_iota(I32, s.shape, 1))
            mask = (rel >= 0) & (rel <= DIL_BACK * d) & ((rel & (d - 1)) == 0)
            s = jnp.where(mask, s, -jnp.inf)
            m = jnp.max(s, axis=-1, keepdims=True)
            p = jnp.exp(s - m)
            den = jnp.sum(p, axis=-1, keepdims=True)
            outs.append(jnp.dot((p / den).astype(BF16), v, preferred_element_type=F32))
            lses.append(m + jnp.log(den))
        top = functools.reduce(jnp.maximum, lses)
        ws = [jnp.exp(l - top) for l in lses]
        ya = sum(w * o for w, o in zip(ws, outs)) / sum(ws)
        o_ref[pl.ds(t0, blk), :] = ya.astype(o_ref.dtype)
        return carry

    lax.fori_loop(0, seq // blk, body, 0)


def _dilated_attention(q, k, v):
    b, s, _ = q.shape
    in_specs, args = [], []
    for g in range(len(DILATIONS)):
        for t in (q, k, v):
            in_specs.append(pl.BlockSpec((None, s, HEAD_DIM),
                                         lambda bi, h, g=g: (bi, 0, g * HEADS_PER_GROUP + h)))
            args.append(t)
    return pl.pallas_call(
        functools.partial(_dil_attn_kernel, seq=s),
        out_shape=jax.ShapeDtypeStruct((b, s, A_OUT), BF16),
        grid=(b, HEADS_PER_GROUP),
        in_specs=in_specs,
        out_specs=pl.BlockSpec((None, s, HEAD_DIM), lambda bi, h: (bi, 0, h)),
        compiler_params=_params("parallel", "parallel"),
        name="dilated_attention",
    )(*args)


def _hgrn_kernel(q_ref, lf_ref, v_ref, gate_ref, gain_ref, o_ref, st_ref, q_scr, v_scr, l_scr, d_scr, *,
                 seq, heads):
    C, R, W = HG_CHUNK, HG_STRIP, HG_DIM
    ns = C // R
    st_ref[...] = jnp.zeros_like(st_ref)
    gain = gain_ref[...]
    row = lax.broadcasted_iota(I32, (C, W), 0)
    far_rows = R * ns * (ns - 1) // 2
    rr = lax.broadcasted_iota(I32, (far_rows, C), 0)
    cc = lax.broadcasted_iota(I32, (far_rows, C), 1)
    keep, start = None, 0
    for j in range(ns - 1):
        size = (ns - 1 - j) * R
        blk = (rr >= start) & (rr < start + size) & (cc >= j * R) & (cc < (j + 1) * R)
        keep = blk if keep is None else keep | blk
        start += size

    def chunk(hd, r0):
        cols = slice(hd * W, (hd + 1) * W)
        lf = lf_ref[pl.ds(r0, C), cols]
        G = lf
        step = 1
        while step < C:
            G = G + jnp.where(row >= step, pltpu.roll(G, step, axis=0), 0.0)
            step *= 2
        kf = 1.0 - jnp.exp(lf)
        qf = q_ref[pl.ds(r0, C), cols].astype(F32)
        v = v_ref[pl.ds(r0, C), cols]
        vf = v.astype(F32)
        st = st_ref[hd]
        o = lax.dot_general((qf * jnp.exp(G)).astype(BF16), st.astype(BF16), NT,
                            preferred_element_type=F32)
        q_scr[hd] = qf
        v_scr[hd] = vf
        l_scr[hd] = lf
        strided = lambda ref, t: ref[hd, pl.ds(t, ns, stride=R), :]
        lt = [strided(l_scr, t) for t in range(R)]
        kt_ = [1.0 - jnp.exp(l) for l in lt]
        vt_ = [strided(v_scr, t) for t in range(R)]
        pre = [lt[0]]
        for t in range(1, R):
            pre.append(pre[-1] + lt[t])
        for t in range(R):
            qt_ = strided(q_scr, t)
            acc = jnp.sum(qt_ * kt_[t], axis=-1, keepdims=True) * vt_[t]
            for s in range(t):
                a = jnp.sum(qt_ * kt_[s] * jnp.exp(pre[t] - pre[s]), axis=-1, keepdims=True)
                acc = acc + a * vt_[s]
            d_scr[hd, pl.ds(t, ns, stride=R), :] = acc
        o = o + d_scr[hd]
        G3, k3 = G.reshape(ns, R, W), kf.reshape(ns, R, W)
        kt = (k3 * jnp.exp(G3[:, R - 1:R] - G3)).reshape(C, W).astype(BF16)
        qt = jnp.concatenate([qf[j * R:] * jnp.exp(G[j * R:] - G[j * R - 1:j * R]) for j in range(1, ns)],
                             axis=0).astype(BF16)
        a = lax.dot_general(qt, kt, NT, preferred_element_type=F32)
        far = jnp.dot(jnp.where(keep, a, 0.0).astype(BF16), v, preferred_element_type=F32)
        pieces = [o[i * R:(i + 1) * R] for i in range(ns)]
        start = 0
        for j in range(ns - 1):
            for i in range(j + 1, ns):
                pieces[i] = pieces[i] + far[start:start + R]
                start += R
        o = jnp.concatenate(pieces, axis=0)
        y = _rms(o, gain) * _silu(gate_ref[pl.ds(r0, C), cols].astype(F32))
        o_ref[pl.ds(r0, C), cols] = y.astype(o_ref.dtype)
        g_last = G[C - 1:C]
        kd = (kf * jnp.exp(g_last - G)).astype(BF16)
        st_ref[hd] = st * jnp.exp(g_last) + lax.dot_general(v, kd, TN, preferred_element_type=F32)

    def body(c, carry):
        r0 = pl.multiple_of(c * C, C)
        for hd in range(heads):
            chunk(hd, r0)
        return carry

    lax.fori_loop(0, seq // C, body, 0)


def _hgrn2(q, logf, v, gate, out_gain, heads):
    b, s, width = q.shape
    spec = pl.BlockSpec((None, s, heads * HG_DIM), lambda bi, h: (bi, 0, h))
    return pl.pallas_call(
        functools.partial(_hgrn_kernel, seq=s, heads=heads),
        out_shape=jax.ShapeDtypeStruct((b, s, width), BF16),
        grid=(b, width // (heads * HG_DIM)),
        in_specs=[spec, spec, spec, spec, pl.BlockSpec((1, HG_DIM), lambda bi, h: (0, 0))],
        out_specs=spec,
        scratch_shapes=[pltpu.VMEM((heads, HG_DIM, HG_DIM), F32)]
        + [pltpu.VMEM((heads, HG_CHUNK, HG_DIM), F32)] * 4,
        compiler_params=_params("parallel", "parallel"),
        name="hgrn2",
    )(q, logf, v, gate, out_gain.reshape(1, HG_DIM))


def _xattn_kernel(q_ref, k_ref, v_ref, o_ref):
    s = lax.dot_general(q_ref[...], k_ref[...], NT, preferred_element_type=F32)
    p = jnp.exp(s - jnp.max(s, axis=-1, keepdims=True))
    p = p / jnp.sum(p, axis=-1, keepdims=True)
    o_ref[...] = jnp.dot(p.astype(BF16), v_ref[...], preferred_element_type=F32).astype(o_ref.dtype)


def _memory_attention(q, k, v, ts):
    b, s, _ = q.shape
    m = k.shape[1]
    return pl.pallas_call(
        _xattn_kernel,
        out_shape=jax.ShapeDtypeStruct((b, s, X_WIDTH), BF16),
        grid=(b, X_HEADS, s // ts),
        in_specs=[pl.BlockSpec((None, ts, X_HEAD_DIM), lambda bi, h, si: (bi, si, h)),
                  pl.BlockSpec((None, m, X_HEAD_DIM), lambda bi, h, si: (bi, 0, h)),
                  pl.BlockSpec((None, m, X_HEAD_DIM), lambda bi, h, si: (bi, 0, h))],
        out_specs=pl.BlockSpec((None, ts, X_HEAD_DIM), lambda bi, h, si: (bi, si, h)),
        compiler_params=_params("parallel", "parallel", "parallel"),
        name="memory_attention",
    )(q, k, v)


def _merge_kernel(ya_ref, yb_ref, yc_ref, wa_ref, wb_ref, wc_ref, ga_ref, gb_ref, gc_ref, o_ref):
    acc = ga_ref[...].astype(F32) * jnp.dot(ya_ref[...], wa_ref[...], preferred_element_type=F32)
    acc += gb_ref[...].astype(F32) * jnp.dot(yb_ref[...], wb_ref[...], preferred_element_type=F32)
    acc += gc_ref[...].astype(F32) * jnp.dot(yc_ref[...], wc_ref[...], preferred_element_type=F32)
    o_ref[...] = acc.astype(o_ref.dtype)


def _merge_branches(ya, yb, yc, wa, wb, wc, gates, tm, tn):
    n, d = yb.shape[0], wa.shape[1]
    nj = d // tn
    row = lambda a: pl.BlockSpec((tm, a.shape[1]), lambda i, j: (i, 0))
    col = lambda w: pl.BlockSpec((w.shape[0], tn), lambda i, j: (0, j))
    gate = lambda br: pl.BlockSpec((tm, tn), lambda i, j: (i, br * nj + j))
    return pl.pallas_call(
        _merge_kernel,
        out_shape=jax.ShapeDtypeStruct((n, d), BF16),
        grid=(n // tm, nj),
        in_specs=[row(ya), row(yb), row(yc), col(wa), col(wb), col(wc), gate(0), gate(1), gate(2)],
        out_specs=pl.BlockSpec((tm, tn), lambda i, j: (i, j)),
        compiler_params=_params("parallel", "arbitrary"),
        name="merge_branches",
    )(ya, yb, yc, wa, wb, wc, gates, gates, gates)


def _outproj_kernel(m_ref, w_ref, x_ref, g_ref, wr_ref, h_ref, hn_ref, lg_ref):
    h = x_ref[...] + jnp.dot(m_ref[...], w_ref[...], preferred_element_type=F32)
    hn = _rms(h, g_ref[...])
    h_ref[...] = h
    hn_ref[...] = hn
    lg_ref[...] = _bdot(hn, wr_ref[...])


def _out_projection(merged, w_out, x, ffn_gain, w_router, tm):
    n, d = x.shape
    e = w_router.shape[1]
    row = pl.BlockSpec((tm, d), lambda i: (i, 0))
    const = lambda a: pl.BlockSpec(a.shape, lambda i: (0, 0))
    g = ffn_gain.reshape(1, d)
    return pl.pallas_call(
        _outproj_kernel,
        out_shape=[jax.ShapeDtypeStruct((n, d), F32), jax.ShapeDtypeStruct((n, d), F32),
                   jax.ShapeDtypeStruct((n, e), F32)],
        grid=(n // tm,),
        in_specs=[row, const(w_out), row, const(g), const(w_router)],
        out_specs=[row, row, pl.BlockSpec((tm, e), lambda i: (i, 0))],
        compiler_params=_params("parallel"),
        name="out_projection",
    )(merged, w_out, x, g, w_router)


def _route_kernel(lg_ref, bias_ref, idx_ref, wt_ref, rank_ref, cnt_ref, carry_ref):
    i = pl.program_id(0)

    @pl.when(i == 0)
    def _():
        carry_ref[...] = jnp.zeros_like(carry_ref)

    scores = jax.nn.sigmoid(lg_ref[...])
    tm, e = scores.shape
    sel = scores + bias_ref[...]
    lane = lax.broadcasted_iota(I32, (tm, e), 1)
    hots, ws = [], []
    for _ in range(TOP_K):
        mx = jnp.max(sel, axis=-1, keepdims=True)
        ix = jnp.min(jnp.where(sel == mx, lane, e), axis=-1, keepdims=True)
        hot = lane == ix
        hots.append(hot)
        ws.append(jnp.sum(jnp.where(hot, scores, 0.0), axis=-1, keepdims=True))
        sel = jnp.where(hot, -jnp.inf, sel)
    chosen = functools.reduce(jnp.logical_or, hots).astype(F32)
    r = lax.broadcasted_iota(I32, (tm, tm), 0)
    c = lax.broadcasted_iota(I32, (tm, tm), 1)
    before = (c < r).astype(BF16)
    ahead = jnp.dot(before, chosen.astype(BF16), preferred_element_type=F32) + carry_ref[...]
    carry_ref[...] += jnp.sum(chosen, axis=0, keepdims=True)
    cnt_ref[...] = carry_ref[...].astype(I32)

    norm = ROUTED_SCALE / sum(ws)
    out_lane = lax.broadcasted_iota(I32, (tm, LANES), 1)
    idx_o = jnp.zeros((tm, LANES), I32)
    wt_o = jnp.zeros((tm, LANES), F32)
    rk_o = jnp.zeros((tm, LANES), I32)
    for j in range(TOP_K):
        ix = jnp.sum(jnp.where(hots[j], lane, 0), axis=-1, keepdims=True)
        rk = jnp.sum(jnp.where(hots[j], ahead, 0.0), axis=-1, keepdims=True).astype(I32)
        idx_o = jnp.where(out_lane == j, ix, idx_o)
        wt_o = jnp.where(out_lane == j, ws[j] * norm, wt_o)
        rk_o = jnp.where(out_lane == j, rk, rk_o)
    idx_ref[...] = idx_o
    wt_ref[...] = wt_o
    rank_ref[...] = rk_o


def _route(logits, bias, tm):
    n, e = logits.shape
    row = pl.BlockSpec((tm, LANES), lambda i: (i, 0))
    return pl.pallas_call(
        _route_kernel,
        out_shape=[jax.ShapeDtypeStruct((n, LANES), I32), jax.ShapeDtypeStruct((n, LANES), F32),
                   jax.ShapeDtypeStruct((n, LANES), I32), jax.ShapeDtypeStruct((1, e), I32)],
        grid=(n // tm,),
        in_specs=[pl.BlockSpec((tm, e), lambda i: (i, 0)), pl.BlockSpec((1, e), lambda i: (0, 0))],
        out_specs=[row, row, row, pl.BlockSpec((1, e), lambda i: (0, 0))],
        scratch_shapes=[pltpu.VMEM((1, e), F32)],
        compiler_params=_params("arbitrary"),
        name="route",
    )(logits, bias.reshape(1, e))


def _dispatch_kernel(tail_ref, pos_ref, hn_ref, xs_ref, zeros, sem, zsem):
    tm = hn_ref.shape[0]
    half = zeros.shape[0]
    sub = 8

    @pl.when(pl.program_id(0) == 0)
    def _():
        zeros[...] = jnp.zeros_like(zeros)

        def zero_rows(off, size):
            return pltpu.make_async_copy(zeros.at[pl.ds(0, size), :], xs_ref.at[pl.ds(off, size), :], zsem)

        def tail_copies(e):
            off, pad = tail_ref[e], tail_ref[N_EXPERTS + e]
            head = jnp.minimum((-off) & (sub - 1), pad)
            for k in range(sub - 1):
                yield k < head, zero_rows(off + k, 1)
            off = pl.multiple_of(off + head, sub)
            rest = pad - head
            size = half
            while size >= sub:
                yield (rest & size) != 0, zero_rows(off, size)
                off = pl.multiple_of(off + (rest & size), sub)
                size //= 2

        def start(e, carry):
            for needed, copy in tail_copies(e):
                pl.when(needed)(copy.start)
            return carry

        def wait(e, carry):
            for needed, copy in tail_copies(e):
                pl.when(needed)(copy.wait)
            return carry

        lax.fori_loop(0, N_EXPERTS, start, 0)
        lax.fori_loop(0, N_EXPERTS, wait, 0)

    def row_copy(r, slot):
        return pltpu.make_async_copy(hn_ref.at[pl.ds(r, 1), :], xs_ref.at[pl.ds(slot, 1), :], sem)

    def issue(r, carry):
        for j in range(TOP_K):
            row_copy(r, pos_ref[r * TOP_K + j]).start()
        return carry

    lax.fori_loop(0, tm, issue, 0)

    def drain(r, carry):
        for j in range(TOP_K):
            row_copy(r, 0).wait()
        return carry

    lax.fori_loop(0, tm, drain, 0)


def _dispatch(hn, pos_flat, tails, n_blocks, tm):
    n, d = hn.shape
    blk = MOE_BLOCK
    return pl.pallas_call(
        _dispatch_kernel,
        out_shape=jax.ShapeDtypeStruct((n_blocks * blk, d), F32),
        grid_spec=pltpu.PrefetchScalarGridSpec(
            num_scalar_prefetch=1,
            grid=(n // tm,),
            in_specs=[pl.BlockSpec((tm * TOP_K,), lambda i, tl: (i,), memory_space=pltpu.SMEM),
                      pl.BlockSpec((tm, d), lambda i, tl: (i, 0))],
            out_specs=pl.BlockSpec(memory_space=pl.ANY),
            scratch_shapes=[pltpu.VMEM((blk // 2, d), F32), pltpu.SemaphoreType.DMA,
                            pltpu.SemaphoreType.DMA]),
        compiler_params=pltpu.CompilerParams(dimension_semantics=("arbitrary",),
                                             vmem_limit_bytes=VMEM_LIMIT, has_side_effects=True),
        name="moe_dispatch",
    )(tails, pos_flat, hn)


def _expert_kernel(be_ref, nu_ref, x_ref, wg_ref, wu_ref, wd_ref, y_ref, wg_b, wu_b, wd_b):
    b = pl.program_id(0)

    @pl.when(b < nu_ref[0])
    def _():
        @pl.when((b == 0) | (be_ref[b] != be_ref[jnp.maximum(b - 1, 0)]))
        def _():
            wg_b[...] = wg_ref[...].astype(BF16)
            wu_b[...] = wu_ref[...].astype(BF16)
            wd_b[...] = wd_ref[...].astype(BF16)

        x = x_ref[...].astype(BF16)
        g = jnp.dot(x, wg_b[...], preferred_element_type=F32)
        u = jnp.dot(x, wu_b[...], preferred_element_type=F32)
        y_ref[...] = _bdot(_silu(g) * u, wd_b[...])


def _experts(xs, block_expert, n_used, n_blocks, wg, wu, wd):
    d = xs.shape[1]
    ff = wg.shape[2]
    blk = MOE_BLOCK
    used = lambda b, be, nu: (jnp.minimum(b, nu[0] - 1), 0)
    return pl.pallas_call(
        _expert_kernel,
        out_shape=jax.ShapeDtypeStruct((n_blocks * blk, d), F32),
        grid_spec=pltpu.PrefetchScalarGridSpec(
            num_scalar_prefetch=2,
            grid=(n_blocks,),
            in_specs=[pl.BlockSpec((blk, d), used),
                      pl.BlockSpec((None, d, ff), lambda b, be, nu: (be[b], 0, 0)),
                      pl.BlockSpec((None, d, ff), lambda b, be, nu: (be[b], 0, 0)),
                      pl.BlockSpec((None, ff, d), lambda b, be, nu: (be[b], 0, 0))],
            out_specs=pl.BlockSpec((blk, d), used),
            scratch_shapes=[pltpu.VMEM((d, ff), BF16), pltpu.VMEM((d, ff), BF16),
                            pltpu.VMEM((ff, d), BF16)]),
        compiler_params=_params("arbitrary"),
        name="moe_experts",
    )(block_expert, n_used, xs, wg, wu, wd)


def _combine_kernel(pos_ref, h_ref, hn_ref, wt_ref, sg_ref, su_ref, sd_ref, y_ref, o_ref, buf, sem):
    tm = h_ref.shape[0]

    def row_copy(r, j, slot):
        return pltpu.make_async_copy(y_ref.at[pl.ds(slot, 1), :], buf.at[j, pl.ds(r, 1), :], sem)

    def issue(r, carry):
        for j in range(TOP_K):
            row_copy(r, j, pos_ref[r * TOP_K + j]).start()
        return carry

    lax.fori_loop(0, tm, issue, 0)

    hn = hn_ref[...].astype(BF16)
    g = jnp.dot(hn, sg_ref[...], preferred_element_type=F32)
    u = jnp.dot(hn, su_ref[...], preferred_element_type=F32)
    acc = h_ref[...] + _bdot(_silu(g) * u, sd_ref[...])

    def drain(r, carry):
        for j in range(TOP_K):
            row_copy(r, j, 0).wait()
        return carry

    lax.fori_loop(0, tm, drain, 0)

    wt = wt_ref[...]
    for j in range(TOP_K):
        acc = acc + wt[:, j:j + 1] * buf[j]
    o_ref[...] = acc


def _combine(h, hn, wts, pos_flat, y, sg, su, sd, tm):
    n, d = h.shape
    row = pl.BlockSpec((tm, d), lambda i: (i, 0))
    const = lambda a: pl.BlockSpec(a.shape, lambda i: (0, 0))
    return pl.pallas_call(
        _combine_kernel,
        out_shape=jax.ShapeDtypeStruct((n, d), F32),
        grid=(n // tm,),
        in_specs=[pl.BlockSpec((tm * TOP_K,), lambda i: (i,), memory_space=pltpu.SMEM),
                  row, row, pl.BlockSpec((tm, LANES), lambda i: (i, 0)),
                  const(sg), const(su), const(sd), pl.BlockSpec(memory_space=pl.ANY)],
        out_specs=row,
        scratch_shapes=[pltpu.VMEM((TOP_K, tm, d), F32), pltpu.SemaphoreType.DMA],
        compiler_params=_params("arbitrary"),
        name="moe_combine",
    )(pos_flat, h, hn, wts, sg, su, sd, y)


def _tile(n, want):
    t = min(n, want)
    assert n % t == 0, (n, want)
    return t


def _layer(layer, h, mem, positions, mix_norm, w_in, a_q_gain, a_k_gain, lower_raw, hg_out_gain, x_q_gain,
           x_k_gain, mem_norm, w_mem_kv, w_branch_a, w_branch_b, w_branch_c, w_out, ffn_norm, w_router,
           router_bias, w_exp_gate, w_exp_up, w_exp_down, w_sh_gate, w_sh_up, w_sh_down):
    b, s, d = h.shape
    n = b * s
    x2 = h.reshape(n, d)
    tm = _tile(n, 1024)
    bf = lambda w: w.astype(BF16)

    xn = _rms_norm_rows(x2, mix_norm, _tile(n, 512))
    cos, sin = _rope_tables(positions, _tile(n, 1024))
    w_in = bf(w_in)
    rope_in = lambda gain: ((cos, (tm, HEAD_DIM), lambda i, j: (i, 0)),
                            (sin, (tm, HEAD_DIM), lambda i, j: (i, 0)),
                            (gain.reshape(1, HEAD_DIM), (1, HEAD_DIM), lambda i, j: (0, 0)))
    proj = functools.partial(_proj, xn, w_in, tm=tm)
    hg_w = lower_raw.shape[1]
    off = 0
    aq, = proj(functools.partial(_ep_qk, HEAD_DIM ** -0.5), n0=off, n=A_WIDTH, tn=512,
               out_dtypes=[BF16], extras=rope_in(a_q_gain), name="proj_aq")
    off += A_WIDTH
    ak, = proj(functools.partial(_ep_qk, 1.0), n0=off, n=A_WIDTH, tn=512,
               out_dtypes=[BF16], extras=rope_in(a_k_gain), name="proj_ak")
    off += A_WIDTH
    av, = proj(_ep_plain, n0=off, n=A_WIDTH, tn=512, out_dtypes=[BF16], name="proj_av")
    off += A_WIDTH
    hq, = proj(_ep_silu, n0=off, n=hg_w, tn=512, out_dtypes=[BF16], name="proj_hq")
    off += hg_w
    logf, = proj(functools.partial(_ep_log_forget, layer), n0=off, n=hg_w, tn=512, out_dtypes=[F32],
                 extras=((lower_raw, (lower_raw.shape[0], 512), lambda i, j: (0, j)),), name="proj_hf")
    off += hg_w
    hiv, = proj(_ep_plain, n0=off, n=hg_w, tn=512, out_dtypes=[BF16], name="proj_hi")
    off += hg_w
    hgate, = proj(_ep_plain, n0=off, n=hg_w, tn=512, out_dtypes=[BF16], name="proj_hgate")
    off += hg_w
    xq, = _proj(xn, w_in[:, off:off + X_WIDTH], functools.partial(_ep_headnorm, X_HEAD_DIM, X_HEAD_DIM ** -0.5),
                n0=0, n=X_WIDTH, tn=2 * X_HEAD_DIM, tm=tm, out_dtypes=[BF16],
                extras=((x_q_gain.reshape(1, X_HEAD_DIM), (1, X_HEAD_DIM), lambda i, j: (0, 0)),),
                name="proj_xq")
    off += X_WIDTH
    gates, = proj(_ep_sigmoid, n0=off, n=3 * d, tn=512, out_dtypes=[BF16], name="proj_gates")

    r3 = lambda t: t.reshape(b, s, t.shape[1])
    ya = _dilated_attention(r3(aq), r3(ak), r3(av)).reshape(n, A_OUT)
    yb = _hgrn2(r3(hq), r3(logf), r3(hiv), r3(hgate), hg_out_gain, HG_HEADS_PER_STEP).reshape(n, hg_w)
    nm = mem.shape[0] * mem.shape[1]
    mem_n = _rms_norm_rows(mem.reshape(nm, d), mem_norm, _tile(nm, 512))
    w_kv = bf(w_mem_kv)
    tmm = _tile(nm, 1024)
    kn, = _proj(mem_n, w_kv, functools.partial(_ep_headnorm, X_HEAD_DIM, 1.0), n0=0, n=X_WIDTH,
                tn=2 * X_HEAD_DIM, tm=tmm, out_dtypes=[BF16],
                extras=((x_k_gain.reshape(1, X_HEAD_DIM), (1, X_HEAD_DIM), lambda i, j: (0, 0)),),
                name="proj_mem_k")
    vm, = _proj(mem_n, w_kv, _ep_plain, n0=X_WIDTH, n=X_WIDTH, tn=2 * X_HEAD_DIM, tm=tmm,
                out_dtypes=[BF16], name="proj_mem_v")
    rm = lambda t: t.reshape(b, mem.shape[1], X_WIDTH)
    yc = _memory_attention(r3(xq), rm(kn), rm(vm), _tile(s, 512)).reshape(n, X_WIDTH)

    merged = _merge_branches(ya, yb, yc, bf(w_branch_a), bf(w_branch_b), bf(w_branch_c), gates, tm, 512)
    h1, hn, logits = _out_projection(merged, bf(w_out), x2, ffn_norm, bf(w_router), _tile(n, 256))

    tr = _tile(n, 256)
    idx, wts, rank, counts = _route(logits, router_bias, tr)
    blk = MOE_BLOCK
    counts = counts[0]
    padded = (counts + blk - 1) // blk * blk
    ends = jnp.cumsum(padded)
    starts = ends - padded
    pos = (starts[idx[:, :TOP_K]] + rank[:, :TOP_K]).reshape(-1).astype(I32)
    n_blocks = -(-(n * TOP_K + N_EXPERTS * (blk - 1)) // blk)
    n_used = (ends[-1] // blk).astype(I32)
    blocks = jnp.minimum(jnp.arange(n_blocks, dtype=I32), n_used - 1)
    block_expert = jnp.minimum(jnp.sum(ends[None, :] <= blocks[:, None] * blk, axis=1), N_EXPERTS - 1)
    tails = jnp.concatenate([starts + counts, padded - counts]).astype(I32)
    xs = _dispatch(hn, pos, tails, n_blocks, tr)
    y = _experts(xs, block_expert.astype(I32), n_used.reshape(1), n_blocks, w_exp_gate, w_exp_up,
                 w_exp_down)
    out = _combine(h1, hn, wts, pos, y, bf(w_sh_gate), bf(w_sh_up), bf(w_sh_down), _tile(n, 128))
    return out.reshape(b, s, d)


def kernel(x, mem, positions, mix_norm, w_in, a_q_gain, a_k_gain, hg_lower_bounds, hg_out_gain, x_q_gain, x_k_gain, mem_norm, w_mem_kv, w_branch_a, w_branch_b, w_branch_c, w_out, ffn_norm, w_router, router_bias, w_exp_gate, w_exp_up, w_exp_down, w_sh_gate, w_sh_up, w_sh_down):
    h = x
    for layer in range(w_in.shape[0]):
        h = _layer(layer, h, mem, positions, mix_norm[layer], w_in[layer], a_q_gain[layer], a_k_gain[layer],
                   hg_lower_bounds, hg_out_gain[layer], x_q_gain[layer], x_k_gain[layer], mem_norm[layer],
                   w_mem_kv[layer], w_branch_a[layer], w_branch_b[layer], w_branch_c[layer], w_out[layer],
                   ffn_norm[layer], w_router[layer], router_bias[layer], w_exp_gate[layer],
                   w_exp_up[layer], w_exp_down[layer], w_sh_gate[layer], w_sh_up[layer], w_sh_down[layer])
    return h
```

```python
import functools
import math

import jax
import jax.numpy as jnp
from jax import lax
from jax.experimental import pallas as pl
from jax.experimental.pallas import tpu as pltpu

F32, BF16, I32, U32 = jnp.float32, jnp.bfloat16, jnp.int32, jnp.uint32

EPS = 1e-6
ROPE_THETA = 10000.0
HEAD_DIM = 128
DILATIONS = (1, 4, 16)
DIL_BACK = 128
HEADS_PER_GROUP = 4
A_WIDTH = len(DILATIONS) * HEADS_PER_GROUP * HEAD_DIM
A_OUT = HEADS_PER_GROUP * HEAD_DIM
HG_DIM = 128
HG_CHUNK = 64
HG_STRIP = 8
HG_HEADS_PER_STEP = 4
X_HEADS = 4
X_HEAD_DIM = 384
X_WIDTH = X_HEADS * X_HEAD_DIM
N_EXPERTS = 64
TOP_K = 8
ROUTED_SCALE = 2.5
MOE_BLOCK = 256
PROJ_SUB_ROWS = 256
DMA_THREADS = 2
LANES = 128
VMEM_LIMIT = 56 * 1024 * 1024

NT = (((1,), (1,)), ((), ()))
TN = (((0,), (0,)), ((), ()))


def _params(*sem):
    return pltpu.CompilerParams(dimension_semantics=sem, vmem_limit_bytes=VMEM_LIMIT)


def _rms(x, gain):
    return x * lax.rsqrt(jnp.mean(x * x, axis=-1, keepdims=True) + EPS) * gain


def _silu(x):
    return x * jax.nn.sigmoid(x)


def _bdot(a, b):
    return jnp.dot(a.astype(BF16), b.astype(BF16), preferred_element_type=F32)


def _norm_kernel(x_ref, g_ref, o_ref):
    o_ref[...] = _rms(x_ref[...].astype(F32), g_ref[...]).astype(o_ref.dtype)


def _rms_norm_rows(x, gain, tm):
    m, d = x.shape
    return pl.pallas_call(
        _norm_kernel,
        out_shape=jax.ShapeDtypeStruct((m, d), BF16),
        grid=(m // tm,),
        in_specs=[pl.BlockSpec((tm, d), lambda i: (i, 0)), pl.BlockSpec((1, d), lambda i: (0, 0))],
        out_specs=pl.BlockSpec((tm, d), lambda i: (i, 0)),
        compiler_params=_params("parallel"),
        name="rms_norm_rows",
    )(x, gain.reshape(1, d))


def _rope_table_kernel(pos_ref, invf_ref, cos_ref, sin_ref):
    ang = pos_ref[...].astype(F32) * invf_ref[...]
    lane = lax.broadcasted_iota(I32, ang.shape, 1)
    sin = jnp.sin(ang)
    cos_ref[...] = jnp.cos(ang)
    sin_ref[...] = jnp.where(lane < HEAD_DIM // 2, -sin, sin)


def _rope_tables(positions, tm):
    n = positions.size
    half = HEAD_DIM // 2
    inv_freq = ROPE_THETA ** (-jnp.arange(half, dtype=F32) / half)
    invf = jnp.concatenate([inv_freq, inv_freq]).reshape(1, HEAD_DIM)
    return pl.pallas_call(
        _rope_table_kernel,
        out_shape=[jax.ShapeDtypeStruct((n, HEAD_DIM), F32)] * 2,
        grid=(n // tm,),
        in_specs=[pl.BlockSpec((tm, 1), lambda i: (i, 0)), pl.BlockSpec((1, HEAD_DIM), lambda i: (0, 0))],
        out_specs=[pl.BlockSpec((tm, HEAD_DIM), lambda i: (i, 0))] * 2,
        compiler_params=_params("parallel"),
        name="rope_tables",
    )(positions.reshape(n, 1), invf)


def _proj(a, w, epilogue, *, n0, n, tn, tm, out_dtypes, extras=(), name):
    m, k = a.shape
    assert n0 % tn == 0 and n % tn == 0 and m % tm == 0, (n0, n, tn, m, tm)
    j0 = n0 // tn
    n_extra = len(extras)
    sub = min(tm, PROJ_SUB_ROWS)

    def kern(a_ref, w_ref, *refs):
        for c in range(tm // sub):
            rows = slice(c * sub, (c + 1) * sub)
            acc = jnp.dot(a_ref[rows, :], w_ref[...], preferred_element_type=F32)
            epilogue(acc, rows, refs[:n_extra], refs[n_extra:])

    in_specs = [pl.BlockSpec((tm, k), lambda i, j: (i, 0)),
                pl.BlockSpec((k, tn), lambda i, j: (0, j + j0))]
    in_specs += [pl.BlockSpec(bs, im) for _, bs, im in extras]
    return pl.pallas_call(
        kern,
        out_shape=[jax.ShapeDtypeStruct((m, n), dt) for dt in out_dtypes],
        grid=(m // tm, n // tn),
        in_specs=in_specs,
        out_specs=[pl.BlockSpec((tm, tn), lambda i, j: (i, j)) for _ in out_dtypes],
        compiler_params=_params("parallel", "arbitrary"),
        name=name,
    )(a, w, *[e[0] for e in extras])


def _ep_plain(acc, rows, ins, outs):
    outs[0][rows, :] = acc.astype(outs[0].dtype)


def _ep_silu(acc, rows, ins, outs):
    outs[0][rows, :] = _silu(acc).astype(outs[0].dtype)


def _ep_sigmoid(acc, rows, ins, outs):
    outs[0][rows, :] = jax.nn.sigmoid(acc).astype(outs[0].dtype)


def _head_qk(scale, x, rows, ins):
    y = _rms(x, ins[2][...])
    return (y * ins[0][rows, :] + pltpu.roll(y, HEAD_DIM // 2, axis=1) * ins[1][rows, :]) * scale


def _head_plain(x, rows, ins):
    return x


def _proj_dilated(a, w, head_fn, *, n0, tm, seq, extras=(), name):
    m, k = a.shape
    tn = A_OUT
    assert n0 % tn == 0 and seq % tm == 0 and m % seq == 0, (n0, tn, seq, tm, m)
    j0 = n0 // tn
    tiles = seq // tm
    n_extra = len(extras)
    sub = min(tm, PROJ_SUB_ROWS)
    n_groups = len(DILATIONS)

    def kern(a_ref, w_ref, *refs):
        ins, outs, scr = refs[:n_extra], refs[n_extra:n_extra + n_groups], refs[-1]
        for g, d in enumerate(DILATIONS):
            @pl.when(pl.program_id(1) == g)
            def _(g=g, d=d):
                per = sub // d
                for c in range(tm // sub):
                    rows = slice(c * sub, (c + 1) * sub)
                    acc = jnp.dot(a_ref[rows, :], w_ref[...], preferred_element_type=F32)
                    for h in range(HEADS_PER_GROUP):
                        cols = slice(h * HEAD_DIM, (h + 1) * HEAD_DIM)
                        scr[c, h] = head_fn(acc[:, cols], rows, ins)
                        for r in range(d):
                            outs[g][r, c * per:(c + 1) * per, cols] = (
                                scr[c, h, pl.ds(r, per, stride=d), :].astype(outs[g].dtype))

    in_specs = [pl.BlockSpec((tm, k), lambda i, j: (i, 0)),
                pl.BlockSpec((k, tn), lambda i, j: (0, j + j0))]
    in_specs += [pl.BlockSpec(bs, im) for _, bs, im in extras]
    return pl.pallas_call(
        kern,
        out_shape=[jax.ShapeDtypeStruct((m // seq, d, seq // d, tn), BF16) for d in DILATIONS],
        grid=(m // tm, n_groups),
        in_specs=in_specs,
        out_specs=[pl.BlockSpec((None, d, tm // d, tn), lambda i, j: (i // tiles, 0, i % tiles, 0))
                   for d in DILATIONS],
        scratch_shapes=[pltpu.VMEM((tm // sub, HEADS_PER_GROUP, sub, HEAD_DIM), F32)],
        compiler_params=_params("parallel", "arbitrary"),
        name=name,
    )(a, w, *[e[0] for e in extras])


def _ep_headnorm(width, scale, acc, rows, ins, outs):
    gain = ins[0][...]
    for h in range(acc.shape[1] // width):
        sl = slice(h * width, (h + 1) * width)
        outs[0][rows, sl] = (_rms(acc[:, sl], gain) * scale).astype(outs[0].dtype)


def _ep_log_forget(layer, acc, rows, ins, outs):
    raw = ins[0][...]
    mx = jnp.max(raw, axis=0, keepdims=True)
    ex = jnp.exp(raw - mx)
    lb = jnp.sum(ex[:layer + 1], axis=0, keepdims=True) / jnp.sum(ex, axis=0, keepdims=True)
    outs[0][rows, :] = jnp.log(lb + (1.0 - lb) * jax.nn.sigmoid(acc))


def _dil_attn_kernel(*refs, seq):
    n_groups = len(DILATIONS)
    o_ref, o_scr, l_scr = refs[3 * n_groups:]
    blk = DIL_BACK

    for g, d in enumerate(DILATIONS):
        q_ref, k_ref, v_ref = refs[3 * g:3 * g + 3]
        length = seq // d
        nqb = max(length // blk, 1)
        qrows = min(blk, length)
        width = min(2 * blk, length)

        for r in range(d):
            for qb in range(nqb):
                q0 = qb * qrows
                ks = min(max(q0 + qrows - width, 0), length - width)
                q = q_ref[r, q0:q0 + qrows, :]
                k = k_ref[r, ks:ks + width, :]
                v = v_ref[r, ks:ks + width, :]
                s = lax.dot_general(q, k, NT, preferred_element_type=F32)
                rel = ((q0 - ks) + lax.broadcasted_iota(I32, s.shape, 0)
                       - lax.broadcasted_iota(I32, s.shape, 1))
                s = jnp.where((rel >= 0) & (rel <= DIL_BACK), s, -jnp.inf)
                m = jnp.max(s, axis=-1, keepdims=True)
                p = jnp.exp(s - m)
                den = jnp.sum(p, axis=-1, keepdims=True)
                o = jnp.dot((p / den).astype(BF16), v, preferred_element_type=F32)
                rows = pl.ds(r + d * q0, qrows, stride=d)
                o_scr[g, rows, :] = o
                l_scr[g, rows, :] = jnp.broadcast_to(m + jnp.log(den), o.shape)

    def merge(qi, carry):
        rows = pl.ds(pl.multiple_of(qi * blk, blk), blk)
        lses = [l_scr[g, rows, :] for g in range(n_groups)]
        top = functools.reduce(jnp.maximum, lses)
        ws = [jnp.exp(l - top) for l in lses]
        ya = sum(w * o_scr[g, rows, :] for g, w in enumerate(ws)) / sum(ws)
        o_ref[rows, :] = ya.astype(o_ref.dtype)
        return carry

    lax.fori_loop(0, seq // blk, merge, 0)


def _dilated_attention(qkv, b, s):
    in_specs, args = [], []
    for g, d in enumerate(DILATIONS):
        for t in qkv[g]:
            in_specs.append(pl.BlockSpec((None, d, s // d, HEAD_DIM), lambda bi, h: (bi, 0, 0, h)))
            args.append(t)
    return pl.pallas_call(
        functools.partial(_dil_attn_kernel, seq=s),
        out_shape=jax.ShapeDtypeStruct((b, s, A_OUT), BF16),
        grid=(b, HEADS_PER_GROUP),
        in_specs=in_specs,
        out_specs=pl.BlockSpec((None, s, HEAD_DIM), lambda bi, h: (bi, 0, h)),
        scratch_shapes=[pltpu.VMEM((len(DILATIONS), s, HEAD_DIM), F32)] * 2,
        compiler_params=_params("parallel", "parallel"),
        name="dilated_attention",
    )(*args)


def _hgrn_kernel(q_ref, lf_ref, v_ref, gate_ref, gain_ref, o_ref, st_ref, q_scr, v_scr, l_scr, d_scr, *,
                 seq, heads):
    C, R, W = HG_CHUNK, HG_STRIP, HG_DIM
    ns = C // R
    st_ref[...] = jnp.zeros_like(st_ref)
    gain = gain_ref[...]
    row = lax.broadcasted_iota(I32, (C, W), 0)
    far_rows = R * ns * (ns - 1) // 2
    rr = lax.broadcasted_iota(I32, (far_rows, C), 0)
    cc = lax.broadcasted_iota(I32, (far_rows, C), 1)
    keep, start = None, 0
    for j in range(ns - 1):
        size = (ns - 1 - j) * R
        blk = (rr >= start) & (rr < start + size) & (cc >= j * R) & (cc < (j + 1) * R)
        keep = blk if keep is None else keep | blk
        start += size

    def chunk(hd, r0):
        cols = slice(hd * W, (hd + 1) * W)
        lf = lf_ref[pl.ds(r0, C), cols]
        G = lf
        step = 1
        while step < C:
            G = G + jnp.where(row >= step, pltpu.roll(G, step, axis=0), 0.0)
            step *= 2
        kf = 1.0 - jnp.exp(lf)
        qf = q_ref[pl.ds(r0, C), cols].astype(F32)
        v = v_ref[pl.ds(r0, C), cols]
        vf = v.astype(F32)
        st = st_ref[hd]
        o = lax.dot_general((qf * jnp.exp(G)).astype(BF16), st.astype(BF16), NT,
                            preferred_element_type=F32)
        q_scr[hd] = qf
        v_scr[hd] = vf
        l_scr[hd] = lf
        strided = lambda ref, t: ref[hd, pl.ds(t, ns, stride=R), :]
        lt = [strided(l_scr, t) for t in range(R)]
        kt_ = [1.0 - jnp.exp(l) for l in lt]
        vt_ = [strided(v_scr, t) for t in range(R)]
        pre = [lt[0]]
        for t in range(1, R):
            pre.append(pre[-1] + lt[t])
        for t in range(R):
            qt_ = strided(q_scr, t)
            acc = jnp.sum(qt_ * kt_[t], axis=-1, keepdims=True) * vt_[t]
            for s in range(t):
                a = jnp.sum(qt_ * kt_[s] * jnp.exp(pre[t] - pre[s]), axis=-1, keepdims=True)
                acc = acc + a * vt_[s]
            d_scr[hd, pl.ds(t, ns, stride=R), :] = acc
        o = o + d_scr[hd]
        G3, k3 = G.reshape(ns, R, W), kf.reshape(ns, R, W)
        kt = (k3 * jnp.exp(G3[:, R - 1:R] - G3)).reshape(C, W).astype(BF16)
        qt = jnp.concatenate([qf[j * R:] * jnp.exp(G[j * R:] - G[j * R - 1:j * R]) for j in range(1, ns)],
                             axis=0).astype(BF16)
        a = lax.dot_general(qt, kt, NT, preferred_element_type=F32)
        far = jnp.dot(jnp.where(keep, a, 0.0).astype(BF16), v, preferred_element_type=F32)
        pieces = [o[i * R:(i + 1) * R] for i in range(ns)]
        start = 0
        for j in range(ns - 1):
            for i in range(j + 1, ns):
                pieces[i] = pieces[i] + far[start:start + R]
                start += R
        o = jnp.concatenate(pieces, axis=0)
        y = _rms(o, gain) * _silu(gate_ref[pl.ds(r0, C), cols].astype(F32))
        o_ref[pl.ds(r0, C), cols] = y.astype(o_ref.dtype)
        g_last = G[C - 1:C]
        kd = (kf * jnp.exp(g_last - G)).astype(BF16)
        st_ref[hd] = st * jnp.exp(g_last) + lax.dot_general(v, kd, TN, preferred_element_type=F32)

    def body(c, carry):
        r0 = pl.multiple_of(c * C, C)
        for hd in range(heads):
            chunk(hd, r0)
        return carry

    lax.fori_loop(0, seq // C, body, 0)


def _hgrn2(q, logf, v, gate, out_gain, heads):
    b, s, width = q.shape
    spec = pl.BlockSpec((None, s, heads * HG_DIM), lambda bi, h: (bi, 0, h))
    return pl.pallas_call(
        functools.partial(_hgrn_kernel, seq=s, heads=heads),
        out_shape=jax.ShapeDtypeStruct((b, s, width), BF16),
        grid=(b, width // (heads * HG_DIM)),
        in_specs=[spec, spec, spec, spec, pl.BlockSpec((1, HG_DIM), lambda bi, h: (0, 0))],
        out_specs=spec,
        scratch_shapes=[pltpu.VMEM((heads, HG_DIM, HG_DIM), F32)]
        + [pltpu.VMEM((heads, HG_CHUNK, HG_DIM), F32)] * 4,
        compiler_params=_params("parallel", "parallel"),
        name="hgrn2",
    )(q, logf, v, gate, out_gain.reshape(1, HG_DIM))


def _xattn_kernel(q_ref, k_ref, v_ref, o_ref):
    s = lax.dot_general(q_ref[...], k_ref[...], NT, preferred_element_type=F32)
    p = jnp.exp(s - jnp.max(s, axis=-1, keepdims=True))
    p = p / jnp.sum(p, axis=-1, keepdims=True)
    o_ref[...] = jnp.dot(p.astype(BF16), v_ref[...], preferred_element_type=F32).astype(o_ref.dtype)


def _memory_attention(q, k, v, ts):
    b, s, _ = q.shape
    m = k.shape[1]
    return pl.pallas_call(
        _xattn_kernel,
        out_shape=jax.ShapeDtypeStruct((b, s, X_WIDTH), BF16),
        grid=(b, X_HEADS, s // ts),
        in_specs=[pl.BlockSpec((None, ts, X_HEAD_DIM), lambda bi, h, si: (bi, si, h)),
                  pl.BlockSpec((None, m, X_HEAD_DIM), lambda bi, h, si: (bi, 0, h)),
                  pl.BlockSpec((None, m, X_HEAD_DIM), lambda bi, h, si: (bi, 0, h))],
        out_specs=pl.BlockSpec((None, ts, X_HEAD_DIM), lambda bi, h, si: (bi, si, h)),
        compiler_params=_params("parallel", "parallel", "parallel"),
        name="memory_attention",
    )(q, k, v)


def _merge_kernel(ya_ref, yb_ref, yc_ref, wa_ref, wb_ref, wc_ref, ga_ref, gb_ref, gc_ref, o_ref):
    acc = ga_ref[...].astype(F32) * jnp.dot(ya_ref[...], wa_ref[...], preferred_element_type=F32)
    acc += gb_ref[...].astype(F32) * jnp.dot(yb_ref[...], wb_ref[...], preferred_element_type=F32)
    acc += gc_ref[...].astype(F32) * jnp.dot(yc_ref[...], wc_ref[...], preferred_element_type=F32)
    o_ref[...] = acc.astype(o_ref.dtype)


def _merge_branches(ya, yb, yc, wa, wb, wc, gates, tm, tn):
    n, d = yb.shape[0], wa.shape[1]
    nj = d // tn
    row = lambda a: pl.BlockSpec((tm, a.shape[1]), lambda i, j: (i, 0))
    col = lambda w: pl.BlockSpec((w.shape[0], tn), lambda i, j: (0, j))
    gate = lambda br: pl.BlockSpec((tm, tn), lambda i, j: (i, br * nj + j))
    return pl.pallas_call(
        _merge_kernel,
        out_shape=jax.ShapeDtypeStruct((n, d), BF16),
        grid=(n // tm, nj),
        in_specs=[row(ya), row(yb), row(yc), col(wa), col(wb), col(wc), gate(0), gate(1), gate(2)],
        out_specs=pl.BlockSpec((tm, tn), lambda i, j: (i, j)),
        compiler_params=_params("parallel", "arbitrary"),
        name="merge_branches",
    )(ya, yb, yc, wa, wb, wc, gates, gates, gates)


def _pack_halves(x):
    c = x.shape[1] // 2
    bits = lambda t: lax.bitcast_convert_type(t.astype(BF16).astype(F32), U32)
    return bits(x[:, c:]) | (bits(x[:, :c]) >> 16)


def _unpack_halves(w):
    return (lax.bitcast_convert_type(w << 16, F32),
            lax.bitcast_convert_type(w & jnp.uint32(0xFFFF0000), F32))


def _outproj_kernel(m_ref, w_ref, x_ref, g_ref, wr_ref, h_ref, hn_ref, lg_ref):
    h = x_ref[...] + jnp.dot(m_ref[...], w_ref[...], preferred_element_type=F32)
    hn = _rms(h, g_ref[...])
    h_ref[...] = h
    hn_ref[...] = _pack_halves(hn)
    lg_ref[...] = _bdot(hn, wr_ref[...])


def _out_projection(merged, w_out, x, ffn_gain, w_router, tm):
    n, d = x.shape
    e = w_router.shape[1]
    row = pl.BlockSpec((tm, d), lambda i: (i, 0))
    const = lambda a: pl.BlockSpec(a.shape, lambda i: (0, 0))
    g = ffn_gain.reshape(1, d)
    return pl.pallas_call(
        _outproj_kernel,
        out_shape=[jax.ShapeDtypeStruct((n, d), F32), jax.ShapeDtypeStruct((n, d // 2), U32),
                   jax.ShapeDtypeStruct((n, e), F32)],
        grid=(n // tm,),
        in_specs=[row, const(w_out), row, const(g), const(w_router)],
        out_specs=[row, pl.BlockSpec((tm, d // 2), lambda i: (i, 0)), pl.BlockSpec((tm, e), lambda i: (i, 0))],
        compiler_params=_params("parallel"),
        name="out_projection",
    )(merged, w_out, x, g, w_router)


def _route_kernel(lg_ref, bias_ref, idx_ref, wt_ref, rank_ref, cnt_ref, carry_ref):
    i = pl.program_id(0)

    @pl.when(i == 0)
    def _():
        carry_ref[...] = jnp.zeros_like(carry_ref)

    scores = jax.nn.sigmoid(lg_ref[...])
    tm, e = scores.shape
    sel = scores + bias_ref[...]
    lane = lax.broadcasted_iota(I32, (tm, e), 1)
    hots, ws = [], []
    for _ in range(TOP_K):
        mx = jnp.max(sel, axis=-1, keepdims=True)
        ix = jnp.min(jnp.where(sel == mx, lane, e), axis=-1, keepdims=True)
        hot = lane == ix
        hots.append(hot)
        ws.append(jnp.sum(jnp.where(hot, scores, 0.0), axis=-1, keepdims=True))
        sel = jnp.where(hot, -jnp.inf, sel)
    chosen = functools.reduce(jnp.logical_or, hots).astype(F32)
    r = lax.broadcasted_iota(I32, (tm, tm), 0)
    c = lax.broadcasted_iota(I32, (tm, tm), 1)
    before = (c < r).astype(BF16)
    ahead = jnp.dot(before, chosen.astype(BF16), preferred_element_type=F32) + carry_ref[...]
    carry_ref[...] += jnp.sum(chosen, axis=0, keepdims=True)
    cnt_ref[...] = carry_ref[...].astype(I32)

    norm = ROUTED_SCALE / sum(ws)
    out_lane = lax.broadcasted_iota(I32, (tm, LANES), 1)
    idx_o = jnp.zeros((tm, LANES), I32)
    wt_o = jnp.zeros((tm, LANES), F32)
    rk_o = jnp.zeros((tm, LANES), I32)
    for j in range(TOP_K):
        ix = jnp.sum(jnp.where(hots[j], lane, 0), axis=-1, keepdims=True)
        rk = jnp.sum(jnp.where(hots[j], ahead, 0.0), axis=-1, keepdims=True).astype(I32)
        idx_o = jnp.where(out_lane == j, ix, idx_o)
        wt_o = jnp.where(out_lane == j, ws[j] * norm, wt_o)
        rk_o = jnp.where(out_lane == j, rk, rk_o)
    idx_ref[...] = idx_o
    wt_ref[...] = wt_o
    rank_ref[...] = rk_o


def _route(logits, bias, tm):
    n, e = logits.shape
    row = pl.BlockSpec((tm, LANES), lambda i: (i, 0))
    return pl.pallas_call(
        _route_kernel,
        out_shape=[jax.ShapeDtypeStruct((n, LANES), I32), jax.ShapeDtypeStruct((n, LANES), F32),
                   jax.ShapeDtypeStruct((n, LANES), I32), jax.ShapeDtypeStruct((1, e), I32)],
        grid=(n // tm,),
        in_specs=[pl.BlockSpec((tm, e), lambda i: (i, 0)), pl.BlockSpec((1, e), lambda i: (0, 0))],
        out_specs=[row, row, row, pl.BlockSpec((1, e), lambda i: (0, 0))],
        scratch_shapes=[pltpu.VMEM((1, e), F32)],
        compiler_params=_params("arbitrary"),
        name="route",
    )(logits, bias.reshape(1, e))


def _dispatch_kernel(tail_ref, pos_ref, hn_ref, xs_ref, zeros, sem, zsem):
    tm = hn_ref.shape[0]
    half = zeros.shape[0]
    sub = 8

    @pl.when(pl.program_id(0) == 0)
    def _():
        zeros[...] = jnp.zeros_like(zeros)

        def zero_rows(off, size):
            return pltpu.make_async_copy(zeros.at[pl.ds(0, size), :], xs_ref.at[pl.ds(off, size), :], zsem)

        def tail_copies(e):
            off, pad = tail_ref[e], tail_ref[N_EXPERTS + e]
            head = jnp.minimum((-off) & (sub - 1), pad)
            for k in range(sub - 1):
                yield k < head, zero_rows(off + k, 1)
            off = pl.multiple_of(off + head, sub)
            rest = pad - head
            size = half
            while size >= sub:
                yield (rest & size) != 0, zero_rows(off, size)
                off = pl.multiple_of(off + (rest & size), sub)
                size //= 2

        def start(e, carry):
            for needed, copy in tail_copies(e):
                pl.when(needed)(copy.start)
            return carry

        def wait(e, carry):
            for needed, copy in tail_copies(e):
                pl.when(needed)(copy.wait)
            return carry

        lax.fori_loop(0, N_EXPERTS, start, 0)
        lax.fori_loop(0, N_EXPERTS, wait, 0)

    def row_copy(r, slot):
        return pltpu.make_async_copy(hn_ref.at[pl.ds(r, 1), :], xs_ref.at[pl.ds(slot, 1), :], sem)

    def issue(r, carry):
        for j in range(TOP_K):
            row_copy(r, pos_ref[r * TOP_K + j]).start(priority=j % DMA_THREADS)
        return carry

    lax.fori_loop(0, tm, issue, 0)

    def drain(r, carry):
        for j in range(TOP_K):
            row_copy(r, 0).wait()
        return carry

    lax.fori_loop(0, tm, drain, 0)


def _dispatch(hn, pos_flat, tails, n_blocks, tm):
    n, d = hn.shape
    blk = MOE_BLOCK
    return pl.pallas_call(
        _dispatch_kernel,
        out_shape=jax.ShapeDtypeStruct((n_blocks * blk, d), hn.dtype),
        grid_spec=pltpu.PrefetchScalarGridSpec(
            num_scalar_prefetch=1,
            grid=(n // tm,),
            in_specs=[pl.BlockSpec((tm * TOP_K,), lambda i, tl: (i,), memory_space=pltpu.SMEM),
                      pl.BlockSpec((tm, d), lambda i, tl: (i, 0))],
            out_specs=pl.BlockSpec(memory_space=pl.ANY),
            scratch_shapes=[pltpu.VMEM((blk // 2, d), hn.dtype), pltpu.SemaphoreType.DMA,
                            pltpu.SemaphoreType.DMA]),
        compiler_params=pltpu.CompilerParams(dimension_semantics=("arbitrary",),
                                             vmem_limit_bytes=VMEM_LIMIT, has_side_effects=True),
        name="moe_dispatch",
    )(tails, pos_flat, hn)


def _expert_kernel(be_ref, nu_ref, x_ref, wg_ref, wu_ref, wd_ref, y_ref, wg_b, wu_b, wd_b):
    b = pl.program_id(0)

    @pl.when(b < nu_ref[0])
    def _():
        @pl.when((b == 0) | (be_ref[b] != be_ref[jnp.maximum(b - 1, 0)]))
        def _():
            wg_b[...] = wg_ref[...].astype(BF16)
            wu_b[...] = wu_ref[...].astype(BF16)
            wd_b[...] = wd_ref[...].astype(BF16)

        y_ref[...] = _pack_halves(_swiglu_packed(x_ref[...], wg_b, wu_b, wd_b))


def _swiglu_packed(x_packed, wg_ref, wu_ref, wd_ref):
    lo, hi = (t.astype(BF16) for t in _unpack_halves(x_packed))
    c = lo.shape[1]
    split_dot = lambda w: (jnp.dot(lo, w[:c, :], preferred_element_type=F32)
                           + jnp.dot(hi, w[c:, :], preferred_element_type=F32))
    return _bdot(_silu(split_dot(wg_ref)) * split_dot(wu_ref), wd_ref[...])


def _experts(xs, block_expert, n_used, n_blocks, wg, wu, wd):
    dh = xs.shape[1]
    d, ff = wg.shape[1], wg.shape[2]
    blk = MOE_BLOCK
    used = lambda b, be, nu: (jnp.minimum(b, nu[0] - 1), 0)
    return pl.pallas_call(
        _expert_kernel,
        out_shape=jax.ShapeDtypeStruct((n_blocks * blk, dh), xs.dtype),
        grid_spec=pltpu.PrefetchScalarGridSpec(
            num_scalar_prefetch=2,
            grid=(n_blocks,),
            in_specs=[pl.BlockSpec((blk, dh), used),
                      pl.BlockSpec((None, d, ff), lambda b, be, nu: (be[b], 0, 0)),
                      pl.BlockSpec((None, d, ff), lambda b, be, nu: (be[b], 0, 0)),
                      pl.BlockSpec((None, ff, d), lambda b, be, nu: (be[b], 0, 0))],
            out_specs=pl.BlockSpec((blk, dh), used),
            scratch_shapes=[pltpu.VMEM((d, ff), BF16), pltpu.VMEM((d, ff), BF16),
                            pltpu.VMEM((ff, d), BF16)]),
        compiler_params=_params("arbitrary"),
        name="moe_experts",
    )(block_expert, n_used, xs, wg, wu, wd)


def _combine_kernel(pos_ref, h_ref, hn_ref, wt_ref, sg_ref, su_ref, sd_ref, y_ref, o_ref, buf, sem):
    tm = h_ref.shape[0]

    def row_copy(r, j, slot):
        return pltpu.make_async_copy(y_ref.at[pl.ds(slot, 1), :], buf.at[j, pl.ds(r, 1), :], sem)

    def issue(r, carry):
        for j in range(TOP_K):
            row_copy(r, j, pos_ref[r * TOP_K + j]).start(priority=j % DMA_THREADS)
        return carry

    lax.fori_loop(0, tm, issue, 0)

    acc = h_ref[...] + _swiglu_packed(hn_ref[...], sg_ref, su_ref, sd_ref)
    c = acc.shape[1] // 2
    acc_lo, acc_hi = acc[:, :c], acc[:, c:]

    def drain(r, carry):
        for j in range(TOP_K):
            row_copy(r, j, 0).wait()
        return carry

    lax.fori_loop(0, tm, drain, 0)

    wt = wt_ref[...]
    for j in range(TOP_K):
        lo, hi = _unpack_halves(buf[j])
        acc_lo = acc_lo + wt[:, j:j + 1] * lo
        acc_hi = acc_hi + wt[:, j:j + 1] * hi
    o_ref[:, :c] = acc_lo
    o_ref[:, c:] = acc_hi


def _combine(h, hn, wts, pos_flat, y, sg, su, sd, tm):
    n, d = h.shape
    row = pl.BlockSpec((tm, d), lambda i: (i, 0))
    const = lambda a: pl.BlockSpec(a.shape, lambda i: (0, 0))
    return pl.pallas_call(
        _combine_kernel,
        out_shape=jax.ShapeDtypeStruct((n, d), F32),
        grid=(n // tm,),
        in_specs=[pl.BlockSpec((tm * TOP_K,), lambda i: (i,), memory_space=pltpu.SMEM),
                  row, pl.BlockSpec((tm, d // 2), lambda i: (i, 0)), pl.BlockSpec((tm, LANES), lambda i: (i, 0)),
                  const(sg), const(su), const(sd), pl.BlockSpec(memory_space=pl.ANY)],
        out_specs=row,
        scratch_shapes=[pltpu.VMEM((TOP_K, tm, d // 2), y.dtype), pltpu.SemaphoreType.DMA],
        compiler_params=_params("arbitrary"),
        name="moe_combine",
    )(pos_flat, h, hn, wts, sg, su, sd, y)


def _tile(n, want):
    t = min(n, want)
    assert n % t == 0, (n, want)
    return t


def _layer(layer, h, mem, positions, mix_norm, w_in, a_q_gain, a_k_gain, lower_raw, hg_out_gain, x_q_gain,
           x_k_gain, mem_norm, w_mem_kv, w_branch_a, w_branch_b, w_branch_c, w_out, ffn_norm, w_router,
           router_bias, w_exp_gate, w_exp_up, w_exp_down, w_sh_gate, w_sh_up, w_sh_down):
    b, s, d = h.shape
    n = b * s
    x2 = h.reshape(n, d)
    tm = _tile(n, 1024)
    bf = lambda w: w.astype(BF16)

    xn = _rms_norm_rows(x2, mix_norm, _tile(n, 512))
    cos, sin = _rope_tables(positions, _tile(n, 1024))
    w_in = bf(w_in)
    tmd = _tile(s, tm)
    rope_in = lambda gain: ((cos, (tmd, HEAD_DIM), lambda i, j: (i, 0)),
                            (sin, (tmd, HEAD_DIM), lambda i, j: (i, 0)),
                            (gain.reshape(1, HEAD_DIM), (1, HEAD_DIM), lambda i, j: (0, 0)))
    proj = functools.partial(_proj, xn, w_in, tm=tm)
    dilated = functools.partial(_proj_dilated, xn, w_in, tm=tmd, seq=s)
    hg_w = lower_raw.shape[1]
    off = 0
    aq = dilated(functools.partial(_head_qk, HEAD_DIM ** -0.5), n0=off, extras=rope_in(a_q_gain),
                 name="proj_aq")
    off += A_WIDTH
    ak = dilated(functools.partial(_head_qk, 1.0), n0=off, extras=rope_in(a_k_gain), name="proj_ak")
    off += A_WIDTH
    av = dilated(_head_plain, n0=off, name="proj_av")
    off += A_WIDTH
    hq, = proj(_ep_silu, n0=off, n=hg_w, tn=512, out_dtypes=[BF16], name="proj_hq")
    off += hg_w
    logf, = proj(functools.partial(_ep_log_forget, layer), n0=off, n=hg_w, tn=512, out_dtypes=[F32],
                 extras=((lower_raw, (lower_raw.shape[0], 512), lambda i, j: (0, j)),), name="proj_hf")
    off += hg_w
    hiv, = proj(_ep_plain, n0=off, n=hg_w, tn=512, out_dtypes=[BF16], name="proj_hi")
    off += hg_w
    hgate, = proj(_ep_plain, n0=off, n=hg_w, tn=512, out_dtypes=[BF16], name="proj_hgate")
    off += hg_w
    xq, = _proj(xn, w_in[:, off:off + X_WIDTH], functools.partial(_ep_headnorm, X_HEAD_DIM, X_HEAD_DIM ** -0.5),
                n0=0, n=X_WIDTH, tn=2 * X_HEAD_DIM, tm=tm, out_dtypes=[BF16],
                extras=((x_q_gain.reshape(1, X_HEAD_DIM), (1, X_HEAD_DIM), lambda i, j: (0, 0)),),
                name="proj_xq")
    off += X_WIDTH
    gates, = proj(_ep_sigmoid, n0=off, n=3 * d, tn=512, out_dtypes=[BF16], name="proj_gates")

    r3 = lambda t: t.reshape(b, s, t.shape[1])
    ya = _dilated_attention(list(zip(aq, ak, av)), b, s).reshape(n, A_OUT)
    yb = _hgrn2(r3(hq), r3(logf), r3(hiv), r3(hgate), hg_out_gain, HG_HEADS_PER_STEP).reshape(n, hg_w)
    nm = mem.shape[0] * mem.shape[1]
    mem_n = _rms_norm_rows(mem.reshape(nm, d), mem_norm, _tile(nm, 512))
    w_kv = bf(w_mem_kv)
    tmm = _tile(nm, 1024)
    kn, = _proj(mem_n, w_kv, functools.partial(_ep_headnorm, X_HEAD_DIM, 1.0), n0=0, n=X_WIDTH,
                tn=2 * X_HEAD_DIM, tm=tmm, out_dtypes=[BF16],
                extras=((x_k_gain.reshape(1, X_HEAD_DIM), (1, X_HEAD_DIM), lambda i, j: (0, 0)),),
                name="proj_mem_k")
    vm, = _proj(mem_n, w_kv, _ep_plain, n0=X_WIDTH, n=X_WIDTH, tn=2 * X_HEAD_DIM, tm=tmm,
                out_dtypes=[BF16], name="proj_mem_v")
    rm = lambda t: t.reshape(b, mem.shape[1], X_WIDTH)
    yc = _memory_attention(r3(xq), rm(kn), rm(vm), _tile(s, 512)).reshape(n, X_WIDTH)

    merged = _merge_branches(ya, yb, yc, bf(w_branch_a), bf(w_branch_b), bf(w_branch_c), gates, tm, 512)
    h1, hn, logits = _out_projection(merged, bf(w_out), x2, ffn_norm, bf(w_router), _tile(n, 256))

    tr = _tile(n, 256)
    idx, wts, rank, counts = _route(logits, router_bias, tr)
    blk = MOE_BLOCK
    counts = counts[0]
    padded = (counts + blk - 1) // blk * blk
    ends = jnp.cumsum(padded)
    starts = ends - padded
    pos = (starts[idx[:, :TOP_K]] + rank[:, :TOP_K]).reshape(-1).astype(I32)
    n_blocks = -(-(n * TOP_K + N_EXPERTS * (blk - 1)) // blk)
    n_used = (ends[-1] // blk).astype(I32)
    blocks = jnp.minimum(jnp.arange(n_blocks, dtype=I32), n_used - 1)
    block_expert = jnp.minimum(jnp.sum(ends[None, :] <= blocks[:, None] * blk, axis=1), N_EXPERTS - 1)
    tails = jnp.concatenate([starts + counts, padded - counts]).astype(I32)
    xs = _dispatch(hn, pos, tails, n_blocks, tr)
    y = _experts(xs, block_expert.astype(I32), n_used.reshape(1), n_blocks, w_exp_gate, w_exp_up,
                 w_exp_down)
    out = _combine(h1, hn, wts, pos, y, bf(w_sh_gate), bf(w_sh_up), bf(w_sh_down), _tile(n, 128))
    return out.reshape(b, s, d)


def kernel(x, mem, positions, mix_norm, w_in, a_q_gain, a_k_gain, hg_lower_bounds, hg_out_gain, x_q_gain, x_k_gain, mem_norm, w_mem_kv, w_branch_a, w_branch_b, w_branch_c, w_out, ffn_norm, w_router, router_bias, w_exp_gate, w_exp_up, w_exp_down, w_sh_gate, w_sh_up, w_sh_down):
    h = x
    for layer in range(w_in.shape[0]):
        h = _layer(layer, h, mem, positions, mix_norm[layer], w_in[layer], a_q_gain[layer], a_k_gain[layer],
                   hg_lower_bounds, hg_out_gain[layer], x_q_gain[layer], x_k_gain[layer], mem_norm[layer],
                   w_mem_kv[layer], w_branch_a[layer], w_branch_b[layer], w_branch_c[layer], w_out[layer],
                   ffn_norm[layer], w_router[layer], router_bias[layer], w_exp_gate[layer],
                   w_exp_up[layer], w_exp_down[layer], w_sh_gate[layer], w_sh_up[layer], w_sh_down[layer])
    return h
```

```python
import functools
import math

import jax
import jax.numpy as jnp
from jax import lax
from jax.experimental import pallas as pl
from jax.experimental.pallas import tpu as pltpu

F32, BF16, I32, U32 = jnp.float32, jnp.bfloat16, jnp.int32, jnp.uint32

EPS = 1e-6
ROPE_THETA = 10000.0
HEAD_DIM = 128
DILATIONS = (1, 4, 16)
DIL_BACK = 128
HEADS_PER_GROUP = 4
A_WIDTH = len(DILATIONS) * HEADS_PER_GROUP * HEAD_DIM
A_OUT = HEADS_PER_GROUP * HEAD_DIM
HG_DIM = 128
HG_CHUNK = 64
HG_STRIP = 8
HG_HEADS_PER_STEP = 4
X_HEADS = 4
X_HEAD_DIM = 384
X_WIDTH = X_HEADS * X_HEAD_DIM
N_EXPERTS = 64
TOP_K = 8
ROUTED_SCALE = 2.5
MOE_BLOCK = 256
PROJ_SUB_ROWS = 256
DMA_THREADS = 2
LANES = 128
ROW_TILE = 8
VMEM_LIMIT = 56 * 1024 * 1024

NT = (((1,), (1,)), ((), ()))
TN = (((0,), (0,)), ((), ()))


def _params(*sem):
    return pltpu.CompilerParams(dimension_semantics=sem, vmem_limit_bytes=VMEM_LIMIT)


def _rms(x, gain):
    return x * lax.rsqrt(jnp.mean(x * x, axis=-1, keepdims=True) + EPS) * gain


def _silu(x):
    return x * jax.nn.sigmoid(x)


def _bdot(a, b):
    return jnp.dot(a.astype(BF16), b.astype(BF16), preferred_element_type=F32)


def _norm_kernel(x_ref, g_ref, o_ref):
    o_ref[...] = _rms(x_ref[...].astype(F32), g_ref[...]).astype(o_ref.dtype)


def _rms_norm_rows(x, gain, tm):
    m, d = x.shape
    return pl.pallas_call(
        _norm_kernel,
        out_shape=jax.ShapeDtypeStruct((m, d), BF16),
        grid=(m // tm,),
        in_specs=[pl.BlockSpec((tm, d), lambda i: (i, 0)), pl.BlockSpec((1, d), lambda i: (0, 0))],
        out_specs=pl.BlockSpec((tm, d), lambda i: (i, 0)),
        compiler_params=_params("parallel"),
        name="rms_norm_rows",
    )(x, gain.reshape(1, d))


def _rope_table_kernel(pos_ref, invf_ref, cos_ref, sin_ref):
    ang = pos_ref[...].astype(F32) * invf_ref[...]
    lane = lax.broadcasted_iota(I32, ang.shape, 1)
    sin = jnp.sin(ang)
    cos_ref[...] = jnp.cos(ang)
    sin_ref[...] = jnp.where(lane < HEAD_DIM // 2, -sin, sin)


def _rope_tables(positions, tm):
    n = positions.size
    half = HEAD_DIM // 2
    inv_freq = ROPE_THETA ** (-jnp.arange(half, dtype=F32) / half)
    invf = jnp.concatenate([inv_freq, inv_freq]).reshape(1, HEAD_DIM)
    return pl.pallas_call(
        _rope_table_kernel,
        out_shape=[jax.ShapeDtypeStruct((n, HEAD_DIM), F32)] * 2,
        grid=(n // tm,),
        in_specs=[pl.BlockSpec((tm, 1), lambda i: (i, 0)), pl.BlockSpec((1, HEAD_DIM), lambda i: (0, 0))],
        out_specs=[pl.BlockSpec((tm, HEAD_DIM), lambda i: (i, 0))] * 2,
        compiler_params=_params("parallel"),
        name="rope_tables",
    )(positions.reshape(n, 1), invf)


def _proj(a, w, epilogue, *, n0, n, tn, tm, out_dtypes, extras=(), name):
    m, k = a.shape
    assert n0 % tn == 0 and n % tn == 0 and m % tm == 0, (n0, n, tn, m, tm)
    j0 = n0 // tn
    n_extra = len(extras)
    sub = min(tm, PROJ_SUB_ROWS)

    def kern(a_ref, w_ref, *refs):
        for c in range(tm // sub):
            rows = slice(c * sub, (c + 1) * sub)
            acc = jnp.dot(a_ref[rows, :], w_ref[...], preferred_element_type=F32)
            epilogue(acc, rows, refs[:n_extra], refs[n_extra:])

    in_specs = [pl.BlockSpec((tm, k), lambda i, j: (i, 0)),
                pl.BlockSpec((k, tn), lambda i, j: (0, j + j0))]
    in_specs += [pl.BlockSpec(bs, im) for _, bs, im in extras]
    return pl.pallas_call(
        kern,
        out_shape=[jax.ShapeDtypeStruct((m, n), dt) for dt in out_dtypes],
        grid=(m // tm, n // tn),
        in_specs=in_specs,
        out_specs=[pl.BlockSpec((tm, tn), lambda i, j: (i, j)) for _ in out_dtypes],
        compiler_params=_params("parallel", "arbitrary"),
        name=name,
    )(a, w, *[e[0] for e in extras])


def _ep_plain(acc, rows, ins, outs):
    outs[0][rows, :] = acc.astype(outs[0].dtype)


def _ep_silu(acc, rows, ins, outs):
    outs[0][rows, :] = _silu(acc).astype(outs[0].dtype)


def _ep_sigmoid(acc, rows, ins, outs):
    outs[0][rows, :] = jax.nn.sigmoid(acc).astype(outs[0].dtype)


def _head_qk(scale, x, rows, ins):
    y = _rms(x, ins[2][...])
    return (y * ins[0][rows, :] + pltpu.roll(y, HEAD_DIM // 2, axis=1) * ins[1][rows, :]) * scale


def _head_plain(x, rows, ins):
    return x


def _proj_dilated(a, w, head_fn, *, n0, tm, seq, extras=(), name):
    m, k = a.shape
    tn = A_OUT
    assert n0 % tn == 0 and seq % tm == 0 and m % seq == 0, (n0, tn, seq, tm, m)
    j0 = n0 // tn
    tiles = seq // tm
    n_extra = len(extras)
    sub = min(tm, PROJ_SUB_ROWS)
    n_groups = len(DILATIONS)

    def kern(a_ref, w_ref, *refs):
        ins, outs, scr = refs[:n_extra], refs[n_extra:n_extra + n_groups], refs[-1]
        for g, d in enumerate(DILATIONS):
            @pl.when(pl.program_id(1) == g)
            def _(g=g, d=d):
                per = sub // d
                for c in range(tm // sub):
                    rows = slice(c * sub, (c + 1) * sub)
                    acc = jnp.dot(a_ref[rows, :], w_ref[...], preferred_element_type=F32)
                    for h in range(HEADS_PER_GROUP):
                        cols = slice(h * HEAD_DIM, (h + 1) * HEAD_DIM)
                        scr[c, h] = head_fn(acc[:, cols], rows, ins)
                        for r in range(d):
                            outs[g][r, c * per:(c + 1) * per, cols] = (
                                scr[c, h, pl.ds(r, per, stride=d), :].astype(outs[g].dtype))

    in_specs = [pl.BlockSpec((tm, k), lambda i, j: (i, 0)),
                pl.BlockSpec((k, tn), lambda i, j: (0, j + j0))]
    in_specs += [pl.BlockSpec(bs, im) for _, bs, im in extras]
    return pl.pallas_call(
        kern,
        out_shape=[jax.ShapeDtypeStruct((m // seq, d, seq // d, tn), BF16) for d in DILATIONS],
        grid=(m // tm, n_groups),
        in_specs=in_specs,
        out_specs=[pl.BlockSpec((None, d, tm // d, tn), lambda i, j: (i // tiles, 0, i % tiles, 0))
                   for d in DILATIONS],
        scratch_shapes=[pltpu.VMEM((tm // sub, HEADS_PER_GROUP, sub, HEAD_DIM), F32)],
        compiler_params=_params("parallel", "arbitrary"),
        name=name,
    )(a, w, *[e[0] for e in extras])


def _ep_headnorm(width, scale, acc, rows, ins, outs):
    gain = ins[0][...]
    for h in range(acc.shape[1] // width):
        sl = slice(h * width, (h + 1) * width)
        outs[0][rows, sl] = (_rms(acc[:, sl], gain) * scale).astype(outs[0].dtype)


def _ep_log_forget(layer, acc, rows, ins, outs):
    raw = ins[0][...]
    mx = jnp.max(raw, axis=0, keepdims=True)
    ex = jnp.exp(raw - mx)
    lb = jnp.sum(ex[:layer + 1], axis=0, keepdims=True) / jnp.sum(ex, axis=0, keepdims=True)
    outs[0][rows, :] = jnp.log(lb + (1.0 - lb) * jax.nn.sigmoid(acc))


def _dil_attn_kernel(*refs, seq):
    n_groups = len(DILATIONS)
    o_ref, o_scr, l_scr = refs[3 * n_groups:]
    blk = DIL_BACK

    for g, d in enumerate(DILATIONS):
        q_ref, k_ref, v_ref = refs[3 * g:3 * g + 3]
        length = seq // d
        nqb = max(length // blk, 1)
        qrows = min(blk, length)
        width = min(2 * blk, length)

        for r in range(d):
            for qb in range(nqb):
                q0 = qb * qrows
                ks = min(max(q0 + qrows - width, 0), length - width)
                q = q_ref[r, q0:q0 + qrows, :]
                k = k_ref[r, ks:ks + width, :]
                v = v_ref[r, ks:ks + width, :]
                s = lax.dot_general(q, k, NT, preferred_element_type=F32)
                rel = ((q0 - ks) + lax.broadcasted_iota(I32, s.shape, 0)
                       - lax.broadcasted_iota(I32, s.shape, 1))
                s = jnp.where((rel >= 0) & (rel <= DIL_BACK), s, -jnp.inf)
                m = jnp.max(s, axis=-1, keepdims=True)
                p = jnp.exp(s - m)
                den = jnp.sum(p, axis=-1, keepdims=True)
                o = jnp.dot((p / den).astype(BF16), v, preferred_element_type=F32)
                rows = pl.ds(r + d * q0, qrows, stride=d)
                o_scr[g, rows, :] = o
                l_scr[g, rows, :] = jnp.broadcast_to(m + jnp.log(den), o.shape)

    def merge(qi, carry):
        rows = pl.ds(pl.multiple_of(qi * blk, blk), blk)
        lses = [l_scr[g, rows, :] for g in range(n_groups)]
        top = functools.reduce(jnp.maximum, lses)
        ws = [jnp.exp(l - top) for l in lses]
        ya = sum(w * o_scr[g, rows, :] for g, w in enumerate(ws)) / sum(ws)
        o_ref[rows, :] = ya.astype(o_ref.dtype)
        return carry

    lax.fori_loop(0, seq // blk, merge, 0)


def _dilated_attention(qkv, b, s):
    in_specs, args = [], []
    for g, d in enumerate(DILATIONS):
        for t in qkv[g]:
            in_specs.append(pl.BlockSpec((None, d, s // d, HEAD_DIM), lambda bi, h: (bi, 0, 0, h)))
            args.append(t)
    return pl.pallas_call(
        functools.partial(_dil_attn_kernel, seq=s),
        out_shape=jax.ShapeDtypeStruct((b, s, A_OUT), BF16),
        grid=(b, HEADS_PER_GROUP),
        in_specs=in_specs,
        out_specs=pl.BlockSpec((None, s, HEAD_DIM), lambda bi, h: (bi, 0, h)),
        scratch_shapes=[pltpu.VMEM((len(DILATIONS), s, HEAD_DIM), F32)] * 2,
        compiler_params=_params("parallel", "parallel"),
        name="dilated_attention",
    )(*args)


def _hgrn_kernel(q_ref, lf_ref, v_ref, gate_ref, gain_ref, o_ref, st_ref, q_scr, v_scr, l_scr, d_scr, *,
                 seq, heads):
    C, R, W = HG_CHUNK, HG_STRIP, HG_DIM
    ns = C // R
    st_ref[...] = jnp.zeros_like(st_ref)
    gain = gain_ref[...]
    row = lax.broadcasted_iota(I32, (C, W), 0)
    far_rows = R * ns * (ns - 1) // 2
    rr = lax.broadcasted_iota(I32, (far_rows, C), 0)
    cc = lax.broadcasted_iota(I32, (far_rows, C), 1)
    keep, start = None, 0
    for j in range(ns - 1):
        size = (ns - 1 - j) * R
        blk = (rr >= start) & (rr < start + size) & (cc >= j * R) & (cc < (j + 1) * R)
        keep = blk if keep is None else keep | blk
        start += size

    def chunk(hd, r0):
        cols = slice(hd * W, (hd + 1) * W)
        lf = lf_ref[pl.ds(r0, C), cols]
        G = lf
        step = 1
        while step < C:
            G = G + jnp.where(row >= step, pltpu.roll(G, step, axis=0), 0.0)
            step *= 2
        kf = 1.0 - jnp.exp(lf)
        qf = q_ref[pl.ds(r0, C), cols].astype(F32)
        v = v_ref[pl.ds(r0, C), cols]
        vf = v.astype(F32)
        st = st_ref[hd]
        o = lax.dot_general((qf * jnp.exp(G)).astype(BF16), st.astype(BF16), NT,
                            preferred_element_type=F32)
        q_scr[hd] = qf
        v_scr[hd] = vf
        l_scr[hd] = lf
        strided = lambda ref, t: ref[hd, pl.ds(t, ns, stride=R), :]
        lt = [strided(l_scr, t) for t in range(R)]
        kt_ = [1.0 - jnp.exp(l) for l in lt]
        vt_ = [strided(v_scr, t) for t in range(R)]
        pre = [lt[0]]
        for t in range(1, R):
            pre.append(pre[-1] + lt[t])
        for t in range(R):
            qt_ = strided(q_scr, t)
            acc = jnp.sum(qt_ * kt_[t], axis=-1, keepdims=True) * vt_[t]
            for s in range(t):
                a = jnp.sum(qt_ * kt_[s] * jnp.exp(pre[t] - pre[s]), axis=-1, keepdims=True)
                acc = acc + a * vt_[s]
            d_scr[hd, pl.ds(t, ns, stride=R), :] = acc
        o = o + d_scr[hd]
        G3, k3 = G.reshape(ns, R, W), kf.reshape(ns, R, W)
        kt = (k3 * jnp.exp(G3[:, R - 1:R] - G3)).reshape(C, W).astype(BF16)
        qt = jnp.concatenate([qf[j * R:] * jnp.exp(G[j * R:] - G[j * R - 1:j * R]) for j in range(1, ns)],
                             axis=0).astype(BF16)
        a = lax.dot_general(qt, kt, NT, preferred_element_type=F32)
        far = jnp.dot(jnp.where(keep, a, 0.0).astype(BF16), v, preferred_element_type=F32)
        pieces = [o[i * R:(i + 1) * R] for i in range(ns)]
        start = 0
        for j in range(ns - 1):
            for i in range(j + 1, ns):
                pieces[i] = pieces[i] + far[start:start + R]
                start += R
        o = jnp.concatenate(pieces, axis=0)
        y = _rms(o, gain) * _silu(gate_ref[pl.ds(r0, C), cols].astype(F32))
        o_ref[pl.ds(r0, C), cols] = y.astype(o_ref.dtype)
        g_last = G[C - 1:C]
        kd = (kf * jnp.exp(g_last - G)).astype(BF16)
        st_ref[hd] = st * jnp.exp(g_last) + lax.dot_general(v, kd, TN, preferred_element_type=F32)

    def body(c, carry):
        r0 = pl.multiple_of(c * C, C)
        for hd in range(heads):
            chunk(hd, r0)
        return carry

    lax.fori_loop(0, seq // C, body, 0)


def _hgrn2(q, logf, v, gate, out_gain, heads):
    b, s, width = q.shape
    spec = pl.BlockSpec((None, s, heads * HG_DIM), lambda bi, h: (bi, 0, h))
    return pl.pallas_call(
        functools.partial(_hgrn_kernel, seq=s, heads=heads),
        out_shape=jax.ShapeDtypeStruct((b, s, width), BF16),
        grid=(b, width // (heads * HG_DIM)),
        in_specs=[spec, spec, spec, spec, pl.BlockSpec((1, HG_DIM), lambda bi, h: (0, 0))],
        out_specs=spec,
        scratch_shapes=[pltpu.VMEM((heads, HG_DIM, HG_DIM), F32)]
        + [pltpu.VMEM((heads, HG_CHUNK, HG_DIM), F32)] * 4,
        compiler_params=_params("parallel", "parallel"),
        name="hgrn2",
    )(q, logf, v, gate, out_gain.reshape(1, HG_DIM))


def _xattn_kernel(q_ref, k_ref, v_ref, o_ref):
    s = lax.dot_general(q_ref[...], k_ref[...], NT, preferred_element_type=F32)
    p = jnp.exp(s - jnp.max(s, axis=-1, keepdims=True))
    p = p / jnp.sum(p, axis=-1, keepdims=True)
    o_ref[...] = jnp.dot(p.astype(BF16), v_ref[...], preferred_element_type=F32).astype(o_ref.dtype)


def _memory_attention(q, k, v, ts):
    b, s, _ = q.shape
    m = k.shape[1]
    return pl.pallas_call(
        _xattn_kernel,
        out_shape=jax.ShapeDtypeStruct((b, s, X_WIDTH), BF16),
        grid=(b, X_HEADS, s // ts),
        in_specs=[pl.BlockSpec((None, ts, X_HEAD_DIM), lambda bi, h, si: (bi, si, h)),
                  pl.BlockSpec((None, m, X_HEAD_DIM), lambda bi, h, si: (bi, 0, h)),
                  pl.BlockSpec((None, m, X_HEAD_DIM), lambda bi, h, si: (bi, 0, h))],
        out_specs=pl.BlockSpec((None, ts, X_HEAD_DIM), lambda bi, h, si: (bi, si, h)),
        compiler_params=_params("parallel", "parallel", "parallel"),
        name="memory_attention",
    )(q, k, v)


def _merge_kernel(ya_ref, yb_ref, yc_ref, wa_ref, wb_ref, wc_ref, ga_ref, gb_ref, gc_ref, o_ref):
    acc = ga_ref[...].astype(F32) * jnp.dot(ya_ref[...], wa_ref[...], preferred_element_type=F32)
    acc += gb_ref[...].astype(F32) * jnp.dot(yb_ref[...], wb_ref[...], preferred_element_type=F32)
    acc += gc_ref[...].astype(F32) * jnp.dot(yc_ref[...], wc_ref[...], preferred_element_type=F32)
    o_ref[...] = acc.astype(o_ref.dtype)


def _merge_branches(ya, yb, yc, wa, wb, wc, gates, tm, tn):
    n, d = yb.shape[0], wa.shape[1]
    nj = d // tn
    row = lambda a: pl.BlockSpec((tm, a.shape[1]), lambda i, j: (i, 0))
    col = lambda w: pl.BlockSpec((w.shape[0], tn), lambda i, j: (0, j))
    gate = lambda br: pl.BlockSpec((tm, tn), lambda i, j: (i, br * nj + j))
    return pl.pallas_call(
        _merge_kernel,
        out_shape=jax.ShapeDtypeStruct((n, d), BF16),
        grid=(n // tm, nj),
        in_specs=[row(ya), row(yb), row(yc), col(wa), col(wb), col(wc), gate(0), gate(1), gate(2)],
        out_specs=pl.BlockSpec((tm, tn), lambda i, j: (i, j)),
        compiler_params=_params("parallel", "arbitrary"),
        name="merge_branches",
    )(ya, yb, yc, wa, wb, wc, gates, gates, gates)


def _pack_halves(x):
    c = x.shape[1] // 2
    bits = lambda t: lax.bitcast_convert_type(t.astype(BF16).astype(F32), U32)
    return bits(x[:, c:]) | (bits(x[:, :c]) >> 16)


def _unpack_halves(w):
    return (lax.bitcast_convert_type(w << 16, F32),
            lax.bitcast_convert_type(w & jnp.uint32(0xFFFF0000), F32))


def _store_row_tiles(ref, packed):
    m = packed.shape[0]
    assert packed.shape[1] == ROW_TILE * LANES, packed.shape
    for a in range(ROW_TILE):
        ref[pl.ds(a, m, stride=ROW_TILE), :] = packed[:, a * LANES:(a + 1) * LANES]


def _load_row_tiles(ref, m):
    return jnp.concatenate([ref[pl.ds(a, m, stride=ROW_TILE), :] for a in range(ROW_TILE)], axis=1)


def _outproj_kernel(m_ref, w_ref, x_ref, g_ref, wr_ref, h_ref, hn_ref, lg_ref):
    h = x_ref[...] + jnp.dot(m_ref[...], w_ref[...], preferred_element_type=F32)
    hn = _rms(h, g_ref[...])
    h_ref[...] = h
    _store_row_tiles(hn_ref, _pack_halves(hn))
    lg_ref[...] = _bdot(hn, wr_ref[...])


def _out_projection(merged, w_out, x, ffn_gain, w_router, tm):
    n, d = x.shape
    e = w_router.shape[1]
    row = pl.BlockSpec((tm, d), lambda i: (i, 0))
    const = lambda a: pl.BlockSpec(a.shape, lambda i: (0, 0))
    g = ffn_gain.reshape(1, d)
    return pl.pallas_call(
        _outproj_kernel,
        out_shape=[jax.ShapeDtypeStruct((n, d), F32), jax.ShapeDtypeStruct((n * ROW_TILE, LANES), U32),
                   jax.ShapeDtypeStruct((n, e), F32)],
        grid=(n // tm,),
        in_specs=[row, const(w_out), row, const(g), const(w_router)],
        out_specs=[row, pl.BlockSpec((tm * ROW_TILE, LANES), lambda i: (i, 0)),
                   pl.BlockSpec((tm, e), lambda i: (i, 0))],
        compiler_params=_params("parallel"),
        name="out_projection",
    )(merged, w_out, x, g, w_router)


def _route_kernel(lg_ref, bias_ref, h_ref, hn_ref, sg_ref, su_ref, sd_ref, idx_ref, wt_ref, rank_ref,
                  cnt_ref, base_ref, carry_ref):
    i = pl.program_id(0)

    @pl.when(i == 0)
    def _():
        carry_ref[...] = jnp.zeros_like(carry_ref)

    base_ref[...] = h_ref[...] + _swiglu_packed(_load_row_tiles(hn_ref, h_ref.shape[0]), sg_ref, su_ref, sd_ref)

    scores = jax.nn.sigmoid(lg_ref[...])
    tm, e = scores.shape
    sel = scores + bias_ref[...]
    lane = lax.broadcasted_iota(I32, (tm, e), 1)
    hots, ws = [], []
    for _ in range(TOP_K):
        mx = jnp.max(sel, axis=-1, keepdims=True)
        ix = jnp.min(jnp.where(sel == mx, lane, e), axis=-1, keepdims=True)
        hot = lane == ix
        hots.append(hot)
        ws.append(jnp.sum(jnp.where(hot, scores, 0.0), axis=-1, keepdims=True))
        sel = jnp.where(hot, -jnp.inf, sel)
    chosen = functools.reduce(jnp.logical_or, hots).astype(F32)
    r = lax.broadcasted_iota(I32, (tm, tm), 0)
    c = lax.broadcasted_iota(I32, (tm, tm), 1)
    before = (c < r).astype(BF16)
    ahead = jnp.dot(before, chosen.astype(BF16), preferred_element_type=F32) + carry_ref[...]
    carry_ref[...] += jnp.sum(chosen, axis=0, keepdims=True)
    cnt_ref[...] = carry_ref[...].astype(I32)

    norm = ROUTED_SCALE / sum(ws)
    out_lane = lax.broadcasted_iota(I32, (tm, LANES), 1)
    idx_o = jnp.zeros((tm, LANES), I32)
    wt_o = jnp.zeros((tm, LANES), F32)
    rk_o = jnp.zeros((tm, LANES), I32)
    for j in range(TOP_K):
        ix = jnp.sum(jnp.where(hots[j], lane, 0), axis=-1, keepdims=True)
        rk = jnp.sum(jnp.where(hots[j], ahead, 0.0), axis=-1, keepdims=True).astype(I32)
        idx_o = jnp.where(out_lane == j, ix, idx_o)
        wt_o = jnp.where(out_lane == j, ws[j] * norm, wt_o)
        rk_o = jnp.where(out_lane == j, rk, rk_o)
    idx_ref[...] = idx_o
    wt_ref[...] = wt_o
    rank_ref[...] = rk_o


def _route(logits, bias, h, hn, sg, su, sd, tm):
    n, e = logits.shape
    d = h.shape[1]
    row = pl.BlockSpec((tm, LANES), lambda i: (i, 0))
    wide = lambda a: pl.BlockSpec((a.shape[0] // (n // tm), a.shape[1]), lambda i: (i, 0))
    const = lambda a: pl.BlockSpec(a.shape, lambda i: (0, 0))
    bias = bias.reshape(1, e)
    return pl.pallas_call(
        _route_kernel,
        out_shape=[jax.ShapeDtypeStruct((n, LANES), I32), jax.ShapeDtypeStruct((n, LANES), F32),
                   jax.ShapeDtypeStruct((n, LANES), I32), jax.ShapeDtypeStruct((1, e), I32),
                   jax.ShapeDtypeStruct((n, d), F32)],
        grid=(n // tm,),
        in_specs=[wide(logits), const(bias), wide(h), wide(hn), const(sg), const(su), const(sd)],
        out_specs=[row, row, row, pl.BlockSpec((1, e), lambda i: (0, 0)), wide(h)],
        scratch_shapes=[pltpu.VMEM((1, e), F32)],
        compiler_params=_params("arbitrary"),
        name="route",
    )(logits, bias, h, hn, sg, su, sd)


def _dispatch_kernel(tail_ref, pos_ref, hn_ref, xs_ref, zeros, sem, zsem):
    tm = hn_ref.shape[0] // ROW_TILE
    half = zeros.shape[0]

    def tile_rows(start, size=ROW_TILE):
        return pl.ds(pl.multiple_of(start, ROW_TILE), size)

    @pl.when(pl.program_id(0) == 0)
    def _():
        zeros[...] = jnp.zeros_like(zeros)

        def tail_copies(e):
            off, pad = tail_ref[e], tail_ref[N_EXPERTS + e]
            size = half
            while size >= ROW_TILE:
                yield (pad & size) != 0, pltpu.make_async_copy(
                    zeros.at[pl.ds(0, size), :], xs_ref.at[tile_rows(off, size), :], zsem)
                off = off + (pad & size)
                size //= 2

        def start(e, carry):
            for needed, copy in tail_copies(e):
                pl.when(needed)(copy.start)
            return carry

        def wait(e, carry):
            for needed, copy in tail_copies(e):
                pl.when(needed)(copy.wait)
            return carry

        lax.fori_loop(0, N_EXPERTS, start, 0)
        lax.fori_loop(0, N_EXPERTS, wait, 0)

    def row_copy(r, dst_row):
        return pltpu.make_async_copy(hn_ref.at[tile_rows(r * ROW_TILE), :], xs_ref.at[tile_rows(dst_row), :], sem)

    def issue(r, carry):
        for j in range(TOP_K):
            row_copy(r, pos_ref[r * TOP_K + j]).start(priority=j % DMA_THREADS)
        return carry

    lax.fori_loop(0, tm, issue, 0)

    def drain(r, carry):
        for j in range(TOP_K):
            row_copy(r, 0).wait()
        return carry

    lax.fori_loop(0, tm, drain, 0)


def _dispatch(hn, pos_rows, tails, n_blocks, tm):
    n = hn.shape[0] // ROW_TILE
    blk = MOE_BLOCK
    return pl.pallas_call(
        _dispatch_kernel,
        out_shape=jax.ShapeDtypeStruct((n_blocks * blk * ROW_TILE, LANES), hn.dtype),
        grid_spec=pltpu.PrefetchScalarGridSpec(
            num_scalar_prefetch=1,
            grid=(n // tm,),
            in_specs=[pl.BlockSpec((tm * TOP_K,), lambda i, tl: (i,), memory_space=pltpu.SMEM),
                      pl.BlockSpec((tm * ROW_TILE, LANES), lambda i, tl: (i, 0))],
            out_specs=pl.BlockSpec(memory_space=pl.ANY),
            scratch_shapes=[pltpu.VMEM((blk // 2 * ROW_TILE, LANES), hn.dtype), pltpu.SemaphoreType.DMA,
                            pltpu.SemaphoreType.DMA]),
        compiler_params=pltpu.CompilerParams(dimension_semantics=("arbitrary",),
                                             vmem_limit_bytes=VMEM_LIMIT, has_side_effects=True),
        name="moe_dispatch",
    )(tails, pos_rows, hn)


def _expert_kernel(be_ref, nu_ref, nxt_ref, x_ref, wg_ref, wu_ref, wd_ref, y_ref, wg_f, wu_f, wd_f, wg_b,
                   wu_b, wd_b, sem):
    b = pl.program_id(0)

    def fetch(e):
        return [pltpu.make_async_copy(src.at[e], dst, sem)
                for src, dst in ((wg_ref, wg_f), (wu_ref, wu_f), (wd_ref, wd_f))]

    @pl.when(b < nu_ref[0])
    def _():
        e = be_ref[b]

        @pl.when(b == 0)
        def _():
            for copy in fetch(e):
                copy.start()

        @pl.when((b == 0) | (e != be_ref[jnp.maximum(b - 1, 0)]))
        def _():
            for copy in fetch(e):
                copy.wait()
            wg_b[...] = wg_f[...].astype(BF16)
            wu_b[...] = wu_f[...].astype(BF16)
            wd_b[...] = wd_f[...].astype(BF16)

            @pl.when(nxt_ref[e] >= 0)
            def _():
                for copy in fetch(nxt_ref[e]):
                    copy.start()

        x = _load_row_tiles(x_ref, x_ref.shape[0] // ROW_TILE)
        _store_row_tiles(y_ref, _pack_halves(_swiglu_packed(x, wg_b, wu_b, wd_b)))


def _swiglu_packed(x_packed, wg_ref, wu_ref, wd_ref):
    lo, hi = (t.astype(BF16) for t in _unpack_halves(x_packed))
    c = lo.shape[1]
    split_dot = lambda w: (jnp.dot(lo, w[:c, :], preferred_element_type=F32)
                           + jnp.dot(hi, w[c:, :], preferred_element_type=F32))
    return _bdot(_silu(split_dot(wg_ref)) * split_dot(wu_ref), wd_ref[...])


def _experts(xs, block_expert, n_used, next_expert, n_blocks, wg, wu, wd):
    d, ff = wg.shape[1], wg.shape[2]
    rows = MOE_BLOCK * ROW_TILE
    used = lambda b, be, nu, nxt: (jnp.minimum(b, nu[0] - 1), 0)
    hbm = pl.BlockSpec(memory_space=pl.ANY)
    return pl.pallas_call(
        _expert_kernel,
        out_shape=jax.ShapeDtypeStruct(xs.shape, xs.dtype),
        grid_spec=pltpu.PrefetchScalarGridSpec(
            num_scalar_prefetch=3,
            grid=(n_blocks,),
            in_specs=[pl.BlockSpec((rows, LANES), used), hbm, hbm, hbm],
            out_specs=pl.BlockSpec((rows, LANES), used),
            scratch_shapes=[pltpu.VMEM((d, ff), F32), pltpu.VMEM((d, ff), F32), pltpu.VMEM((ff, d), F32),
                            pltpu.VMEM((d, ff), BF16), pltpu.VMEM((d, ff), BF16), pltpu.VMEM((ff, d), BF16),
                            pltpu.SemaphoreType.DMA]),
        compiler_params=_params("arbitrary"),
        name="moe_experts",
    )(block_expert, n_used, next_expert, xs, wg, wu, wd)


def _combine_kernel(pos_ref, pos_next_ref, base_ref, wt_ref, y_ref, o_ref, buf, sem):
    i = pl.program_id(0)
    tm = base_ref.shape[0]
    cur = i % 2

    def tile_rows(start):
        return pl.ds(pl.multiple_of(start, ROW_TILE), ROW_TILE)

    def row_copy(half, r, j, src_row):
        return pltpu.make_async_copy(y_ref.at[tile_rows(src_row), :],
                                     buf.at[half, j, tile_rows(r * ROW_TILE), :], sem.at[half])

    def request(slots_ref, half):
        def issue(r, carry):
            for j in range(TOP_K):
                row_copy(half, r, j, slots_ref[r * TOP_K + j]).start(priority=j % DMA_THREADS)
            return carry

        lax.fori_loop(0, tm, issue, 0)

    @pl.when(i == 0)
    def _():
        request(pos_ref, 0)

    @pl.when(i + 1 < pl.num_programs(0))
    def _():
        request(pos_next_ref, 1 - cur)

    def drain(r, carry):
        for j in range(TOP_K):
            row_copy(cur, r, j, 0).wait()
        return carry

    lax.fori_loop(0, tm, drain, 0)

    acc = base_ref[...]
    c = acc.shape[1] // 2
    acc_lo, acc_hi = acc[:, :c], acc[:, c:]
    wt = wt_ref[...]
    for j in range(TOP_K):
        lo, hi = _unpack_halves(_load_row_tiles(buf.at[cur, j], tm))
        acc_lo = acc_lo + wt[:, j:j + 1] * lo
        acc_hi = acc_hi + wt[:, j:j + 1] * hi
    o_ref[:, :c] = acc_lo
    o_ref[:, c:] = acc_hi


def _combine(base, wts, pos_flat, y, tm):
    n, d = base.shape
    steps = n // tm
    row = pl.BlockSpec((tm, d), lambda i: (i, 0))
    slots = lambda im: pl.BlockSpec((tm * TOP_K,), im, memory_space=pltpu.SMEM)
    return pl.pallas_call(
        _combine_kernel,
        out_shape=jax.ShapeDtypeStruct((n, d), F32),
        grid=(steps,),
        in_specs=[slots(lambda i: (i,)), slots(lambda i: (jnp.minimum(i + 1, steps - 1),)),
                  row, pl.BlockSpec((tm, LANES), lambda i: (i, 0)), pl.BlockSpec(memory_space=pl.ANY)],
        out_specs=row,
        scratch_shapes=[pltpu.VMEM((2, TOP_K, tm * ROW_TILE, LANES), y.dtype), pltpu.SemaphoreType.DMA((2,))],
        compiler_params=_params("arbitrary"),
        name="moe_combine",
    )(pos_flat, pos_flat, base, wts, y)


def _tile(n, want):
    t = min(n, want)
    assert n % t == 0, (n, want)
    return t


def _layer(layer, h, mem, positions, mix_norm, w_in, a_q_gain, a_k_gain, lower_raw, hg_out_gain, x_q_gain,
           x_k_gain, mem_norm, w_mem_kv, w_branch_a, w_branch_b, w_branch_c, w_out, ffn_norm, w_router,
           router_bias, w_exp_gate, w_exp_up, w_exp_down, w_sh_gate, w_sh_up, w_sh_down):
    b, s, d = h.shape
    n = b * s
    x2 = h.reshape(n, d)
    tm = _tile(n, 1024)
    bf = lambda w: w.astype(BF16)

    xn = _rms_norm_rows(x2, mix_norm, _tile(n, 512))
    cos, sin = _rope_tables(positions, _tile(n, 1024))
    w_in = bf(w_in)
    tmd = _tile(s, tm)
    rope_in = lambda gain: ((cos, (tmd, HEAD_DIM), lambda i, j: (i, 0)),
                            (sin, (tmd, HEAD_DIM), lambda i, j: (i, 0)),
                            (gain.reshape(1, HEAD_DIM), (1, HEAD_DIM), lambda i, j: (0, 0)))
    proj = functools.partial(_proj, xn, w_in, tm=tm)
    dilated = functools.partial(_proj_dilated, xn, w_in, tm=tmd, seq=s)
    hg_w = lower_raw.shape[1]
    off = 0
    aq = dilated(functools.partial(_head_qk, HEAD_DIM ** -0.5), n0=off, extras=rope_in(a_q_gain),
                 name="proj_aq")
    off += A_WIDTH
    ak = dilated(functools.partial(_head_qk, 1.0), n0=off, extras=rope_in(a_k_gain), name="proj_ak")
    off += A_WIDTH
    av = dilated(_head_plain, n0=off, name="proj_av")
    off += A_WIDTH
    hq, = proj(_ep_silu, n0=off, n=hg_w, tn=512, out_dtypes=[BF16], name="proj_hq")
    off += hg_w
    logf, = proj(functools.partial(_ep_log_forget, layer), n0=off, n=hg_w, tn=512, out_dtypes=[F32],
                 extras=((lower_raw, (lower_raw.shape[0], 512), lambda i, j: (0, j)),), name="proj_hf")
    off += hg_w
    hiv, = proj(_ep_plain, n0=off, n=hg_w, tn=512, out_dtypes=[BF16], name="proj_hi")
    off += hg_w
    hgate, = proj(_ep_plain, n0=off, n=hg_w, tn=512, out_dtypes=[BF16], name="proj_hgate")
    off += hg_w
    xq, = _proj(xn, w_in[:, off:off + X_WIDTH], functools.partial(_ep_headnorm, X_HEAD_DIM, X_HEAD_DIM ** -0.5),
                n0=0, n=X_WIDTH, tn=2 * X_HEAD_DIM, tm=tm, out_dtypes=[BF16],
                extras=((x_q_gain.reshape(1, X_HEAD_DIM), (1, X_HEAD_DIM), lambda i, j: (0, 0)),),
                name="proj_xq")
    off += X_WIDTH
    gates, = proj(_ep_sigmoid, n0=off, n=3 * d, tn=512, out_dtypes=[BF16], name="proj_gates")

    r3 = lambda t: t.reshape(b, s, t.shape[1])
    ya = _dilated_attention(list(zip(aq, ak, av)), b, s).reshape(n, A_OUT)
    yb = _hgrn2(r3(hq), r3(logf), r3(hiv), r3(hgate), hg_out_gain, HG_HEADS_PER_STEP).reshape(n, hg_w)
    nm = mem.shape[0] * mem.shape[1]
    mem_n = _rms_norm_rows(mem.reshape(nm, d), mem_norm, _tile(nm, 512))
    w_kv = bf(w_mem_kv)
    tmm = _tile(nm, 1024)
    kn, = _proj(mem_n, w_kv, functools.partial(_ep_headnorm, X_HEAD_DIM, 1.0), n0=0, n=X_WIDTH,
                tn=2 * X_HEAD_DIM, tm=tmm, out_dtypes=[BF16],
                extras=((x_k_gain.reshape(1, X_HEAD_DIM), (1, X_HEAD_DIM), lambda i, j: (0, 0)),),
                name="proj_mem_k")
    vm, = _proj(mem_n, w_kv, _ep_plain, n0=X_WIDTH, n=X_WIDTH, tn=2 * X_HEAD_DIM, tm=tmm,
                out_dtypes=[BF16], name="proj_mem_v")
    rm = lambda t: t.reshape(b, mem.shape[1], X_WIDTH)
    yc = _memory_attention(r3(xq), rm(kn), rm(vm), _tile(s, 512)).reshape(n, X_WIDTH)

    merged = _merge_branches(ya, yb, yc, bf(w_branch_a), bf(w_branch_b), bf(w_branch_c), gates, tm, 512)
    h1, hn, logits = _out_projection(merged, bf(w_out), x2, ffn_norm, bf(w_router), _tile(n, 256))

    tr = _tile(n, 256)
    idx, wts, rank, counts, base = _route(logits, router_bias, h1, hn, bf(w_sh_gate), bf(w_sh_up),
                                          bf(w_sh_down), _tile(n, 512))
    blk = MOE_BLOCK
    counts = counts[0]
    padded = (counts + blk - 1) // blk * blk
    ends = jnp.cumsum(padded)
    starts = ends - padded
    hit = idx[:, :TOP_K, None] == jnp.arange(N_EXPERTS, dtype=I32)
    pos = (jnp.sum(jnp.where(hit, starts, 0), axis=-1) + rank[:, :TOP_K]).reshape(-1).astype(I32)
    pos = pos * ROW_TILE
    n_blocks = -(-(n * TOP_K + N_EXPERTS * (blk - 1)) // blk)
    n_used = (ends[-1] // blk).astype(I32)
    blocks = jnp.minimum(jnp.arange(n_blocks, dtype=I32), n_used - 1)
    block_expert = jnp.minimum(jnp.sum(ends[None, :] <= blocks[:, None] * blk, axis=1), N_EXPERTS - 1)
    tails = (jnp.concatenate([starts + counts, padded - counts]) * ROW_TILE).astype(I32)
    xs = _dispatch(hn, pos, tails, n_blocks, tr)
    later = lax.cummin(jnp.where(counts > 0, jnp.arange(N_EXPERTS, dtype=I32), N_EXPERTS), reverse=True)
    later = jnp.concatenate([later[1:], jnp.full((1,), N_EXPERTS, I32)])
    next_expert = jnp.where(later < N_EXPERTS, later, -1).astype(I32)
    y = _experts(xs, block_expert.astype(I32), n_used.reshape(1), next_expert, n_blocks, w_exp_gate,
                 w_exp_up, w_exp_down)
    out = _combine(base, wts, pos, y, _tile(n, 128))
    return out.reshape(b, s, d)


def kernel(x, mem, positions, mix_norm, w_in, a_q_gain, a_k_gain, hg_lower_bounds, hg_out_gain, x_q_gain, x_k_gain, mem_norm, w_mem_kv, w_branch_a, w_branch_b, w_branch_c, w_out, ffn_norm, w_router, router_bias, w_exp_gate, w_exp_up, w_exp_down, w_sh_gate, w_sh_up, w_sh_down):
    h = x
    for layer in range(w_in.shape[0]):
        h = _layer(layer, h, mem, positions, mix_norm[layer], w_in[layer], a_q_gain[layer], a_k_gain[layer],
                   hg_lower_bounds, hg_out_gain[layer], x_q_gain[layer], x_k_gain[layer], mem_norm[layer],
                   w_mem_kv[layer], w_branch_a[layer], w_branch_b[layer], w_branch_c[layer], w_out[layer],
                   ffn_norm[layer], w_router[layer], router_bias[layer], w_exp_gate[layer],
                   w_exp_up[layer], w_exp_down[layer], w_sh_gate[layer], w_sh_up[layer], w_sh_down[layer])
    return h
```

```python
import functools
import math

import jax
import jax.numpy as jnp
from jax import lax
from jax.experimental import pallas as pl
from jax.experimental.pallas import tpu as pltpu

F32, BF16, I32 = jnp.float32, jnp.bfloat16, jnp.int32

EPS = 1e-6
ROPE_THETA = 10000.0
HEAD_DIM = 128
DILATIONS = (1, 4, 16)
DIL_BACK = 128
HEADS_PER_GROUP = 4
A_WIDTH = len(DILATIONS) * HEADS_PER_GROUP * HEAD_DIM
A_OUT = HEADS_PER_GROUP * HEAD_DIM
HG_DIM = 128
HG_CHUNK = 64
HG_STRIP = 8
HG_HEADS_PER_STEP = 8
X_HEADS = 4
X_HEAD_DIM = 384
X_WIDTH = X_HEADS * X_HEAD_DIM
N_EXPERTS = 64
TOP_K = 8
ROUTED_SCALE = 2.5
MOE_BLOCK = 512
MOE_SUB_ROWS = 256
PROJ_SUB_ROWS = 256
DMA_THREADS = 2
LANES = 128
ROW_TILE = 16
SUBLANES = 8
VMEM_LIMIT = 56 * 1024 * 1024

NT = (((1,), (1,)), ((), ()))
TN = (((0,), (0,)), ((), ()))


def _params(*sem):
    return pltpu.CompilerParams(dimension_semantics=sem, vmem_limit_bytes=VMEM_LIMIT)


def _rms(x, gain):
    return x * lax.rsqrt(jnp.mean(x * x, axis=-1, keepdims=True) + EPS) * gain


def _silu(x):
    return x * jax.nn.sigmoid(x)


def _bdot(a, b):
    return jnp.dot(a.astype(BF16), b.astype(BF16), preferred_element_type=F32)


def _norm_kernel(x_ref, g_ref, o_ref):
    o_ref[...] = _rms(x_ref[...].astype(F32), g_ref[...]).astype(o_ref.dtype)


def _rms_norm_rows(x, gain, tm):
    m, d = x.shape
    return pl.pallas_call(
        _norm_kernel,
        out_shape=jax.ShapeDtypeStruct((m, d), BF16),
        grid=(m // tm,),
        in_specs=[pl.BlockSpec((tm, d), lambda i: (i, 0)), pl.BlockSpec((1, d), lambda i: (0, 0))],
        out_specs=pl.BlockSpec((tm, d), lambda i: (i, 0)),
        compiler_params=_params("parallel"),
        name="rms_norm_rows",
    )(x, gain.reshape(1, d))


def _rope_table_kernel(pos_ref, invf_ref, cos_ref, sin_ref):
    ang = pos_ref[...].astype(F32) * invf_ref[...]
    lane = lax.broadcasted_iota(I32, ang.shape, 1)
    sin = jnp.sin(ang)
    cos_ref[...] = jnp.cos(ang)
    sin_ref[...] = jnp.where(lane < HEAD_DIM // 2, -sin, sin)


def _rope_tables(positions, tm):
    n = positions.size
    half = HEAD_DIM // 2
    inv_freq = ROPE_THETA ** (-jnp.arange(half, dtype=F32) / half)
    invf = jnp.concatenate([inv_freq, inv_freq]).reshape(1, HEAD_DIM)
    return pl.pallas_call(
        _rope_table_kernel,
        out_shape=[jax.ShapeDtypeStruct((n, HEAD_DIM), F32)] * 2,
        grid=(n // tm,),
        in_specs=[pl.BlockSpec((tm, 1), lambda i: (i, 0)), pl.BlockSpec((1, HEAD_DIM), lambda i: (0, 0))],
        out_specs=[pl.BlockSpec((tm, HEAD_DIM), lambda i: (i, 0))] * 2,
        compiler_params=_params("parallel"),
        name="rope_tables",
    )(positions.reshape(n, 1), invf)


def _proj(a, w, epilogue, *, n0, n, tn, tm, out_dtypes, extras=(), name):
    m, k = a.shape
    assert n0 % tn == 0 and n % tn == 0 and m % tm == 0, (n0, n, tn, m, tm)
    j0 = n0 // tn
    n_extra = len(extras)
    sub = min(tm, PROJ_SUB_ROWS)

    def kern(a_ref, w_ref, *refs):
        for c in range(tm // sub):
            rows = slice(c * sub, (c + 1) * sub)
            acc = jnp.dot(a_ref[rows, :], w_ref[...], preferred_element_type=F32)
            epilogue(acc, rows, refs[:n_extra], refs[n_extra:])

    in_specs = [pl.BlockSpec((tm, k), lambda i, j: (i, 0)),
                pl.BlockSpec((k, tn), lambda i, j: (0, j + j0))]
    in_specs += [pl.BlockSpec(bs, im) for _, bs, im in extras]
    return pl.pallas_call(
        kern,
        out_shape=[jax.ShapeDtypeStruct((m, n), dt) for dt in out_dtypes],
        grid=(m // tm, n // tn),
        in_specs=in_specs,
        out_specs=[pl.BlockSpec((tm, tn), lambda i, j: (i, j)) for _ in out_dtypes],
        compiler_params=_params("parallel", "arbitrary"),
        name=name,
    )(a, w, *[e[0] for e in extras])


def _ep_plain(acc, rows, ins, outs):
    outs[0][rows, :] = acc.astype(outs[0].dtype)


def _ep_silu(acc, rows, ins, outs):
    outs[0][rows, :] = _silu(acc).astype(outs[0].dtype)


def _ep_sigmoid(acc, rows, ins, outs):
    outs[0][rows, :] = jax.nn.sigmoid(acc).astype(outs[0].dtype)


def _head_qk(scale, x, rows, ins):
    y = _rms(x, ins[2][...])
    return (y * ins[0][rows, :] + pltpu.roll(y, HEAD_DIM // 2, axis=1) * ins[1][rows, :]) * scale


def _head_plain(x, rows, ins):
    return x


def _proj_dilated(a, w, head_fn, *, n0, tm, seq, extras=(), name):
    m, k = a.shape
    tn = A_OUT
    assert n0 % tn == 0 and seq % tm == 0 and m % seq == 0, (n0, tn, seq, tm, m)
    j0 = n0 // tn
    tiles = seq // tm
    n_extra = len(extras)
    sub = min(tm, PROJ_SUB_ROWS)
    n_groups = len(DILATIONS)

    def kern(a_ref, w_ref, *refs):
        ins, outs, scr = refs[:n_extra], refs[n_extra:n_extra + n_groups], refs[-1]
        for g, d in enumerate(DILATIONS):
            @pl.when(pl.program_id(1) == g)
            def _(g=g, d=d):
                per = sub // d
                for c in range(tm // sub):
                    rows = slice(c * sub, (c + 1) * sub)
                    acc = jnp.dot(a_ref[rows, :], w_ref[...], preferred_element_type=F32)
                    for h in range(HEADS_PER_GROUP):
                        cols = slice(h * HEAD_DIM, (h + 1) * HEAD_DIM)
                        scr[c, h] = head_fn(acc[:, cols], rows, ins)
                        for r in range(d):
                            outs[g][r, c * per:(c + 1) * per, cols] = (
                                scr[c, h, pl.ds(r, per, stride=d), :].astype(outs[g].dtype))

    in_specs = [pl.BlockSpec((tm, k), lambda i, j: (i, 0)),
                pl.BlockSpec((k, tn), lambda i, j: (0, j + j0))]
    in_specs += [pl.BlockSpec(bs, im) for _, bs, im in extras]
    return pl.pallas_call(
        kern,
        out_shape=[jax.ShapeDtypeStruct((m // seq, d, seq // d, tn), BF16) for d in DILATIONS],
        grid=(m // tm, n_groups),
        in_specs=in_specs,
        out_specs=[pl.BlockSpec((None, d, tm // d, tn), lambda i, j: (i // tiles, 0, i % tiles, 0))
                   for d in DILATIONS],
        scratch_shapes=[pltpu.VMEM((tm // sub, HEADS_PER_GROUP, sub, HEAD_DIM), F32)],
        compiler_params=_params("parallel", "arbitrary"),
        name=name,
    )(a, w, *[e[0] for e in extras])


def _ep_headnorm(width, scale, acc, rows, ins, outs):
    gain = ins[0][...]
    for h in range(acc.shape[1] // width):
        sl = slice(h * width, (h + 1) * width)
        outs[0][rows, sl] = (_rms(acc[:, sl], gain) * scale).astype(outs[0].dtype)


def _ep_log_forget(layer, acc, rows, ins, outs):
    raw = ins[0][...]
    mx = jnp.max(raw, axis=0, keepdims=True)
    ex = jnp.exp(raw - mx)
    lb = jnp.sum(ex[:layer + 1], axis=0, keepdims=True) / jnp.sum(ex, axis=0, keepdims=True)
    outs[0][rows, :] = jnp.log(lb + (1.0 - lb) * jax.nn.sigmoid(acc))


def _dil_attn_kernel(*refs, seq):
    n_groups = len(DILATIONS)
    o_ref, o_scr, l_scr = refs[3 * n_groups:]
    blk = DIL_BACK

    for g, d in enumerate(DILATIONS):
        q_ref, k_ref, v_ref = refs[3 * g:3 * g + 3]
        length = seq // d
        nqb = max(length // blk, 1)
        qrows = min(blk, length)
        width = min(2 * blk, length)

        for r in range(d):
            for qb in range(nqb):
                q0 = qb * qrows
                ks = min(max(q0 + qrows - width, 0), length - width)
                q = q_ref[r, q0:q0 + qrows, :]
                k = k_ref[r, ks:ks + width, :]
                v = v_ref[r, ks:ks + width, :]
                s = lax.dot_general(q, k, NT, preferred_element_type=F32)
                rel = ((q0 - ks) + lax.broadcasted_iota(I32, s.shape, 0)
                       - lax.broadcasted_iota(I32, s.shape, 1))
                s = jnp.where((rel >= 0) & (rel <= DIL_BACK), s, -jnp.inf)
                m = jnp.max(s, axis=-1, keepdims=True)
                p = jnp.exp(s - m)
                den = jnp.sum(p, axis=-1, keepdims=True)
                o = jnp.dot((p / den).astype(BF16), v, preferred_element_type=F32)
                rows = pl.ds(r + d * q0, qrows, stride=d)
                o_scr[g, rows, :] = o
                l_scr[g, rows, :] = jnp.broadcast_to(m + jnp.log(den), o.shape)

    def merge(qi, carry):
        rows = pl.ds(pl.multiple_of(qi * blk, blk), blk)
        lses = [l_scr[g, rows, :] for g in range(n_groups)]
        top = functools.reduce(jnp.maximum, lses)
        ws = [jnp.exp(l - top) for l in lses]
        ya = sum(w * o_scr[g, rows, :] for g, w in enumerate(ws)) / sum(ws)
        o_ref[rows, :] = ya.astype(o_ref.dtype)
        return carry

    lax.fori_loop(0, seq // blk, merge, 0)


def _dilated_attention(qkv, b, s):
    in_specs, args = [], []
    for g, d in enumerate(DILATIONS):
        for t in qkv[g]:
            in_specs.append(pl.BlockSpec((None, d, s // d, HEAD_DIM), lambda bi, h: (bi, 0, 0, h)))
            args.append(t)
    return pl.pallas_call(
        functools.partial(_dil_attn_kernel, seq=s),
        out_shape=jax.ShapeDtypeStruct((b, s, A_OUT), BF16),
        grid=(b, HEADS_PER_GROUP),
        in_specs=in_specs,
        out_specs=pl.BlockSpec((None, s, HEAD_DIM), lambda bi, h: (bi, 0, h)),
        scratch_shapes=[pltpu.VMEM((len(DILATIONS), s, HEAD_DIM), F32)] * 2,
        compiler_params=_params("parallel", "parallel"),
        name="dilated_attention",
    )(*args)


def _hgrn_kernel(q_ref, lf_ref, v_ref, gate_ref, gain_ref, o_ref, st_ref, q_scr, v_scr, l_scr, d_scr, *,
                 seq, heads):
    C, R, W = HG_CHUNK, HG_STRIP, HG_DIM
    ns = C // R
    st_ref[...] = jnp.zeros_like(st_ref)
    gain = gain_ref[...]
    row = lax.broadcasted_iota(I32, (C, W), 0)
    far_rows = R * ns * (ns - 1) // 2
    rr = lax.broadcasted_iota(I32, (far_rows, C), 0)
    cc = lax.broadcasted_iota(I32, (far_rows, C), 1)
    keep, start = None, 0
    for j in range(ns - 1):
        size = (ns - 1 - j) * R
        blk = (rr >= start) & (rr < start + size) & (cc >= j * R) & (cc < (j + 1) * R)
        keep = blk if keep is None else keep | blk
        start += size

    def chunk(hd, r0):
        cols = slice(hd * W, (hd + 1) * W)
        lf = lf_ref[pl.ds(r0, C), cols]
        G = lf
        step = 1
        while step < C:
            G = G + jnp.where(row >= step, pltpu.roll(G, step, axis=0), 0.0)
            step *= 2
        kf = 1.0 - jnp.exp(lf)
        qf = q_ref[pl.ds(r0, C), cols].astype(F32)
        v = v_ref[pl.ds(r0, C), cols]
        vf = v.astype(F32)
        st = st_ref[hd]
        o = lax.dot_general((qf * jnp.exp(G)).astype(BF16), st.astype(BF16), NT,
                            preferred_element_type=F32)
        q_scr[hd] = qf
        v_scr[hd] = vf
        l_scr[hd] = lf
        strided = lambda ref, t: ref[hd, pl.ds(t, ns, stride=R), :]
        lt = [strided(l_scr, t) for t in range(R)]
        kt_ = [1.0 - jnp.exp(l) for l in lt]
        vt_ = [strided(v_scr, t) for t in range(R)]
        pre = [lt[0]]
        for t in range(1, R):
            pre.append(pre[-1] + lt[t])
        for t in range(R):
            qt_ = strided(q_scr, t)
            acc = jnp.sum(qt_ * kt_[t], axis=-1, keepdims=True) * vt_[t]
            for s in range(t):
                a = jnp.sum(qt_ * kt_[s] * jnp.exp(pre[t] - pre[s]), axis=-1, keepdims=True)
                acc = acc + a * vt_[s]
            d_scr[hd, pl.ds(t, ns, stride=R), :] = acc
        o = o + d_scr[hd]
        G3, k3 = G.reshape(ns, R, W), kf.reshape(ns, R, W)
        kt = (k3 * jnp.exp(G3[:, R - 1:R] - G3)).reshape(C, W).astype(BF16)
        qt = jnp.concatenate([qf[j * R:] * jnp.exp(G[j * R:] - G[j * R - 1:j * R]) for j in range(1, ns)],
                             axis=0).astype(BF16)
        a = lax.dot_general(qt, kt, NT, preferred_element_type=F32)
        far = jnp.dot(jnp.where(keep, a, 0.0).astype(BF16), v, preferred_element_type=F32)
        pieces = [o[i * R:(i + 1) * R] for i in range(ns)]
        start = 0
        for j in range(ns - 1):
            for i in range(j + 1, ns):
                pieces[i] = pieces[i] + far[start:start + R]
                start += R
        o = jnp.concatenate(pieces, axis=0)
        y = _rms(o, gain) * _silu(gate_ref[pl.ds(r0, C), cols].astype(F32))
        o_ref[pl.ds(r0, C), cols] = y.astype(o_ref.dtype)
        g_last = G[C - 1:C]
        kd = (kf * jnp.exp(g_last - G)).astype(BF16)
        st_ref[hd] = st * jnp.exp(g_last) + lax.dot_general(v, kd, TN, preferred_element_type=F32)

    def body(c, carry):
        r0 = pl.multiple_of(c * C, C)
        for hd in range(heads):
            chunk(hd, r0)
        return carry

    lax.fori_loop(0, seq // C, body, 0)


def _hgrn2(q, logf, v, gate, out_gain, heads):
    b, s, width = q.shape
    spec = pl.BlockSpec((None, s, heads * HG_DIM), lambda bi, h: (bi, 0, h))
    return pl.pallas_call(
        functools.partial(_hgrn_kernel, seq=s, heads=heads),
        out_shape=jax.ShapeDtypeStruct((b, s, width), BF16),
        grid=(b, width // (heads * HG_DIM)),
        in_specs=[spec, spec, spec, spec, pl.BlockSpec((1, HG_DIM), lambda bi, h: (0, 0))],
        out_specs=spec,
        scratch_shapes=[pltpu.VMEM((heads, HG_DIM, HG_DIM), F32)]
        + [pltpu.VMEM((heads, HG_CHUNK, HG_DIM), F32)] * 4,
        compiler_params=_params("parallel", "parallel"),
        name="hgrn2",
    )(q, logf, v, gate, out_gain.reshape(1, HG_DIM))


def _xattn_kernel(q_ref, k_ref, v_ref, o_ref):
    for h in range(X_HEADS):
        cols = slice(h * X_HEAD_DIM, (h + 1) * X_HEAD_DIM)
        s = lax.dot_general(q_ref[:, cols], k_ref[:, cols], NT, preferred_element_type=F32)
        p = jnp.exp(s - jnp.max(s, axis=-1, keepdims=True))
        p = p / jnp.sum(p, axis=-1, keepdims=True)
        o_ref[:, cols] = jnp.dot(p.astype(BF16), v_ref[:, cols],
                                 preferred_element_type=F32).astype(o_ref.dtype)


def _memory_attention(q, k, v, ts):
    b, s, _ = q.shape
    m = k.shape[1]
    return pl.pallas_call(
        _xattn_kernel,
        out_shape=jax.ShapeDtypeStruct((b, s, X_WIDTH), BF16),
        grid=(b, s // ts),
        in_specs=[pl.BlockSpec((None, ts, X_WIDTH), lambda bi, si: (bi, si, 0)),
                  pl.BlockSpec((None, m, X_WIDTH), lambda bi, si: (bi, 0, 0)),
                  pl.BlockSpec((None, m, X_WIDTH), lambda bi, si: (bi, 0, 0))],
        out_specs=pl.BlockSpec((None, ts, X_WIDTH), lambda bi, si: (bi, si, 0)),
        compiler_params=_params("parallel", "parallel"),
        name="memory_attention",
    )(q, k, v)


def _merge_kernel(ya_ref, yb_ref, yc_ref, wa_ref, wb_ref, wc_ref, ga_ref, gb_ref, gc_ref, o_ref):
    acc = ga_ref[...].astype(F32) * jnp.dot(ya_ref[...], wa_ref[...], preferred_element_type=F32)
    acc += gb_ref[...].astype(F32) * jnp.dot(yb_ref[...], wb_ref[...], preferred_element_type=F32)
    acc += gc_ref[...].astype(F32) * jnp.dot(yc_ref[...], wc_ref[...], preferred_element_type=F32)
    o_ref[...] = acc.astype(o_ref.dtype)


def _merge_branches(ya, yb, yc, wa, wb, wc, gates, tm, tn):
    n, d = yb.shape[0], wa.shape[1]
    nj = d // tn
    row = lambda a: pl.BlockSpec((tm, a.shape[1]), lambda i, j: (i, 0))
    col = lambda w: pl.BlockSpec((w.shape[0], tn), lambda i, j: (0, j))
    gate = lambda br: pl.BlockSpec((tm, tn), lambda i, j: (i, br * nj + j))
    return pl.pallas_call(
        _merge_kernel,
        out_shape=jax.ShapeDtypeStruct((n, d), BF16),
        grid=(n // tm, nj),
        in_specs=[row(ya), row(yb), row(yc), col(wa), col(wb), col(wc), gate(0), gate(1), gate(2)],
        out_specs=pl.BlockSpec((tm, tn), lambda i, j: (i, j)),
        compiler_params=_params("parallel", "arbitrary"),
        name="merge_branches",
    )(ya, yb, yc, wa, wb, wc, gates, gates, gates)


def _store_row_tiles(ref, x, scr):
    m = x.shape[0]
    assert x.shape[1] == ROW_TILE * LANES, x.shape
    for a in range(ROW_TILE):
        scr[a // SUBLANES, pl.ds(a % SUBLANES, m, stride=SUBLANES), :] = x[:, a * LANES:(a + 1) * LANES]
    groups = [scr[t].reshape(m, SUBLANES, LANES) for t in range(ROW_TILE // SUBLANES)]
    ref[...] = jnp.stack(groups, axis=1).reshape(m * ROW_TILE, LANES).astype(ref.dtype)


def _row_tile_scratch(m, *lead):
    return pltpu.VMEM((*lead, ROW_TILE // SUBLANES, m * SUBLANES, LANES), F32)


def _row_tiles_to_scratch(ref, scr):
    m = scr.shape[1] // SUBLANES
    x = ref[...].astype(F32).reshape(m, ROW_TILE // SUBLANES, SUBLANES, LANES)
    for t in range(ROW_TILE // SUBLANES):
        scr[t] = x[:, t].reshape(m * SUBLANES, LANES)


def _scratch_chunk(scr, a):
    m = scr.shape[1] // SUBLANES
    return scr[a // SUBLANES, pl.ds(a % SUBLANES, m, stride=SUBLANES), :]


def _load_row_tiles(ref, scr):
    _row_tiles_to_scratch(ref, scr)
    return jnp.concatenate([_scratch_chunk(scr, a) for a in range(ROW_TILE)], axis=1)


def _swiglu(x, wg_ref, wu_ref, wd_ref):
    x = x.astype(BF16)
    g = jnp.dot(x, wg_ref[...], preferred_element_type=F32)
    u = jnp.dot(x, wu_ref[...], preferred_element_type=F32)
    return _bdot(_silu(g) * u, wd_ref[...])


def _outproj_kernel(m_ref, w_ref, x_ref, g_ref, wr_ref, h_ref, hn_ref, lg_ref, scr):
    tm = m_ref.shape[0]
    sub = scr.shape[2] // SUBLANES
    for c in range(tm // sub):
        rows = slice(c * sub, (c + 1) * sub)
        h = x_ref[rows, :] + jnp.dot(m_ref[rows, :], w_ref[...], preferred_element_type=F32)
        hn = _rms(h, g_ref[...])
        h_ref[rows, :] = h
        _store_row_tiles(hn_ref.at[pl.ds(c * sub * ROW_TILE, sub * ROW_TILE), :], hn, scr.at[c])
        lg_ref[rows, :] = _bdot(hn, wr_ref[...])


def _out_projection(merged, w_out, x, ffn_gain, w_router, tm):
    n, d = x.shape
    e = w_router.shape[1]
    sub = min(tm, PROJ_SUB_ROWS)
    row = pl.BlockSpec((tm, d), lambda i: (i, 0))
    const = lambda a: pl.BlockSpec(a.shape, lambda i: (0, 0))
    g = ffn_gain.reshape(1, d)
    return pl.pallas_call(
        _outproj_kernel,
        out_shape=[jax.ShapeDtypeStruct((n, d), F32), jax.ShapeDtypeStruct((n * ROW_TILE, LANES), BF16),
                   jax.ShapeDtypeStruct((n, e), F32)],
        grid=(n // tm,),
        in_specs=[row, const(w_out), row, const(g), const(w_router)],
        out_specs=[row, pl.BlockSpec((tm * ROW_TILE, LANES), lambda i: (i, 0)),
                   pl.BlockSpec((tm, e), lambda i: (i, 0))],
        scratch_shapes=[_row_tile_scratch(sub, tm // sub)],
        compiler_params=_params("parallel"),
        name="out_projection",
    )(merged, w_out, x, g, w_router)


def _route_kernel(lg_ref, bias_ref, h_ref, hn_ref, sg_ref, su_ref, sd_ref, idx_ref, wt_ref, rank_ref,
                  cnt_ref, base_ref, carry_ref, scr):
    i = pl.program_id(0)

    @pl.when(i == 0)
    def _():
        carry_ref[...] = jnp.zeros_like(carry_ref)

    base_ref[...] = h_ref[...] + _swiglu(_load_row_tiles(hn_ref, scr), sg_ref, su_ref, sd_ref)

    scores = jax.nn.sigmoid(lg_ref[...])
    tm, e = scores.shape
    sel = scores + bias_ref[...]
    lane = lax.broadcasted_iota(I32, (tm, e), 1)
    hots, ws = [], []
    for _ in range(TOP_K):
        mx = jnp.max(sel, axis=-1, keepdims=True)
        ix = jnp.min(jnp.where(sel == mx, lane, e), axis=-1, keepdims=True)
        hot = lane == ix
        hots.append(hot)
        ws.append(jnp.sum(jnp.where(hot, scores, 0.0), axis=-1, keepdims=True))
        sel = jnp.where(hot, -jnp.inf, sel)
    chosen = functools.reduce(jnp.logical_or, hots).astype(F32)
    r = lax.broadcasted_iota(I32, (tm, tm), 0)
    c = lax.broadcasted_iota(I32, (tm, tm), 1)
    before = (c < r).astype(BF16)
    ahead = jnp.dot(before, chosen.astype(BF16), preferred_element_type=F32) + carry_ref[...]
    carry_ref[...] += jnp.sum(chosen, axis=0, keepdims=True)
    cnt_ref[...] = carry_ref[...].astype(I32)

    norm = ROUTED_SCALE / sum(ws)
    out_lane = lax.broadcasted_iota(I32, (tm, LANES), 1)
    idx_o = jnp.zeros((tm, LANES), I32)
    wt_o = jnp.zeros((tm, LANES), F32)
    rk_o = jnp.zeros((tm, LANES), I32)
    for j in range(TOP_K):
        ix = jnp.sum(jnp.where(hots[j], lane, 0), axis=-1, keepdims=True)
        rk = jnp.sum(jnp.where(hots[j], ahead, 0.0), axis=-1, keepdims=True).astype(I32)
        idx_o = jnp.where(out_lane == j, ix, idx_o)
        wt_o = jnp.where(out_lane == j, ws[j] * norm, wt_o)
        rk_o = jnp.where(out_lane == j, rk, rk_o)
    idx_ref[...] = idx_o
    wt_ref[...] = wt_o
    rank_ref[...] = rk_o


def _route(logits, bias, h, hn, sg, su, sd, tm):
    n, e = logits.shape
    d = h.shape[1]
    row = pl.BlockSpec((tm, LANES), lambda i: (i, 0))
    wide = lambda a: pl.BlockSpec((a.shape[0] // (n // tm), a.shape[1]), lambda i: (i, 0))
    const = lambda a: pl.BlockSpec(a.shape, lambda i: (0, 0))
    bias = bias.reshape(1, e)
    return pl.pallas_call(
        _route_kernel,
        out_shape=[jax.ShapeDtypeStruct((n, LANES), I32), jax.ShapeDtypeStruct((n, LANES), F32),
                   jax.ShapeDtypeStruct((n, LANES), I32), jax.ShapeDtypeStruct((1, e), I32),
                   jax.ShapeDtypeStruct((n, d), F32)],
        grid=(n // tm,),
        in_specs=[wide(logits), const(bias), wide(h), wide(hn), const(sg), const(su), const(sd)],
        out_specs=[row, row, row, pl.BlockSpec((1, e), lambda i: (0, 0)), wide(h)],
        scratch_shapes=[pltpu.VMEM((1, e), F32), _row_tile_scratch(tm)],
        compiler_params=_params("arbitrary"),
        name="route",
    )(logits, bias, h, hn, sg, su, sd)


def _dispatch_kernel(tail_ref, pos_ref, hn_ref, xs_ref, zeros, sem, zsem):
    tm = hn_ref.shape[0] // ROW_TILE
    half = zeros.shape[0]

    def tile_rows(start, size=ROW_TILE):
        return pl.ds(pl.multiple_of(start, ROW_TILE), size)

    @pl.when(pl.program_id(0) == 0)
    def _():
        zeros[...] = jnp.zeros_like(zeros)

        def tail_copies(e):
            off, pad = tail_ref[e], tail_ref[N_EXPERTS + e]
            size = half
            while size >= ROW_TILE:
                yield (pad & size) != 0, pltpu.make_async_copy(
                    zeros.at[pl.ds(0, size), :], xs_ref.at[tile_rows(off, size), :], zsem)
                off = off + (pad & size)
                size //= 2

        def start(e, carry):
            for needed, copy in tail_copies(e):
                pl.when(needed)(copy.start)
            return carry

        def wait(e, carry):
            for needed, copy in tail_copies(e):
                pl.when(needed)(copy.wait)
            return carry

        lax.fori_loop(0, N_EXPERTS, start, 0)
        lax.fori_loop(0, N_EXPERTS, wait, 0)

    def row_copy(r, dst_row):
        return pltpu.make_async_copy(hn_ref.at[tile_rows(r * ROW_TILE), :], xs_ref.at[tile_rows(dst_row), :], sem)

    def issue(r, carry):
        for j in range(TOP_K):
            row_copy(r, pos_ref[r * TOP_K + j]).start(priority=j % DMA_THREADS)
        return carry

    lax.fori_loop(0, tm, issue, 0)
    for j in range(TOP_K):
        pltpu.make_async_copy(hn_ref, xs_ref.at[pl.ds(0, tm * ROW_TILE), :], sem).wait()


def _dispatch(hn, pos_rows, tails, n_blocks, tm):
    n = hn.shape[0] // ROW_TILE
    blk = MOE_BLOCK
    return pl.pallas_call(
        _dispatch_kernel,
        out_shape=jax.ShapeDtypeStruct((n_blocks * blk * ROW_TILE, LANES), hn.dtype),
        grid_spec=pltpu.PrefetchScalarGridSpec(
            num_scalar_prefetch=1,
            grid=(n // tm,),
            in_specs=[pl.BlockSpec((tm * TOP_K,), lambda i, tl: (i,), memory_space=pltpu.SMEM),
                      pl.BlockSpec((tm * ROW_TILE, LANES), lambda i, tl: (i, 0))],
            out_specs=pl.BlockSpec(memory_space=pl.ANY),
            scratch_shapes=[pltpu.VMEM((blk // 2 * ROW_TILE, LANES), hn.dtype), pltpu.SemaphoreType.DMA,
                            pltpu.SemaphoreType.DMA]),
        compiler_params=pltpu.CompilerParams(dimension_semantics=("arbitrary",),
                                             vmem_limit_bytes=VMEM_LIMIT, has_side_effects=True),
        name="moe_dispatch",
    )(tails, pos_rows, hn)


def _expert_kernel(be_ref, nu_ref, nxt_ref, x_ref, wg_ref, wu_ref, wd_ref, y_ref, wg_f, wu_f, wd_f, wg_b,
                   wu_b, wd_b, x_scr, y_scr, sem):
    b = pl.program_id(0)

    def fetch(e):
        return [pltpu.make_async_copy(src.at[e], dst, sem)
                for src, dst in ((wg_ref, wg_f), (wu_ref, wu_f), (wd_ref, wd_f))]

    @pl.when(b < nu_ref[0])
    def _():
        e = be_ref[b]

        @pl.when(b == 0)
        def _():
            for copy in fetch(e):
                copy.start()

        @pl.when((b == 0) | (e != be_ref[jnp.maximum(b - 1, 0)]))
        def _():
            for copy in fetch(e):
                copy.wait()
            wg_b[...] = wg_f[...].astype(BF16)
            wu_b[...] = wu_f[...].astype(BF16)
            wd_b[...] = wd_f[...].astype(BF16)

            @pl.when(nxt_ref[e] >= 0)
            def _():
                for copy in fetch(nxt_ref[e]):
                    copy.start()

        sub_rows = x_scr.shape[2] // SUBLANES * ROW_TILE
        for c in range(x_ref.shape[0] // sub_rows):
            rows = pl.ds(c * sub_rows, sub_rows)
            x = _load_row_tiles(x_ref.at[rows, :], x_scr.at[c])
            _store_row_tiles(y_ref.at[rows, :], _swiglu(x, wg_b, wu_b, wd_b), y_scr.at[c])


def _experts(xs, block_expert, n_used, next_expert, n_blocks, wg, wu, wd):
    d, ff = wg.shape[1], wg.shape[2]
    rows = MOE_BLOCK * ROW_TILE
    sub = min(MOE_BLOCK, MOE_SUB_ROWS)
    relayout = _row_tile_scratch(sub, MOE_BLOCK // sub)
    used = lambda b, be, nu, nxt: (jnp.minimum(b, nu[0] - 1), 0)
    hbm = pl.BlockSpec(memory_space=pl.ANY)
    return pl.pallas_call(
        _expert_kernel,
        out_shape=jax.ShapeDtypeStruct(xs.shape, xs.dtype),
        grid_spec=pltpu.PrefetchScalarGridSpec(
            num_scalar_prefetch=3,
            grid=(n_blocks,),
            in_specs=[pl.BlockSpec((rows, LANES), used), hbm, hbm, hbm],
            out_specs=pl.BlockSpec((rows, LANES), used),
            scratch_shapes=[pltpu.VMEM((d, ff), F32), pltpu.VMEM((d, ff), F32), pltpu.VMEM((ff, d), F32),
                            pltpu.VMEM((d, ff), BF16), pltpu.VMEM((d, ff), BF16), pltpu.VMEM((ff, d), BF16),
                            relayout, relayout, pltpu.SemaphoreType.DMA]),
        compiler_params=_params("arbitrary"),
        name="moe_experts",
    )(block_expert, n_used, next_expert, xs, wg, wu, wd)


def _combine_kernel(pos_ref, pos_next_ref, base_ref, wt_ref, y_ref, o_ref, buf, scr, sem):
    i = pl.program_id(0)
    tm = base_ref.shape[0]
    cur = i % 2

    def tile_rows(start):
        return pl.ds(pl.multiple_of(start, ROW_TILE), ROW_TILE)

    def row_copy(half, r, j, src_row):
        return pltpu.make_async_copy(y_ref.at[tile_rows(src_row), :],
                                     buf.at[half, j, tile_rows(r * ROW_TILE), :], sem.at[half])

    def request(slots_ref, half):
        def issue(r, carry):
            for j in range(TOP_K):
                row_copy(half, r, j, slots_ref[r * TOP_K + j]).start(priority=j % DMA_THREADS)
            return carry

        lax.fori_loop(0, tm, issue, 0)

    @pl.when(i == 0)
    def _():
        request(pos_ref, 0)

    @pl.when(i + 1 < pl.num_programs(0))
    def _():
        request(pos_next_ref, 1 - cur)

    for j in range(TOP_K):
        pltpu.make_async_copy(y_ref.at[pl.ds(0, tm * ROW_TILE), :], buf.at[cur, j], sem.at[cur]).wait()

    for j in range(TOP_K):
        _row_tiles_to_scratch(buf.at[cur, j], scr.at[j])
    wt = wt_ref[...]
    for a in range(ROW_TILE):
        cols = slice(a * LANES, (a + 1) * LANES)
        acc = base_ref[:, cols]
        for j in range(TOP_K):
            acc = acc + wt[:, j:j + 1] * _scratch_chunk(scr.at[j], a)
        o_ref[:, cols] = acc


def _combine(base, wts, pos_flat, y, tm):
    n, d = base.shape
    steps = n // tm
    row = pl.BlockSpec((tm, d), lambda i: (i, 0))
    slots = lambda im: pl.BlockSpec((tm * TOP_K,), im, memory_space=pltpu.SMEM)
    return pl.pallas_call(
        _combine_kernel,
        out_shape=jax.ShapeDtypeStruct((n, d), F32),
        grid=(steps,),
        in_specs=[slots(lambda i: (i,)), slots(lambda i: (jnp.minimum(i + 1, steps - 1),)),
                  row, pl.BlockSpec((tm, LANES), lambda i: (i, 0)), pl.BlockSpec(memory_space=pl.ANY)],
        out_specs=row,
        scratch_shapes=[pltpu.VMEM((2, TOP_K, tm * ROW_TILE, LANES), y.dtype), _row_tile_scratch(tm, TOP_K),
                        pltpu.SemaphoreType.DMA((2,))],
        compiler_params=_params("arbitrary"),
        name="moe_combine",
    )(pos_flat, pos_flat, base, wts, y)


def _tile(n, want):
    t = min(n, want)
    assert n % t == 0, (n, want)
    return t


def _layer(layer, h, mem, positions, mix_norm, w_in, a_q_gain, a_k_gain, lower_raw, hg_out_gain, x_q_gain,
           x_k_gain, mem_norm, w_mem_kv, w_branch_a, w_branch_b, w_branch_c, w_out, ffn_norm, w_router,
           router_bias, w_exp_gate, w_exp_up, w_exp_down, w_sh_gate, w_sh_up, w_sh_down):
    b, s, d = h.shape
    n = b * s
    x2 = h.reshape(n, d)
    tm = _tile(n, 1024)
    bf = lambda w: w.astype(BF16)

    xn = _rms_norm_rows(x2, mix_norm, _tile(n, 512))
    cos, sin = _rope_tables(positions, _tile(n, 1024))
    w_in = bf(w_in)
    tmd = _tile(s, tm)
    rope_in = lambda gain: ((cos, (tmd, HEAD_DIM), lambda i, j: (i, 0)),
                            (sin, (tmd, HEAD_DIM), lambda i, j: (i, 0)),
                            (gain.reshape(1, HEAD_DIM), (1, HEAD_DIM), lambda i, j: (0, 0)))
    proj = functools.partial(_proj, xn, w_in, tm=tm)
    dilated = functools.partial(_proj_dilated, xn, w_in, tm=tmd, seq=s)
    hg_w = lower_raw.shape[1]
    off = 0
    aq = dilated(functools.partial(_head_qk, HEAD_DIM ** -0.5), n0=off, extras=rope_in(a_q_gain),
                 name="proj_aq")
    off += A_WIDTH
    ak = dilated(functools.partial(_head_qk, 1.0), n0=off, extras=rope_in(a_k_gain), name="proj_ak")
    off += A_WIDTH
    av = dilated(_head_plain, n0=off, name="proj_av")
    off += A_WIDTH
    hq, = proj(_ep_silu, n0=off, n=hg_w, tn=512, out_dtypes=[BF16], name="proj_hq")
    off += hg_w
    logf, = proj(functools.partial(_ep_log_forget, layer), n0=off, n=hg_w, tn=512, out_dtypes=[F32],
                 extras=((lower_raw, (lower_raw.shape[0], 512), lambda i, j: (0, j)),), name="proj_hf")
    off += hg_w
    hiv, = proj(_ep_plain, n0=off, n=hg_w, tn=512, out_dtypes=[BF16], name="proj_hi")
    off += hg_w
    hgate, = proj(_ep_plain, n0=off, n=hg_w, tn=512, out_dtypes=[BF16], name="proj_hgate")
    off += hg_w
    xq, = _proj(xn, w_in[:, off:off + X_WIDTH], functools.partial(_ep_headnorm, X_HEAD_DIM, X_HEAD_DIM ** -0.5),
                n0=0, n=X_WIDTH, tn=2 * X_HEAD_DIM, tm=tm, out_dtypes=[BF16],
                extras=((x_q_gain.reshape(1, X_HEAD_DIM), (1, X_HEAD_DIM), lambda i, j: (0, 0)),),
                name="proj_xq")
    off += X_WIDTH
    gates, = proj(_ep_sigmoid, n0=off, n=3 * d, tn=512, out_dtypes=[BF16], name="proj_gates")

    r3 = lambda t: t.reshape(b, s, t.shape[1])
    ya = _dilated_attention(list(zip(aq, ak, av)), b, s).reshape(n, A_OUT)
    yb = _hgrn2(r3(hq), r3(logf), r3(hiv), r3(hgate), hg_out_gain, HG_HEADS_PER_STEP).reshape(n, hg_w)
    nm = mem.shape[0] * mem.shape[1]
    mem_n = _rms_norm_rows(mem.reshape(nm, d), mem_norm, _tile(nm, 512))
    w_kv = bf(w_mem_kv)
    tmm = _tile(nm, 1024)
    kn, = _proj(mem_n, w_kv, functools.partial(_ep_headnorm, X_HEAD_DIM, 1.0), n0=0, n=X_WIDTH,
                tn=2 * X_HEAD_DIM, tm=tmm, out_dtypes=[BF16],
                extras=((x_k_gain.reshape(1, X_HEAD_DIM), (1, X_HEAD_DIM), lambda i, j: (0, 0)),),
                name="proj_mem_k")
    vm, = _proj(mem_n, w_kv, _ep_plain, n0=X_WIDTH, n=X_WIDTH, tn=2 * X_HEAD_DIM, tm=tmm,
                out_dtypes=[BF16], name="proj_mem_v")
    rm = lambda t: t.reshape(b, mem.shape[1], X_WIDTH)
    yc = _memory_attention(r3(xq), rm(kn), rm(vm), _tile(s, 512)).reshape(n, X_WIDTH)

    merged = _merge_branches(ya, yb, yc, bf(w_branch_a), bf(w_branch_b), bf(w_branch_c), gates, tm, 512)
    h1, hn, logits = _out_projection(merged, bf(w_out), x2, ffn_norm, bf(w_router), _tile(n, 512))

    tr = _tile(n, 256)
    idx, wts, rank, counts, base = _route(logits, router_bias, h1, hn, bf(w_sh_gate), bf(w_sh_up),
                                          bf(w_sh_down), _tile(n, 512))
    blk = MOE_BLOCK
    counts = counts[0]
    padded = (counts + blk - 1) // blk * blk
    ends = jnp.cumsum(padded)
    starts = ends - padded
    hit = idx[:, :TOP_K, None] == jnp.arange(N_EXPERTS, dtype=I32)
    pos = (jnp.sum(jnp.where(hit, starts, 0), axis=-1) + rank[:, :TOP_K]).reshape(-1).astype(I32)
    pos = pos * ROW_TILE
    n_blocks = -(-(n * TOP_K + N_EXPERTS * (blk - 1)) // blk)
    n_used = (ends[-1] // blk).astype(I32)
    blocks = jnp.minimum(jnp.arange(n_blocks, dtype=I32), n_used - 1)
    block_expert = jnp.minimum(jnp.sum(ends[None, :] <= blocks[:, None] * blk, axis=1), N_EXPERTS - 1)
    tails = (jnp.concatenate([starts + counts, padded - counts]) * ROW_TILE).astype(I32)
    xs = _dispatch(hn, pos, tails, n_blocks, tr)
    later = lax.cummin(jnp.where(counts > 0, jnp.arange(N_EXPERTS, dtype=I32), N_EXPERTS), reverse=True)
    later = jnp.concatenate([later[1:], jnp.full((1,), N_EXPERTS, I32)])
    next_expert = jnp.where(later < N_EXPERTS, later, -1).astype(I32)
    y = _experts(xs, block_expert.astype(I32), n_used.reshape(1), next_expert, n_blocks, w_exp_gate,
                 w_exp_up, w_exp_down)
    out = _combine(base, wts, pos, y, _tile(n, 128))
    return out.reshape(b, s, d)


def kernel(x, mem, positions, mix_norm, w_in, a_q_gain, a_k_gain, hg_lower_bounds, hg_out_gain, x_q_gain, x_k_gain, mem_norm, w_mem_kv, w_branch_a, w_branch_b, w_branch_c, w_out, ffn_norm, w_router, router_bias, w_exp_gate, w_exp_up, w_exp_down, w_sh_gate, w_sh_up, w_sh_down):
    h = x
    for layer in range(w_in.shape[0]):
        h = _layer(layer, h, mem, positions, mix_norm[layer], w_in[layer], a_q_gain[layer], a_k_gain[layer],
                   hg_lower_bounds, hg_out_gain[layer], x_q_gain[layer], x_k_gain[layer], mem_norm[layer],
                   w_mem_kv[layer], w_branch_a[layer], w_branch_b[layer], w_branch_c[layer], w_out[layer],
                   ffn_norm[layer], w_router[layer], router_bias[layer], w_exp_gate[layer],
                   w_exp_up[layer], w_exp_down[layer], w_sh_gate[layer], w_sh_up[layer], w_sh_down[layer])
    return h
```

```python
import functools
import math

import jax
import jax.numpy as jnp
from jax import lax
from jax.experimental import pallas as pl
from jax.experimental.pallas import tpu as pltpu

F32, BF16, I32 = jnp.float32, jnp.bfloat16, jnp.int32

EPS = 1e-6
ROPE_THETA = 10000.0
HEAD_DIM = 128
DILATIONS = (1, 4, 16)
DIL_BACK = 128
HEADS_PER_GROUP = 4
A_WIDTH = len(DILATIONS) * HEADS_PER_GROUP * HEAD_DIM
A_OUT = HEADS_PER_GROUP * HEAD_DIM
HG_DIM = 128
HG_CHUNK = 64
HG_STRIP = 8
HG_HEADS_PER_STEP = 8
X_HEADS = 4
X_HEAD_DIM = 384
X_WIDTH = X_HEADS * X_HEAD_DIM
N_EXPERTS = 64
TOP_K = 8
ROUTED_SCALE = 2.5
MOE_BLOCK = 512
MOE_SUB_ROWS = 256
PROJ_SUB_ROWS = 256
DMA_THREADS = 2
LANES = 128
ROW_TILE = 16
SUBLANES = 8
VMEM_LIMIT = 56 * 1024 * 1024

NT = (((1,), (1,)), ((), ()))
TN = (((0,), (0,)), ((), ()))


def _params(*sem):
    return pltpu.CompilerParams(dimension_semantics=sem, vmem_limit_bytes=VMEM_LIMIT)


def _rms(x, gain):
    return x * lax.rsqrt(jnp.mean(x * x, axis=-1, keepdims=True) + EPS) * gain


def _silu(x):
    return x * jax.nn.sigmoid(x)


def _bdot(a, b):
    return jnp.dot(a.astype(BF16), b.astype(BF16), preferred_element_type=F32)


def _norm_kernel(x_ref, g_ref, o_ref):
    o_ref[...] = _rms(x_ref[...].astype(F32), g_ref[...]).astype(o_ref.dtype)


def _rms_norm_rows(x, gain, tm):
    m, d = x.shape
    return pl.pallas_call(
        _norm_kernel,
        out_shape=jax.ShapeDtypeStruct((m, d), BF16),
        grid=(m // tm,),
        in_specs=[pl.BlockSpec((tm, d), lambda i: (i, 0)), pl.BlockSpec((1, d), lambda i: (0, 0))],
        out_specs=pl.BlockSpec((tm, d), lambda i: (i, 0)),
        compiler_params=_params("parallel"),
        name="rms_norm_rows",
    )(x, gain.reshape(1, d))


def _rope_table_kernel(pos_ref, invf_ref, cos_ref, sin_ref):
    ang = pos_ref[...].astype(F32) * invf_ref[...]
    lane = lax.broadcasted_iota(I32, ang.shape, 1)
    sin = jnp.sin(ang)
    cos_ref[...] = jnp.cos(ang)
    sin_ref[...] = jnp.where(lane < HEAD_DIM // 2, -sin, sin)


def _rope_tables(positions, tm):
    n = positions.size
    half = HEAD_DIM // 2
    inv_freq = ROPE_THETA ** (-jnp.arange(half, dtype=F32) / half)
    invf = jnp.concatenate([inv_freq, inv_freq]).reshape(1, HEAD_DIM)
    return pl.pallas_call(
        _rope_table_kernel,
        out_shape=[jax.ShapeDtypeStruct((n, HEAD_DIM), F32)] * 2,
        grid=(n // tm,),
        in_specs=[pl.BlockSpec((tm, 1), lambda i: (i, 0)), pl.BlockSpec((1, HEAD_DIM), lambda i: (0, 0))],
        out_specs=[pl.BlockSpec((tm, HEAD_DIM), lambda i: (i, 0))] * 2,
        compiler_params=_params("parallel"),
        name="rope_tables",
    )(positions.reshape(n, 1), invf)


def _proj(a, w, epilogue, *, n0, n, tn, tm, out_dtypes, extras=(), name):
    m, k = a.shape
    assert n0 % tn == 0 and n % tn == 0 and m % tm == 0, (n0, n, tn, m, tm)
    j0 = n0 // tn
    n_extra = len(extras)
    sub = min(tm, PROJ_SUB_ROWS)

    def kern(a_ref, w_ref, *refs):
        for c in range(tm // sub):
            rows = slice(c * sub, (c + 1) * sub)
            acc = jnp.dot(a_ref[rows, :], w_ref[...], preferred_element_type=F32)
            epilogue(acc, rows, refs[:n_extra], refs[n_extra:])

    in_specs = [pl.BlockSpec((tm, k), lambda i, j: (i, 0)),
                pl.BlockSpec((k, tn), lambda i, j: (0, j + j0))]
    in_specs += [pl.BlockSpec(bs, im) for _, bs, im in extras]
    return pl.pallas_call(
        kern,
        out_shape=[jax.ShapeDtypeStruct((m, n), dt) for dt in out_dtypes],
        grid=(m // tm, n // tn),
        in_specs=in_specs,
        out_specs=[pl.BlockSpec((tm, tn), lambda i, j: (i, j)) for _ in out_dtypes],
        compiler_params=_params("parallel", "arbitrary"),
        name=name,
    )(a, w, *[e[0] for e in extras])


def _ep_plain(acc, rows, ins, outs):
    outs[0][rows, :] = acc.astype(outs[0].dtype)


def _ep_silu(acc, rows, ins, outs):
    outs[0][rows, :] = _silu(acc).astype(outs[0].dtype)


def _ep_sigmoid(acc, rows, ins, outs):
    outs[0][rows, :] = jax.nn.sigmoid(acc).astype(outs[0].dtype)


def _head_qk(scale, x, rows, ins):
    y = _rms(x, ins[2][...])
    return (y * ins[0][rows, :] + pltpu.roll(y, HEAD_DIM // 2, axis=1) * ins[1][rows, :]) * scale


def _head_plain(x, rows, ins):
    return x


def _proj_dilated(a, w, head_fn, *, n0, tm, seq, extras=(), name):
    m, k = a.shape
    tn = A_OUT
    assert n0 % tn == 0 and seq % tm == 0 and m % seq == 0, (n0, tn, seq, tm, m)
    j0 = n0 // tn
    tiles = seq // tm
    n_extra = len(extras)
    sub = min(tm, PROJ_SUB_ROWS)
    n_groups = len(DILATIONS)

    def kern(a_ref, w_ref, *refs):
        ins, outs, scr = refs[:n_extra], refs[n_extra:n_extra + n_groups], refs[-1]
        for g, d in enumerate(DILATIONS):
            @pl.when(pl.program_id(1) == g)
            def _(g=g, d=d):
                per = sub // d
                for c in range(tm // sub):
                    rows = slice(c * sub, (c + 1) * sub)
                    acc = jnp.dot(a_ref[rows, :], w_ref[...], preferred_element_type=F32)
                    for h in range(HEADS_PER_GROUP):
                        cols = slice(h * HEAD_DIM, (h + 1) * HEAD_DIM)
                        scr[c, h] = head_fn(acc[:, cols], rows, ins)
                        for r in range(d):
                            outs[g][r, c * per:(c + 1) * per, cols] = (
                                scr[c, h, pl.ds(r, per, stride=d), :].astype(outs[g].dtype))

    in_specs = [pl.BlockSpec((tm, k), lambda i, j: (i, 0)),
                pl.BlockSpec((k, tn), lambda i, j: (0, j + j0))]
    in_specs += [pl.BlockSpec(bs, im) for _, bs, im in extras]
    return pl.pallas_call(
        kern,
        out_shape=[jax.ShapeDtypeStruct((m // seq, d, seq // d, tn), BF16) for d in DILATIONS],
        grid=(m // tm, n_groups),
        in_specs=in_specs,
        out_specs=[pl.BlockSpec((None, d, tm // d, tn), lambda i, j: (i // tiles, 0, i % tiles, 0))
                   for d in DILATIONS],
        scratch_shapes=[pltpu.VMEM((tm // sub, HEADS_PER_GROUP, sub, HEAD_DIM), F32)],
        compiler_params=_params("parallel", "arbitrary"),
        name=name,
    )(a, w, *[e[0] for e in extras])


def _ep_headnorm(width, scale, acc, rows, ins, outs):
    gain = ins[0][...]
    for h in range(acc.shape[1] // width):
        sl = slice(h * width, (h + 1) * width)
        outs[0][rows, sl] = (_rms(acc[:, sl], gain) * scale).astype(outs[0].dtype)


def _ep_log_forget(layer, acc, rows, ins, outs):
    raw = ins[0][...]
    mx = jnp.max(raw, axis=0, keepdims=True)
    ex = jnp.exp(raw - mx)
    lb = jnp.sum(ex[:layer + 1], axis=0, keepdims=True) / jnp.sum(ex, axis=0, keepdims=True)
    outs[0][rows, :] = jnp.log(lb + (1.0 - lb) * jax.nn.sigmoid(acc))


def _dil_attn_kernel(*refs, seq):
    n_groups = len(DILATIONS)
    o_ref, o_scr, l_scr = refs[3 * n_groups:]
    blk = DIL_BACK

    for g, d in enumerate(DILATIONS):
        q_ref, k_ref, v_ref = refs[3 * g:3 * g + 3]
        length = seq // d
        nqb = max(length // blk, 1)
        qrows = min(blk, length)
        width = min(2 * blk, length)

        for r in range(d):
            for qb in range(nqb):
                q0 = qb * qrows
                ks = min(max(q0 + qrows - width, 0), length - width)
                q = q_ref[r, q0:q0 + qrows, :]
                k = k_ref[r, ks:ks + width, :]
                v = v_ref[r, ks:ks + width, :]
                s = lax.dot_general(q, k, NT, preferred_element_type=F32)
                rel = ((q0 - ks) + lax.broadcasted_iota(I32, s.shape, 0)
                       - lax.broadcasted_iota(I32, s.shape, 1))
                s = jnp.where((rel >= 0) & (rel <= DIL_BACK), s, -jnp.inf)
                m = jnp.max(s, axis=-1, keepdims=True)
                p = jnp.exp(s - m)
                den = jnp.sum(p, axis=-1, keepdims=True)
                o = jnp.dot((p / den).astype(BF16), v, preferred_element_type=F32)
                rows = pl.ds(r + d * q0, qrows, stride=d)
                o_scr[g, rows, :] = o
                l_scr[g, rows, :] = jnp.broadcast_to(m + jnp.log(den), o.shape)

    def merge(qi, carry):
        rows = pl.ds(pl.multiple_of(qi * blk, blk), blk)
        lses = [l_scr[g, rows, :] for g in range(n_groups)]
        top = functools.reduce(jnp.maximum, lses)
        ws = [jnp.exp(l - top) for l in lses]
        ya = sum(w * o_scr[g, rows, :] for g, w in enumerate(ws)) / sum(ws)
        o_ref[rows, :] = ya.astype(o_ref.dtype)
        return carry

    lax.fori_loop(0, seq // blk, merge, 0)


def _dilated_attention(qkv, b, s):
    in_specs, args = [], []
    for g, d in enumerate(DILATIONS):
        for t in qkv[g]:
            in_specs.append(pl.BlockSpec((None, d, s // d, HEAD_DIM), lambda bi, h: (bi, 0, 0, h)))
            args.append(t)
    return pl.pallas_call(
        functools.partial(_dil_attn_kernel, seq=s),
        out_shape=jax.ShapeDtypeStruct((b, s, A_OUT), BF16),
        grid=(b, HEADS_PER_GROUP),
        in_specs=in_specs,
        out_specs=pl.BlockSpec((None, s, HEAD_DIM), lambda bi, h: (bi, 0, h)),
        scratch_shapes=[pltpu.VMEM((len(DILATIONS), s, HEAD_DIM), F32)] * 2,
        compiler_params=_params("parallel", "parallel"),
        name="dilated_attention",
    )(*args)


def _hgrn_kernel(q_ref, lf_ref, v_ref, gate_ref, gain_ref, o_ref, st_ref, q_scr, v_scr, l_scr, d_scr,
                 g_scr, k_scr, *, seq, heads):
    C, R, W = HG_CHUNK, HG_STRIP, HG_DIM
    ns = C // R
    st_ref[...] = jnp.zeros_like(st_ref)
    gain = gain_ref[...]
    row_s = lax.broadcasted_iota(I32, (ns, W), 0)
    far_rows = R * ns * (ns - 1) // 2
    rr = lax.broadcasted_iota(I32, (far_rows, C), 0)
    cc = lax.broadcasted_iota(I32, (far_rows, C), 1)
    keep, start = None, 0
    for j in range(ns - 1):
        size = (ns - 1 - j) * R
        blk = (rr >= start) & (rr < start + size) & (cc >= j * R) & (cc < (j + 1) * R)
        keep = blk if keep is None else keep | blk
        start += size

    def chunk(hd, r0):
        cols = slice(hd * W, (hd + 1) * W)
        qf = q_ref[pl.ds(r0, C), cols].astype(F32)
        v = v_ref[pl.ds(r0, C), cols]
        vf = v.astype(F32)
        q_scr[hd] = qf
        v_scr[hd] = vf
        l_scr[hd] = lf_ref[pl.ds(r0, C), cols]
        rows_t = lambda t: pl.ds(t, ns, stride=R)
        lt = [l_scr[hd, rows_t(t), :] for t in range(R)]
        ft = [jnp.exp(l) for l in lt]
        vt_ = [v_scr[hd, rows_t(t), :] for t in range(R)]
        pre = [lt[0]]
        for t in range(1, R):
            pre.append(pre[-1] + lt[t])
        total = pre[R - 1]
        before = total
        step = 1
        while step < ns:
            before = before + jnp.where(row_s >= step, pltpu.roll(before, step, axis=0), 0.0)
            step *= 2
        before = before - total
        kt_ = [1.0 - f for f in ft]
        for t in range(R):
            g_scr[hd, rows_t(t), :] = pre[t] + before
            k_scr[hd, rows_t(t), :] = kt_[t]
        G = g_scr[hd]
        kf = k_scr[hd]
        st = st_ref[hd]
        o = lax.dot_general((qf * jnp.exp(G)).astype(BF16), st.astype(BF16), NT,
                            preferred_element_type=F32)
        decay = []
        for t in range(R):
            decay = [dcy * ft[t] for dcy in decay]
            if t > 0:
                decay.append(ft[t])
            qt_ = q_scr[hd, rows_t(t), :]
            acc = jnp.sum(qt_ * kt_[t], axis=-1, keepdims=True) * vt_[t]
            for s in range(t):
                a = jnp.sum(qt_ * kt_[s] * decay[s], axis=-1, keepdims=True)
                acc = acc + a * vt_[s]
            d_scr[hd, rows_t(t), :] = acc
        o = o + d_scr[hd]
        G3, k3 = G.reshape(ns, R, W), kf.reshape(ns, R, W)
        kt = (k3 * jnp.exp(G3[:, R - 1:R] - G3)).reshape(C, W).astype(BF16)
        qt = jnp.concatenate([qf[j * R:] * jnp.exp(G[j * R:] - G[j * R - 1:j * R]) for j in range(1, ns)],
                             axis=0).astype(BF16)
        a = lax.dot_general(qt, kt, NT, preferred_element_type=F32)
        far = jnp.dot(jnp.where(keep, a, 0.0).astype(BF16), v, preferred_element_type=F32)
        pieces = [o[i * R:(i + 1) * R] for i in range(ns)]
        start = 0
        for j in range(ns - 1):
            for i in range(j + 1, ns):
                pieces[i] = pieces[i] + far[start:start + R]
                start += R
        o = jnp.concatenate(pieces, axis=0)
        y = _rms(o, gain) * _silu(gate_ref[pl.ds(r0, C), cols].astype(F32))
        o_ref[pl.ds(r0, C), cols] = y.astype(o_ref.dtype)
        g_last = G[C - 1:C]
        kd = (kf * jnp.exp(g_last - G)).astype(BF16)
        st_ref[hd] = st * jnp.exp(g_last) + lax.dot_general(v, kd, TN, preferred_element_type=F32)

    def body(c, carry):
        r0 = pl.multiple_of(c * C, C)
        for hd in range(heads):
            chunk(hd, r0)
        return carry

    lax.fori_loop(0, seq // C, body, 0)


def _hgrn2(q, logf, v, gate, out_gain, heads):
    b, s, width = q.shape
    spec = pl.BlockSpec((None, s, heads * HG_DIM), lambda bi, h: (bi, 0, h))
    return pl.pallas_call(
        functools.partial(_hgrn_kernel, seq=s, heads=heads),
        out_shape=jax.ShapeDtypeStruct((b, s, width), BF16),
        grid=(b, width // (heads * HG_DIM)),
        in_specs=[spec, spec, spec, spec, pl.BlockSpec((1, HG_DIM), lambda bi, h: (0, 0))],
        out_specs=spec,
        scratch_shapes=[pltpu.VMEM((heads, HG_DIM, HG_DIM), F32)]
        + [pltpu.VMEM((heads, HG_CHUNK, HG_DIM), F32)] * 6,
        compiler_params=_params("parallel", "parallel"),
        name="hgrn2",
    )(q, logf, v, gate, out_gain.reshape(1, HG_DIM))


def _xattn_kernel(q_ref, k_ref, v_ref, o_ref):
    for h in range(X_HEADS):
        cols = slice(h * X_HEAD_DIM, (h + 1) * X_HEAD_DIM)
        s = lax.dot_general(q_ref[:, cols], k_ref[:, cols], NT, preferred_element_type=F32)
        p = jnp.exp(s - jnp.max(s, axis=-1, keepdims=True))
        p = p / jnp.sum(p, axis=-1, keepdims=True)
        o_ref[:, cols] = jnp.dot(p.astype(BF16), v_ref[:, cols],
                                 preferred_element_type=F32).astype(o_ref.dtype)


def _memory_attention(q, k, v, ts):
    b, s, _ = q.shape
    m = k.shape[1]
    return pl.pallas_call(
        _xattn_kernel,
        out_shape=jax.ShapeDtypeStruct((b, s, X_WIDTH), BF16),
        grid=(b, s // ts),
        in_specs=[pl.BlockSpec((None, ts, X_WIDTH), lambda bi, si: (bi, si, 0)),
                  pl.BlockSpec((None, m, X_WIDTH), lambda bi, si: (bi, 0, 0)),
                  pl.BlockSpec((None, m, X_WIDTH), lambda bi, si: (bi, 0, 0))],
        out_specs=pl.BlockSpec((None, ts, X_WIDTH), lambda bi, si: (bi, si, 0)),
        compiler_params=_params("parallel", "parallel"),
        name="memory_attention",
    )(q, k, v)


def _merge_kernel(ya_ref, yb_ref, yc_ref, wa_ref, wb_ref, wc_ref, ga_ref, gb_ref, gc_ref, o_ref):
    acc = ga_ref[...].astype(F32) * jnp.dot(ya_ref[...], wa_ref[...], preferred_element_type=F32)
    acc += gb_ref[...].astype(F32) * jnp.dot(yb_ref[...], wb_ref[...], preferred_element_type=F32)
    acc += gc_ref[...].astype(F32) * jnp.dot(yc_ref[...], wc_ref[...], preferred_element_type=F32)
    o_ref[...] = acc.astype(o_ref.dtype)


def _merge_branches(ya, yb, yc, wa, wb, wc, gates, tm, tn):
    n, d = yb.shape[0], wa.shape[1]
    nj = d // tn
    row = lambda a: pl.BlockSpec((tm, a.shape[1]), lambda i, j: (i, 0))
    col = lambda w: pl.BlockSpec((w.shape[0], tn), lambda i, j: (0, j))
    gate = lambda br: pl.BlockSpec((tm, tn), lambda i, j: (i, br * nj + j))
    return pl.pallas_call(
        _merge_kernel,
        out_shape=jax.ShapeDtypeStruct((n, d), BF16),
        grid=(n // tm, nj),
        in_specs=[row(ya), row(yb), row(yc), col(wa), col(wb), col(wc), gate(0), gate(1), gate(2)],
        out_specs=pl.BlockSpec((tm, tn), lambda i, j: (i, j)),
        compiler_params=_params("parallel", "arbitrary"),
        name="merge_branches",
    )(ya, yb, yc, wa, wb, wc, gates, gates, gates)


def _store_row_tiles(ref, x, scr):
    m = x.shape[0]
    assert x.shape[1] == ROW_TILE * LANES, x.shape
    for a in range(ROW_TILE):
        scr[a // SUBLANES, pl.ds(a % SUBLANES, m, stride=SUBLANES), :] = x[:, a * LANES:(a + 1) * LANES]
    groups = [scr[t].reshape(m, SUBLANES, LANES) for t in range(ROW_TILE // SUBLANES)]
    ref[...] = jnp.stack(groups, axis=1).reshape(m * ROW_TILE, LANES).astype(ref.dtype)


def _row_tile_scratch(m, *lead):
    return pltpu.VMEM((*lead, ROW_TILE // SUBLANES, m * SUBLANES, LANES), F32)


def _row_tiles_to_scratch(ref, scr):
    m = scr.shape[1] // SUBLANES
    x = ref[...].astype(F32).reshape(m, ROW_TILE // SUBLANES, SUBLANES, LANES)
    for t in range(ROW_TILE // SUBLANES):
        scr[t] = x[:, t].reshape(m * SUBLANES, LANES)


def _scratch_chunk(scr, a):
    m = scr.shape[1] // SUBLANES
    return scr[a // SUBLANES, pl.ds(a % SUBLANES, m, stride=SUBLANES), :]


def _load_row_tiles(ref, scr):
    _row_tiles_to_scratch(ref, scr)
    return jnp.concatenate([_scratch_chunk(scr, a) for a in range(ROW_TILE)], axis=1)


def _swiglu(x, wg_ref, wu_ref, wd_ref):
    x = x.astype(BF16)
    g = jnp.dot(x, wg_ref[...], preferred_element_type=F32)
    u = jnp.dot(x, wu_ref[...], preferred_element_type=F32)
    return _bdot(_silu(g) * u, wd_ref[...])


def _outproj_kernel(m_ref, w_ref, x_ref, g_ref, wr_ref, h_ref, hn_ref, lg_ref, scr):
    tm = m_ref.shape[0]
    sub = scr.shape[2] // SUBLANES
    for c in range(tm // sub):
        rows = slice(c * sub, (c + 1) * sub)
        h = x_ref[rows, :] + jnp.dot(m_ref[rows, :], w_ref[...], preferred_element_type=F32)
        hn = _rms(h, g_ref[...])
        h_ref[rows, :] = h
        _store_row_tiles(hn_ref.at[pl.ds(c * sub * ROW_TILE, sub * ROW_TILE), :], hn, scr.at[c])
        lg_ref[rows, :] = _bdot(hn, wr_ref[...])


def _out_projection(merged, w_out, x, ffn_gain, w_router, tm):
    n, d = x.shape
    e = w_router.shape[1]
    sub = min(tm, PROJ_SUB_ROWS)
    row = pl.BlockSpec((tm, d), lambda i: (i, 0))
    const = lambda a: pl.BlockSpec(a.shape, lambda i: (0, 0))
    g = ffn_gain.reshape(1, d)
    return pl.pallas_call(
        _outproj_kernel,
        out_shape=[jax.ShapeDtypeStruct((n, d), F32), jax.ShapeDtypeStruct((n * ROW_TILE, LANES), BF16),
                   jax.ShapeDtypeStruct((n, e), F32)],
        grid=(n // tm,),
        in_specs=[row, const(w_out), row, const(g), const(w_router)],
        out_specs=[row, pl.BlockSpec((tm * ROW_TILE, LANES), lambda i: (i, 0)),
                   pl.BlockSpec((tm, e), lambda i: (i, 0))],
        scratch_shapes=[_row_tile_scratch(sub, tm // sub)],
        compiler_params=_params("parallel"),
        name="out_projection",
    )(merged, w_out, x, g, w_router)


def _route_kernel(lg_ref, bias_ref, h_ref, hn_ref, sg_ref, su_ref, sd_ref, idx_ref, wt_ref, rank_ref,
                  cnt_ref, base_ref, carry_ref, scr):
    i = pl.program_id(0)

    @pl.when(i == 0)
    def _():
        carry_ref[...] = jnp.zeros_like(carry_ref)

    base_ref[...] = h_ref[...] + _swiglu(_load_row_tiles(hn_ref, scr), sg_ref, su_ref, sd_ref)

    scores = jax.nn.sigmoid(lg_ref[...])
    tm, e = scores.shape
    sel = scores + bias_ref[...]
    lane = lax.broadcasted_iota(I32, (tm, e), 1)
    hots, ws = [], []
    for _ in range(TOP_K):
        mx = jnp.max(sel, axis=-1, keepdims=True)
        ix = jnp.min(jnp.where(sel == mx, lane, e), axis=-1, keepdims=True)
        hot = lane == ix
        hots.append(hot)
        ws.append(jnp.sum(jnp.where(hot, scores, 0.0), axis=-1, keepdims=True))
        sel = jnp.where(hot, -jnp.inf, sel)
    chosen = functools.reduce(jnp.logical_or, hots).astype(F32)
    r = lax.broadcasted_iota(I32, (tm, tm), 0)
    c = lax.broadcasted_iota(I32, (tm, tm), 1)
    before = (c < r).astype(BF16)
    ahead = jnp.dot(before, chosen.astype(BF16), preferred_element_type=F32) + carry_ref[...]
    carry_ref[...] += jnp.sum(chosen, axis=0, keepdims=True)
    cnt_ref[...] = carry_ref[...].astype(I32)

    norm = ROUTED_SCALE / sum(ws)
    out_lane = lax.broadcasted_iota(I32, (tm, LANES), 1)
    idx_o = jnp.zeros((tm, LANES), I32)
    wt_o = jnp.zeros((tm, LANES), F32)
    rk_o = jnp.zeros((tm, LANES), I32)
    for j in range(TOP_K):
        ix = jnp.sum(jnp.where(hots[j], lane, 0), axis=-1, keepdims=True)
        rk = jnp.sum(jnp.where(hots[j], ahead, 0.0), axis=-1, keepdims=True).astype(I32)
        idx_o = jnp.where(out_lane == j, ix, idx_o)
        wt_o = jnp.where(out_lane == j, ws[j] * norm, wt_o)
        rk_o = jnp.where(out_lane == j, rk, rk_o)
    idx_ref[...] = idx_o
    wt_ref[...] = wt_o
    rank_ref[...] = rk_o


def _route(logits, bias, h, hn, sg, su, sd, tm):
    n, e = logits.shape
    d = h.shape[1]
    row = pl.BlockSpec((tm, LANES), lambda i: (i, 0))
    wide = lambda a: pl.BlockSpec((a.shape[0] // (n // tm), a.shape[1]), lambda i: (i, 0))
    const = lambda a: pl.BlockSpec(a.shape, lambda i: (0, 0))
    bias = bias.reshape(1, e)
    return pl.pallas_call(
        _route_kernel,
        out_shape=[jax.ShapeDtypeStruct((n, LANES), I32), jax.ShapeDtypeStruct((n, LANES), F32),
                   jax.ShapeDtypeStruct((n, LANES), I32), jax.ShapeDtypeStruct((1, e), I32),
                   jax.ShapeDtypeStruct((n, d), F32)],
        grid=(n // tm,),
        in_specs=[wide(logits), const(bias), wide(h), wide(hn), const(sg), const(su), const(sd)],
        out_specs=[row, row, row, pl.BlockSpec((1, e), lambda i: (0, 0)), wide(h)],
        scratch_shapes=[pltpu.VMEM((1, e), F32), _row_tile_scratch(tm)],
        compiler_params=_params("arbitrary"),
        name="route",
    )(logits, bias, h, hn, sg, su, sd)


def _dispatch_kernel(tail_ref, pos_ref, hn_ref, xs_ref, zeros, sem, zsem):
    tm = hn_ref.shape[0] // ROW_TILE
    half = zeros.shape[0]

    def tile_rows(start, size=ROW_TILE):
        return pl.ds(pl.multiple_of(start, ROW_TILE), size)

    @pl.when(pl.program_id(0) == 0)
    def _():
        zeros[...] = jnp.zeros_like(zeros)

        def tail_copies(e):
            off, pad = tail_ref[e], tail_ref[N_EXPERTS + e]
            size = half
            while size >= ROW_TILE:
                yield (pad & size) != 0, pltpu.make_async_copy(
                    zeros.at[pl.ds(0, size), :], xs_ref.at[tile_rows(off, size), :], zsem)
                off = off + (pad & size)
                size //= 2

        def start(e, carry):
            for needed, copy in tail_copies(e):
                pl.when(needed)(copy.start)
            return carry

        def wait(e, carry):
            for needed, copy in tail_copies(e):
                pl.when(needed)(copy.wait)
            return carry

        lax.fori_loop(0, N_EXPERTS, start, 0)
        lax.fori_loop(0, N_EXPERTS, wait, 0)

    def row_copy(r, dst_row):
        return pltpu.make_async_copy(hn_ref.at[tile_rows(r * ROW_TILE), :], xs_ref.at[tile_rows(dst_row), :], sem)

    def issue(r, carry):
        for j in range(TOP_K):
            row_copy(r, pos_ref[r * TOP_K + j]).start(priority=j % DMA_THREADS)
        return carry

    lax.fori_loop(0, tm, issue, 0)
    for j in range(TOP_K):
        pltpu.make_async_copy(hn_ref, xs_ref.at[pl.ds(0, tm * ROW_TILE), :], sem).wait()


def _dispatch(hn, pos_rows, tails, n_blocks, tm):
    n = hn.shape[0] // ROW_TILE
    blk = MOE_BLOCK
    return pl.pallas_call(
        _dispatch_kernel,
        out_shape=jax.ShapeDtypeStruct((n_blocks * blk * ROW_TILE, LANES), hn.dtype),
        grid_spec=pltpu.PrefetchScalarGridSpec(
            num_scalar_prefetch=1,
            grid=(n // tm,),
            in_specs=[pl.BlockSpec((tm * TOP_K,), lambda i, tl: (i,), memory_space=pltpu.SMEM),
                      pl.BlockSpec((tm * ROW_TILE, LANES), lambda i, tl: (i, 0))],
            out_specs=pl.BlockSpec(memory_space=pl.ANY),
            scratch_shapes=[pltpu.VMEM((blk // 2 * ROW_TILE, LANES), hn.dtype), pltpu.SemaphoreType.DMA,
                            pltpu.SemaphoreType.DMA]),
        compiler_params=pltpu.CompilerParams(dimension_semantics=("arbitrary",),
                                             vmem_limit_bytes=VMEM_LIMIT, has_side_effects=True),
        name="moe_dispatch",
    )(tails, pos_rows, hn)


def _expert_kernel(be_ref, nu_ref, nxt_ref, valid_ref, x_ref, wg_ref, wu_ref, wd_ref, y_ref, wg_f, wu_f, wd_f,
                   wg_b, wu_b, wd_b, x_scr, y_scr, sem):
    b = pl.program_id(0)

    def fetch(e):
        return [pltpu.make_async_copy(src.at[e], dst, sem)
                for src, dst in ((wg_ref, wg_f), (wu_ref, wu_f), (wd_ref, wd_f))]

    @pl.when(b < nu_ref[0])
    def _():
        e = be_ref[b]

        @pl.when(b == 0)
        def _():
            for copy in fetch(e):
                copy.start()

        @pl.when((b == 0) | (e != be_ref[jnp.maximum(b - 1, 0)]))
        def _():
            for copy in fetch(e):
                copy.wait()
            wg_b[...] = wg_f[...].astype(BF16)
            wu_b[...] = wu_f[...].astype(BF16)
            wd_b[...] = wd_f[...].astype(BF16)

            @pl.when(nxt_ref[e] >= 0)
            def _():
                for copy in fetch(nxt_ref[e]):
                    copy.start()

        sub = x_scr.shape[2] // SUBLANES
        n_sub = x_ref.shape[0] // (sub * ROW_TILE)

        def run(live):
            for c in range(n_sub):
                rows = pl.ds(c * sub * ROW_TILE, sub * ROW_TILE)
                if c < live:
                    x = _load_row_tiles(x_ref.at[rows, :], x_scr.at[c])
                    _store_row_tiles(y_ref.at[rows, :], _swiglu(x, wg_b, wu_b, wd_b), y_scr.at[c])
                else:
                    y_ref[rows, :] = jnp.zeros((sub * ROW_TILE, LANES), y_ref.dtype)

        live = (valid_ref[b] + sub - 1) // sub
        for k in range(1, n_sub + 1):
            pl.when(live == k)(functools.partial(run, k))


def _experts(xs, block_expert, n_used, next_expert, valid_rows, n_blocks, wg, wu, wd):
    d, ff = wg.shape[1], wg.shape[2]
    rows = MOE_BLOCK * ROW_TILE
    sub = min(MOE_BLOCK, MOE_SUB_ROWS)
    relayout = _row_tile_scratch(sub, MOE_BLOCK // sub)
    used = lambda b, be, nu, nxt, valid: (jnp.minimum(b, nu[0] - 1), 0)
    hbm = pl.BlockSpec(memory_space=pl.ANY)
    return pl.pallas_call(
        _expert_kernel,
        out_shape=jax.ShapeDtypeStruct(xs.shape, xs.dtype),
        grid_spec=pltpu.PrefetchScalarGridSpec(
            num_scalar_prefetch=4,
            grid=(n_blocks,),
            in_specs=[pl.BlockSpec((rows, LANES), used), hbm, hbm, hbm],
            out_specs=pl.BlockSpec((rows, LANES), used),
            scratch_shapes=[pltpu.VMEM((d, ff), F32), pltpu.VMEM((d, ff), F32), pltpu.VMEM((ff, d), F32),
                            pltpu.VMEM((d, ff), BF16), pltpu.VMEM((d, ff), BF16), pltpu.VMEM((ff, d), BF16),
                            relayout, relayout, pltpu.SemaphoreType.DMA]),
        compiler_params=_params("arbitrary"),
        name="moe_experts",
    )(block_expert, n_used, next_expert, valid_rows, xs, wg, wu, wd)


def _combine_kernel(pos_ref, pos_next_ref, base_ref, wt_ref, y_ref, o_ref, buf, scr, sem):
    i = pl.program_id(0)
    tm = base_ref.shape[0]
    cur = i % 2

    def tile_rows(start):
        return pl.ds(pl.multiple_of(start, ROW_TILE), ROW_TILE)

    def row_copy(half, r, j, src_row):
        return pltpu.make_async_copy(y_ref.at[tile_rows(src_row), :],
                                     buf.at[half, j, tile_rows(r * ROW_TILE), :], sem.at[half])

    def request(slots_ref, half):
        def issue(r, carry):
            for j in range(TOP_K):
                row_copy(half, r, j, slots_ref[r * TOP_K + j]).start(priority=j % DMA_THREADS)
            return carry

        lax.fori_loop(0, tm, issue, 0)

    @pl.when(i == 0)
    def _():
        request(pos_ref, 0)

    @pl.when(i + 1 < pl.num_programs(0))
    def _():
        request(pos_next_ref, 1 - cur)

    for j in range(TOP_K):
        pltpu.make_async_copy(y_ref.at[pl.ds(0, tm * ROW_TILE), :], buf.at[cur, j], sem.at[cur]).wait()

    for j in range(TOP_K):
        _row_tiles_to_scratch(buf.at[cur, j], scr.at[j])
    wt = wt_ref[...]
    for a in range(ROW_TILE):
        cols = slice(a * LANES, (a + 1) * LANES)
        acc = base_ref[:, cols]
        for j in range(TOP_K):
            acc = acc + wt[:, j:j + 1] * _scratch_chunk(scr.at[j], a)
        o_ref[:, cols] = acc


def _combine(base, wts, pos_flat, y, tm):
    n, d = base.shape
    steps = n // tm
    row = pl.BlockSpec((tm, d), lambda i: (i, 0))
    slots = lambda im: pl.BlockSpec((tm * TOP_K,), im, memory_space=pltpu.SMEM)
    return pl.pallas_call(
        _combine_kernel,
        out_shape=jax.ShapeDtypeStruct((n, d), F32),
        grid=(steps,),
        in_specs=[slots(lambda i: (i,)), slots(lambda i: (jnp.minimum(i + 1, steps - 1),)),
                  row, pl.BlockSpec((tm, LANES), lambda i: (i, 0)), pl.BlockSpec(memory_space=pl.ANY)],
        out_specs=row,
        scratch_shapes=[pltpu.VMEM((2, TOP_K, tm * ROW_TILE, LANES), y.dtype), _row_tile_scratch(tm, TOP_K),
                        pltpu.SemaphoreType.DMA((2,))],
        compiler_params=_params("arbitrary"),
        name="moe_combine",
    )(pos_flat, pos_flat, base, wts, y)


def _tile(n, want):
    t = min(n, want)
    assert n % t == 0, (n, want)
    return t


def _layer(layer, h, mem, positions, mix_norm, w_in, a_q_gain, a_k_gain, lower_raw, hg_out_gain, x_q_gain,
           x_k_gain, mem_norm, w_mem_kv, w_branch_a, w_branch_b, w_branch_c, w_out, ffn_norm, w_router,
           router_bias, w_exp_gate, w_exp_up, w_exp_down, w_sh_gate, w_sh_up, w_sh_down):
    b, s, d = h.shape
    n = b * s
    x2 = h.reshape(n, d)
    tm = _tile(n, 1024)
    bf = lambda w: w.astype(BF16)

    xn = _rms_norm_rows(x2, mix_norm, _tile(n, 512))
    cos, sin = _rope_tables(positions, _tile(n, 1024))
    w_in = bf(w_in)
    tmd = _tile(s, tm)
    rope_in = lambda gain: ((cos, (tmd, HEAD_DIM), lambda i, j: (i, 0)),
                            (sin, (tmd, HEAD_DIM), lambda i, j: (i, 0)),
                            (gain.reshape(1, HEAD_DIM), (1, HEAD_DIM), lambda i, j: (0, 0)))
    proj = functools.partial(_proj, xn, w_in, tm=tm)
    dilated = functools.partial(_proj_dilated, xn, w_in, tm=tmd, seq=s)
    hg_w = lower_raw.shape[1]
    off = 0
    aq = dilated(functools.partial(_head_qk, HEAD_DIM ** -0.5), n0=off, extras=rope_in(a_q_gain),
                 name="proj_aq")
    off += A_WIDTH
    ak = dilated(functools.partial(_head_qk, 1.0), n0=off, extras=rope_in(a_k_gain), name="proj_ak")
    off += A_WIDTH
    av = dilated(_head_plain, n0=off, name="proj_av")
    off += A_WIDTH
    hq, = proj(_ep_silu, n0=off, n=hg_w, tn=512, out_dtypes=[BF16], name="proj_hq")
    off += hg_w
    logf, = proj(functools.partial(_ep_log_forget, layer), n0=off, n=hg_w, tn=512, out_dtypes=[F32],
                 extras=((lower_raw, (lower_raw.shape[0], 512), lambda i, j: (0, j)),), name="proj_hf")
    off += hg_w
    hiv, = proj(_ep_plain, n0=off, n=hg_w, tn=512, out_dtypes=[BF16], name="proj_hi")
    off += hg_w
    hgate, = proj(_ep_plain, n0=off, n=hg_w, tn=512, out_dtypes=[BF16], name="proj_hgate")
    off += hg_w
    xq, = _proj(xn, w_in[:, off:off + X_WIDTH], functools.partial(_ep_headnorm, X_HEAD_DIM, X_HEAD_DIM ** -0.5),
                n0=0, n=X_WIDTH, tn=2 * X_HEAD_DIM, tm=tm, out_dtypes=[BF16],
                extras=((x_q_gain.reshape(1, X_HEAD_DIM), (1, X_HEAD_DIM), lambda i, j: (0, 0)),),
                name="proj_xq")
    off += X_WIDTH
    gates, = proj(_ep_sigmoid, n0=off, n=3 * d, tn=1024, out_dtypes=[BF16], name="proj_gates")

    r3 = lambda t: t.reshape(b, s, t.shape[1])
    ya = _dilated_attention(list(zip(aq, ak, av)), b, s).reshape(n, A_OUT)
    yb = _hgrn2(r3(hq), r3(logf), r3(hiv), r3(hgate), hg_out_gain, HG_HEADS_PER_STEP).reshape(n, hg_w)
    nm = mem.shape[0] * mem.shape[1]
    mem_n = _rms_norm_rows(mem.reshape(nm, d), mem_norm, _tile(nm, 512))
    w_kv = bf(w_mem_kv)
    tmm = _tile(nm, 1024)
    kn, = _proj(mem_n, w_kv, functools.partial(_ep_headnorm, X_HEAD_DIM, 1.0), n0=0, n=X_WIDTH,
                tn=2 * X_HEAD_DIM, tm=tmm, out_dtypes=[BF16],
                extras=((x_k_gain.reshape(1, X_HEAD_DIM), (1, X_HEAD_DIM), lambda i, j: (0, 0)),),
                name="proj_mem_k")
    vm, = _proj(mem_n, w_kv, _ep_plain, n0=X_WIDTH, n=X_WIDTH, tn=2 * X_HEAD_DIM, tm=tmm,
                out_dtypes=[BF16], name="proj_mem_v")
    rm = lambda t: t.reshape(b, mem.shape[1], X_WIDTH)
    yc = _memory_attention(r3(xq), rm(kn), rm(vm), _tile(s, 512)).reshape(n, X_WIDTH)

    merged = _merge_branches(ya, yb, yc, bf(w_branch_a), bf(w_branch_b), bf(w_branch_c), gates, tm, 512)
    h1, hn, logits = _out_projection(merged, bf(w_out), x2, ffn_norm, bf(w_router), _tile(n, 512))

    tr = _tile(n, 256)
    idx, wts, rank, counts, base = _route(logits, router_bias, h1, hn, bf(w_sh_gate), bf(w_sh_up),
                                          bf(w_sh_down), _tile(n, 512))
    blk = MOE_BLOCK
    counts = counts[0]
    padded = (counts + blk - 1) // blk * blk
    ends = jnp.cumsum(padded)
    starts = ends - padded
    hit = idx[:, :TOP_K, None] == jnp.arange(N_EXPERTS, dtype=I32)
    pos = (jnp.sum(jnp.where(hit, starts, 0), axis=-1) + rank[:, :TOP_K]).reshape(-1).astype(I32)
    pos = pos * ROW_TILE
    n_blocks = -(-(n * TOP_K + N_EXPERTS * (blk - 1)) // blk)
    n_used = (ends[-1] // blk).astype(I32)
    blocks = jnp.minimum(jnp.arange(n_blocks, dtype=I32), n_used - 1)
    block_expert = jnp.minimum(jnp.sum(ends[None, :] <= blocks[:, None] * blk, axis=1), N_EXPERTS - 1)
    tails = (jnp.concatenate([starts + counts, padded - counts]) * ROW_TILE).astype(I32)
    xs = _dispatch(hn, pos, tails, n_blocks, tr)
    later = lax.cummin(jnp.where(counts > 0, jnp.arange(N_EXPERTS, dtype=I32), N_EXPERTS), reverse=True)
    later = jnp.concatenate([later[1:], jnp.full((1,), N_EXPERTS, I32)])
    next_expert = jnp.where(later < N_EXPERTS, later, -1).astype(I32)
    valid_rows = jnp.clip(starts[block_expert] + counts[block_expert] - blocks * blk, 0, blk).astype(I32)
    y = _experts(xs, block_expert.astype(I32), n_used.reshape(1), next_expert, valid_rows, n_blocks,
                 w_exp_gate, w_exp_up, w_exp_down)
    out = _combine(base, wts, pos, y, _tile(n, 128))
    return out.reshape(b, s, d)


def kernel(x, mem, positions, mix_norm, w_in, a_q_gain, a_k_gain, hg_lower_bounds, hg_out_gain, x_q_gain, x_k_gain, mem_norm, w_mem_kv, w_branch_a, w_branch_b, w_branch_c, w_out, ffn_norm, w_router, router_bias, w_exp_gate, w_exp_up, w_exp_down, w_sh_gate, w_sh_up, w_sh_down):
    h = x
    for layer in range(w_in.shape[0]):
        h = _layer(layer, h, mem, positions, mix_norm[layer], w_in[layer], a_q_gain[layer], a_k_gain[layer],
                   hg_lower_bounds, hg_out_gain[layer], x_q_gain[layer], x_k_gain[layer], mem_norm[layer],
                   w_mem_kv[layer], w_branch_a[layer], w_branch_b[layer], w_branch_c[layer], w_out[layer],
                   ffn_norm[layer], w_router[layer], router_bias[layer], w_exp_gate[layer],
                   w_exp_up[layer], w_exp_down[layer], w_sh_gate[layer], w_sh_up[layer], w_sh_down[layer])
    return h
```

```python
import functools
import math

import jax
import jax.numpy as jnp
from jax import lax
from jax.experimental import pallas as pl
from jax.experimental.pallas import tpu as pltpu

F32, BF16, I32 = jnp.float32, jnp.bfloat16, jnp.int32

EPS = 1e-6
ROPE_THETA = 10000.0
HEAD_DIM = 128
DILATIONS = (1, 4, 16)
DIL_BACK = 128
HEADS_PER_GROUP = 4
A_WIDTH = len(DILATIONS) * HEADS_PER_GROUP * HEAD_DIM
A_OUT = HEADS_PER_GROUP * HEAD_DIM
HG_DIM = 128
HG_CHUNK = 64
HG_STRIP = 8
HG_HEADS_PER_STEP = 8
X_HEADS = 4
X_HEAD_DIM = 384
X_WIDTH = X_HEADS * X_HEAD_DIM
N_EXPERTS = 64
TOP_K = 8
ROUTED_SCALE = 2.5
MOE_BLOCK = 512
MOE_SUB_ROWS = 256
PROJ_SUB_ROWS = 256
DMA_THREADS = 2
LANES = 128
ROW_TILE = 16
SUBLANES = 8
VMEM_LIMIT = 56 * 1024 * 1024

NT = (((1,), (1,)), ((), ()))
TN = (((0,), (0,)), ((), ()))


def _params(*sem):
    return pltpu.CompilerParams(dimension_semantics=sem, vmem_limit_bytes=VMEM_LIMIT)


def _rms(x, gain):
    return x * lax.rsqrt(jnp.mean(x * x, axis=-1, keepdims=True) + EPS) * gain


def _silu(x):
    return x * jax.nn.sigmoid(x)


def _bdot(a, b):
    return jnp.dot(a.astype(BF16), b.astype(BF16), preferred_element_type=F32)


def _norm_kernel(x_ref, g_ref, o_ref):
    o_ref[...] = _rms(x_ref[...].astype(F32), g_ref[...]).astype(o_ref.dtype)


def _rms_norm_rows(x, gain, tm):
    m, d = x.shape
    return pl.pallas_call(
        _norm_kernel,
        out_shape=jax.ShapeDtypeStruct((m, d), BF16),
        grid=(m // tm,),
        in_specs=[pl.BlockSpec((tm, d), lambda i: (i, 0)), pl.BlockSpec((1, d), lambda i: (0, 0))],
        out_specs=pl.BlockSpec((tm, d), lambda i: (i, 0)),
        compiler_params=_params("parallel"),
        name="rms_norm_rows",
    )(x, gain.reshape(1, d))


def _rope_table_kernel(pos_ref, invf_ref, cos_ref, sin_ref):
    ang = pos_ref[...].astype(F32) * invf_ref[...]
    lane = lax.broadcasted_iota(I32, ang.shape, 1)
    sin = jnp.sin(ang)
    cos_ref[...] = jnp.cos(ang)
    sin_ref[...] = jnp.where(lane < HEAD_DIM // 2, -sin, sin)


def _rope_tables(positions, tm):
    n = positions.size
    half = HEAD_DIM // 2
    inv_freq = ROPE_THETA ** (-jnp.arange(half, dtype=F32) / half)
    invf = jnp.concatenate([inv_freq, inv_freq]).reshape(1, HEAD_DIM)
    return pl.pallas_call(
        _rope_table_kernel,
        out_shape=[jax.ShapeDtypeStruct((n, HEAD_DIM), F32)] * 2,
        grid=(n // tm,),
        in_specs=[pl.BlockSpec((tm, 1), lambda i: (i, 0)), pl.BlockSpec((1, HEAD_DIM), lambda i: (0, 0))],
        out_specs=[pl.BlockSpec((tm, HEAD_DIM), lambda i: (i, 0))] * 2,
        compiler_params=_params("parallel"),
        name="rope_tables",
    )(positions.reshape(n, 1), invf)


def _proj(a, w, epilogue, *, n0, n, tn, tm, out_dtypes, extras=(), name):
    m, k = a.shape
    assert n0 % tn == 0 and n % tn == 0 and m % tm == 0, (n0, n, tn, m, tm)
    j0 = n0 // tn
    n_extra = len(extras)
    sub = min(tm, PROJ_SUB_ROWS)

    def kern(a_ref, w_ref, *refs):
        for c in range(tm // sub):
            rows = slice(c * sub, (c + 1) * sub)
            acc = jnp.dot(a_ref[rows, :], w_ref[...], preferred_element_type=F32)
            epilogue(acc, rows, refs[:n_extra], refs[n_extra:])

    in_specs = [pl.BlockSpec((tm, k), lambda i, j: (i, 0)),
                pl.BlockSpec((k, tn), lambda i, j: (0, j + j0))]
    in_specs += [pl.BlockSpec(bs, im) for _, bs, im in extras]
    return pl.pallas_call(
        kern,
        out_shape=[jax.ShapeDtypeStruct((m, n), dt) for dt in out_dtypes],
        grid=(m // tm, n // tn),
        in_specs=in_specs,
        out_specs=[pl.BlockSpec((tm, tn), lambda i, j: (i, j)) for _ in out_dtypes],
        compiler_params=_params("parallel", "arbitrary"),
        name=name,
    )(a, w, *[e[0] for e in extras])


def _ep_plain(acc, rows, ins, outs):
    outs[0][rows, :] = acc.astype(outs[0].dtype)


def _ep_silu(acc, rows, ins, outs):
    outs[0][rows, :] = _silu(acc).astype(outs[0].dtype)


def _ep_sigmoid(acc, rows, ins, outs):
    outs[0][rows, :] = jax.nn.sigmoid(acc).astype(outs[0].dtype)


def _head_qk(scale, x, rows, ins):
    y = _rms(x, ins[2][...])
    return (y * ins[0][rows, :] + pltpu.roll(y, HEAD_DIM // 2, axis=1) * ins[1][rows, :]) * scale


def _head_plain(x, rows, ins):
    return x


def _proj_dilated(a, w, head_fn, *, n0, tm, seq, extras=(), name):
    m, k = a.shape
    tn = A_OUT
    assert n0 % tn == 0 and seq % tm == 0 and m % seq == 0, (n0, tn, seq, tm, m)
    j0 = n0 // tn
    tiles = seq // tm
    n_extra = len(extras)
    sub = min(tm, PROJ_SUB_ROWS)
    n_groups = len(DILATIONS)

    def kern(a_ref, w_ref, *refs):
        ins, outs, scr = refs[:n_extra], refs[n_extra:n_extra + n_groups], refs[-1]
        for g, d in enumerate(DILATIONS):
            @pl.when(pl.program_id(1) == g)
            def _(g=g, d=d):
                per = sub // d
                for c in range(tm // sub):
                    rows = slice(c * sub, (c + 1) * sub)
                    acc = jnp.dot(a_ref[rows, :], w_ref[...], preferred_element_type=F32)
                    for h in range(HEADS_PER_GROUP):
                        cols = slice(h * HEAD_DIM, (h + 1) * HEAD_DIM)
                        scr[c, h] = head_fn(acc[:, cols], rows, ins)
                        for r in range(d):
                            outs[g][r, c * per:(c + 1) * per, cols] = (
                                scr[c, h, pl.ds(r, per, stride=d), :].astype(outs[g].dtype))

    in_specs = [pl.BlockSpec((tm, k), lambda i, j: (i, 0)),
                pl.BlockSpec((k, tn), lambda i, j: (0, j + j0))]
    in_specs += [pl.BlockSpec(bs, im) for _, bs, im in extras]
    return pl.pallas_call(
        kern,
        out_shape=[jax.ShapeDtypeStruct((m // seq, d, seq // d, tn), BF16) for d in DILATIONS],
        grid=(m // tm, n_groups),
        in_specs=in_specs,
        out_specs=[pl.BlockSpec((None, d, tm // d, tn), lambda i, j: (i // tiles, 0, i % tiles, 0))
                   for d in DILATIONS],
        scratch_shapes=[pltpu.VMEM((tm // sub, HEADS_PER_GROUP, sub, HEAD_DIM), F32)],
        compiler_params=_params("parallel", "arbitrary"),
        name=name,
    )(a, w, *[e[0] for e in extras])


def _ep_headnorm(width, scale, acc, rows, ins, outs):
    gain = ins[0][...]
    for h in range(acc.shape[1] // width):
        sl = slice(h * width, (h + 1) * width)
        outs[0][rows, sl] = (_rms(acc[:, sl], gain) * scale).astype(outs[0].dtype)


def _ep_log_forget(layer, acc, rows, ins, outs):
    raw = ins[0][...]
    mx = jnp.max(raw, axis=0, keepdims=True)
    ex = jnp.exp(raw - mx)
    lb = jnp.sum(ex[:layer + 1], axis=0, keepdims=True) / jnp.sum(ex, axis=0, keepdims=True)
    outs[0][rows, :] = jnp.log(lb + (1.0 - lb) * jax.nn.sigmoid(acc))


def _dil_attn_kernel(*refs, seq):
    n_groups = len(DILATIONS)
    o_ref, o_scr, l_scr = refs[3 * n_groups:]
    blk = DIL_BACK

    for g, d in enumerate(DILATIONS):
        q_ref, k_ref, v_ref = refs[3 * g:3 * g + 3]
        length = seq // d
        nqb = max(length // blk, 1)
        qrows = min(blk, length)
        width = min(2 * blk, length)

        for r in range(d):
            for qb in range(nqb):
                q0 = qb * qrows
                ks = min(max(q0 + qrows - width, 0), length - width)
                q = q_ref[r, q0:q0 + qrows, :]
                k = k_ref[r, ks:ks + width, :]
                v = v_ref[r, ks:ks + width, :]
                s = lax.dot_general(q, k, NT, preferred_element_type=F32)
                rel = ((q0 - ks) + lax.broadcasted_iota(I32, s.shape, 0)
                       - lax.broadcasted_iota(I32, s.shape, 1))
                s = jnp.where((rel >= 0) & (rel <= DIL_BACK), s, -jnp.inf)
                m = jnp.max(s, axis=-1, keepdims=True)
                p = jnp.exp(s - m)
                den = jnp.sum(p, axis=-1, keepdims=True)
                o = jnp.dot((p / den).astype(BF16), v, preferred_element_type=F32)
                rows = pl.ds(r + d * q0, qrows, stride=d)
                o_scr[g, rows, :] = o
                l_scr[g, rows, :] = jnp.broadcast_to(m + jnp.log(den), o.shape)

    def merge(qi, carry):
        rows = pl.ds(pl.multiple_of(qi * blk, blk), blk)
        lses = [l_scr[g, rows, :] for g in range(n_groups)]
        top = functools.reduce(jnp.maximum, lses)
        ws = [jnp.exp(l - top) for l in lses]
        ya = sum(w * o_scr[g, rows, :] for g, w in enumerate(ws)) / sum(ws)
        o_ref[rows, :] = ya.astype(o_ref.dtype)
        return carry

    lax.fori_loop(0, seq // blk, merge, 0)


def _dilated_attention(qkv, b, s):
    in_specs, args = [], []
    for g, d in enumerate(DILATIONS):
        for t in qkv[g]:
            in_specs.append(pl.BlockSpec((None, d, s // d, HEAD_DIM), lambda bi, h: (bi, 0, 0, h)))
            args.append(t)
    return pl.pallas_call(
        functools.partial(_dil_attn_kernel, seq=s),
        out_shape=jax.ShapeDtypeStruct((b, s, A_OUT), BF16),
        grid=(b, HEADS_PER_GROUP),
        in_specs=in_specs,
        out_specs=pl.BlockSpec((None, s, HEAD_DIM), lambda bi, h: (bi, 0, h)),
        scratch_shapes=[pltpu.VMEM((len(DILATIONS), s, HEAD_DIM), F32)] * 2,
        compiler_params=_params("parallel", "parallel"),
        name="dilated_attention",
    )(*args)


def _hgrn_kernel(q_ref, lf_ref, v_ref, gate_ref, gain_ref, o_ref, st_ref, q_scr, v_scr, l_scr, d_scr,
                 g_scr, k_scr, *, seq, heads):
    C, R, W = HG_CHUNK, HG_STRIP, HG_DIM
    ns = C // R
    st_ref[...] = jnp.zeros_like(st_ref)
    gain = gain_ref[...]
    row_s = lax.broadcasted_iota(I32, (ns, W), 0)
    far_rows = R * ns * (ns - 1) // 2
    rr = lax.broadcasted_iota(I32, (far_rows, C), 0)
    cc = lax.broadcasted_iota(I32, (far_rows, C), 1)
    keep, start = None, 0
    for j in range(ns - 1):
        size = (ns - 1 - j) * R
        blk = (rr >= start) & (rr < start + size) & (cc >= j * R) & (cc < (j + 1) * R)
        keep = blk if keep is None else keep | blk
        start += size

    def chunk(hd, r0):
        cols = slice(hd * W, (hd + 1) * W)
        qf = q_ref[pl.ds(r0, C), cols].astype(F32)
        v = v_ref[pl.ds(r0, C), cols]
        vf = v.astype(F32)
        q_scr[hd] = qf
        v_scr[hd] = vf
        l_scr[hd] = lf_ref[pl.ds(r0, C), cols]
        rows_t = lambda t: pl.ds(t, ns, stride=R)
        lt = [l_scr[hd, rows_t(t), :] for t in range(R)]
        ft = [jnp.exp(l) for l in lt]
        vt_ = [v_scr[hd, rows_t(t), :] for t in range(R)]
        pre = [lt[0]]
        for t in range(1, R):
            pre.append(pre[-1] + lt[t])
        total = pre[R - 1]
        before = total
        step = 1
        while step < ns:
            before = before + jnp.where(row_s >= step, pltpu.roll(before, step, axis=0), 0.0)
            step *= 2
        before = before - total
        kt_ = [1.0 - f for f in ft]
        for t in range(R):
            g_scr[hd, rows_t(t), :] = pre[t] + before
            k_scr[hd, rows_t(t), :] = kt_[t]
        G = g_scr[hd]
        kf = k_scr[hd]
        st = st_ref[hd]
        o = lax.dot_general((qf * jnp.exp(G)).astype(BF16), st.astype(BF16), NT,
                            preferred_element_type=F32)
        decay = []
        for t in range(R):
            decay = [dcy * ft[t] for dcy in decay]
            if t > 0:
                decay.append(ft[t])
            qt_ = q_scr[hd, rows_t(t), :]
            acc = jnp.sum(qt_ * kt_[t], axis=-1, keepdims=True) * vt_[t]
            for s in range(t):
                a = jnp.sum(qt_ * kt_[s] * decay[s], axis=-1, keepdims=True)
                acc = acc + a * vt_[s]
            d_scr[hd, rows_t(t), :] = acc
        o = o + d_scr[hd]
        G3, k3 = G.reshape(ns, R, W), kf.reshape(ns, R, W)
        kt = (k3 * jnp.exp(G3[:, R - 1:R] - G3)).reshape(C, W).astype(BF16)
        qt = jnp.concatenate([qf[j * R:] * jnp.exp(G[j * R:] - G[j * R - 1:j * R]) for j in range(1, ns)],
                             axis=0).astype(BF16)
        a = lax.dot_general(qt, kt, NT, preferred_element_type=F32)
        far = jnp.dot(jnp.where(keep, a, 0.0).astype(BF16), v, preferred_element_type=F32)
        pieces = [o[i * R:(i + 1) * R] for i in range(ns)]
        start = 0
        for j in range(ns - 1):
            for i in range(j + 1, ns):
                pieces[i] = pieces[i] + far[start:start + R]
                start += R
        o = jnp.concatenate(pieces, axis=0)
        y = _rms(o, gain) * _silu(gate_ref[pl.ds(r0, C), cols].astype(F32))
        o_ref[pl.ds(r0, C), cols] = y.astype(o_ref.dtype)
        g_last = G[C - 1:C]
        kd = (kf * jnp.exp(g_last - G)).astype(BF16)
        st_ref[hd] = st * jnp.exp(g_last) + lax.dot_general(v, kd, TN, preferred_element_type=F32)

    def body(c, carry):
        r0 = pl.multiple_of(c * C, C)
        for hd in range(heads):
            chunk(hd, r0)
        return carry

    lax.fori_loop(0, seq // C, body, 0)


def _hgrn2(q, logf, v, gate, out_gain, heads):
    b, s, width = q.shape
    spec = pl.BlockSpec((None, s, heads * HG_DIM), lambda bi, h: (bi, 0, h))
    return pl.pallas_call(
        functools.partial(_hgrn_kernel, seq=s, heads=heads),
        out_shape=jax.ShapeDtypeStruct((b, s, width), BF16),
        grid=(b, width // (heads * HG_DIM)),
        in_specs=[spec, spec, spec, spec, pl.BlockSpec((1, HG_DIM), lambda bi, h: (0, 0))],
        out_specs=spec,
        scratch_shapes=[pltpu.VMEM((heads, HG_DIM, HG_DIM), F32)]
        + [pltpu.VMEM((heads, HG_CHUNK, HG_DIM), F32)] * 6,
        compiler_params=_params("parallel", "parallel"),
        name="hgrn2",
    )(q, logf, v, gate, out_gain.reshape(1, HG_DIM))


def _xattn_kernel(q_ref, k_ref, v_ref, o_ref):
    for h in range(X_HEADS):
        cols = slice(h * X_HEAD_DIM, (h + 1) * X_HEAD_DIM)
        s = lax.dot_general(q_ref[:, cols], k_ref[:, cols], NT, preferred_element_type=F32)
        p = jnp.exp(s - jnp.max(s, axis=-1, keepdims=True))
        p = p / jnp.sum(p, axis=-1, keepdims=True)
        o_ref[:, cols] = jnp.dot(p.astype(BF16), v_ref[:, cols],
                                 preferred_element_type=F32).astype(o_ref.dtype)


def _memory_attention(q, k, v, ts):
    b, s, _ = q.shape
    m = k.shape[1]
    return pl.pallas_call(
        _xattn_kernel,
        out_shape=jax.ShapeDtypeStruct((b, s, X_WIDTH), BF16),
        grid=(b, s // ts),
        in_specs=[pl.BlockSpec((None, ts, X_WIDTH), lambda bi, si: (bi, si, 0)),
                  pl.BlockSpec((None, m, X_WIDTH), lambda bi, si: (bi, 0, 0)),
                  pl.BlockSpec((None, m, X_WIDTH), lambda bi, si: (bi, 0, 0))],
        out_specs=pl.BlockSpec((None, ts, X_WIDTH), lambda bi, si: (bi, si, 0)),
        compiler_params=_params("parallel", "parallel"),
        name="memory_attention",
    )(q, k, v)


def _merge_kernel(ya_ref, yb_ref, yc_ref, wa_ref, wb_ref, wc_ref, ga_ref, gb_ref, gc_ref, o_ref):
    acc = ga_ref[...].astype(F32) * jnp.dot(ya_ref[...], wa_ref[...], preferred_element_type=F32)
    acc += gb_ref[...].astype(F32) * jnp.dot(yb_ref[...], wb_ref[...], preferred_element_type=F32)
    acc += gc_ref[...].astype(F32) * jnp.dot(yc_ref[...], wc_ref[...], preferred_element_type=F32)
    o_ref[...] = acc.astype(o_ref.dtype)


def _merge_branches(ya, yb, yc, wa, wb, wc, gates, tm, tn):
    n, d = yb.shape[0], wa.shape[1]
    nj = d // tn
    row = lambda a: pl.BlockSpec((tm, a.shape[1]), lambda i, j: (i, 0))
    col = lambda w: pl.BlockSpec((w.shape[0], tn), lambda i, j: (0, j))
    gate = lambda br: pl.BlockSpec((tm, tn), lambda i, j: (i, br * nj + j))
    return pl.pallas_call(
        _merge_kernel,
        out_shape=jax.ShapeDtypeStruct((n, d), BF16),
        grid=(n // tm, nj),
        in_specs=[row(ya), row(yb), row(yc), col(wa), col(wb), col(wc), gate(0), gate(1), gate(2)],
        out_specs=pl.BlockSpec((tm, tn), lambda i, j: (i, j)),
        compiler_params=_params("parallel", "arbitrary"),
        name="merge_branches",
    )(ya, yb, yc, wa, wb, wc, gates, gates, gates)


def _store_row_tiles(ref, x, scr):
    m = x.shape[0]
    assert x.shape[1] == ROW_TILE * LANES, x.shape
    for a in range(ROW_TILE):
        scr[a // SUBLANES, pl.ds(a % SUBLANES, m, stride=SUBLANES), :] = x[:, a * LANES:(a + 1) * LANES]
    groups = [scr[t].reshape(m, SUBLANES, LANES) for t in range(ROW_TILE // SUBLANES)]
    ref[...] = jnp.stack(groups, axis=1).reshape(m * ROW_TILE, LANES).astype(ref.dtype)


def _row_tile_scratch(m, *lead):
    return pltpu.VMEM((*lead, ROW_TILE // SUBLANES, m * SUBLANES, LANES), F32)


def _row_tiles_to_scratch(ref, scr):
    m = scr.shape[1] // SUBLANES
    x = ref[...].astype(F32).reshape(m, ROW_TILE // SUBLANES, SUBLANES, LANES)
    for t in range(ROW_TILE // SUBLANES):
        scr[t] = x[:, t].reshape(m * SUBLANES, LANES)


def _scratch_chunk(scr, a):
    m = scr.shape[1] // SUBLANES
    return scr[a // SUBLANES, pl.ds(a % SUBLANES, m, stride=SUBLANES), :]


def _load_row_tiles(ref, scr):
    _row_tiles_to_scratch(ref, scr)
    return jnp.concatenate([_scratch_chunk(scr, a) for a in range(ROW_TILE)], axis=1)


def _swiglu(x, wg_ref, wu_ref, wd_ref):
    x = x.astype(BF16)
    g = jnp.dot(x, wg_ref[...], preferred_element_type=F32)
    u = jnp.dot(x, wu_ref[...], preferred_element_type=F32)
    return _bdot(_silu(g) * u, wd_ref[...])


def _outproj_kernel(m_ref, w_ref, x_ref, g_ref, wr_ref, h_ref, hn_ref, lg_ref, scr):
    tm = m_ref.shape[0]
    sub = scr.shape[2] // SUBLANES
    for c in range(tm // sub):
        rows = slice(c * sub, (c + 1) * sub)
        h = x_ref[rows, :] + jnp.dot(m_ref[rows, :], w_ref[...], preferred_element_type=F32)
        hn = _rms(h, g_ref[...])
        h_ref[rows, :] = h
        _store_row_tiles(hn_ref.at[pl.ds(c * sub * ROW_TILE, sub * ROW_TILE), :], hn, scr.at[c])
        lg_ref[rows, :] = _bdot(hn, wr_ref[...])


def _out_projection(merged, w_out, x, ffn_gain, w_router, tm):
    n, d = x.shape
    e = w_router.shape[1]
    sub = min(tm, PROJ_SUB_ROWS)
    row = pl.BlockSpec((tm, d), lambda i: (i, 0))
    const = lambda a: pl.BlockSpec(a.shape, lambda i: (0, 0))
    g = ffn_gain.reshape(1, d)
    return pl.pallas_call(
        _outproj_kernel,
        out_shape=[jax.ShapeDtypeStruct((n, d), F32), jax.ShapeDtypeStruct((n * ROW_TILE, LANES), BF16),
                   jax.ShapeDtypeStruct((n, e), F32)],
        grid=(n // tm,),
        in_specs=[row, const(w_out), row, const(g), const(w_router)],
        out_specs=[row, pl.BlockSpec((tm * ROW_TILE, LANES), lambda i: (i, 0)),
                   pl.BlockSpec((tm, e), lambda i: (i, 0))],
        scratch_shapes=[_row_tile_scratch(sub, tm // sub)],
        compiler_params=_params("parallel"),
        name="out_projection",
    )(merged, w_out, x, g, w_router)


def _route_kernel(lg_ref, bias_ref, h_ref, hn_ref, sg_ref, su_ref, sd_ref, idx_ref, wt_ref, rank_ref,
                  cnt_ref, base_ref, carry_ref, scr):
    i = pl.program_id(0)

    @pl.when(i == 0)
    def _():
        carry_ref[...] = jnp.zeros_like(carry_ref)

    base_ref[...] = h_ref[...] + _swiglu(_load_row_tiles(hn_ref, scr), sg_ref, su_ref, sd_ref)

    scores = jax.nn.sigmoid(lg_ref[...])
    tm, e = scores.shape
    sel = scores + bias_ref[...]
    lane = lax.broadcasted_iota(I32, (tm, e), 1)
    hots, ws = [], []
    for _ in range(TOP_K):
        mx = jnp.max(sel, axis=-1, keepdims=True)
        ix = jnp.min(jnp.where(sel == mx, lane, e), axis=-1, keepdims=True)
        hot = lane == ix
        hots.append(hot)
        ws.append(jnp.sum(jnp.where(hot, scores, 0.0), axis=-1, keepdims=True))
        sel = jnp.where(hot, -jnp.inf, sel)
    chosen = functools.reduce(jnp.logical_or, hots).astype(F32)
    r = lax.broadcasted_iota(I32, (tm, tm), 0)
    c = lax.broadcasted_iota(I32, (tm, tm), 1)
    before = (c < r).astype(BF16)
    ahead = jnp.dot(before, chosen.astype(BF16), preferred_element_type=F32) + carry_ref[...]
    carry_ref[...] += jnp.sum(chosen, axis=0, keepdims=True)
    cnt_ref[...] = carry_ref[...].astype(I32)

    norm = ROUTED_SCALE / sum(ws)
    out_lane = lax.broadcasted_iota(I32, (tm, LANES), 1)
    idx_o = jnp.zeros((tm, LANES), I32)
    wt_o = jnp.zeros((tm, LANES), F32)
    rk_o = jnp.zeros((tm, LANES), I32)
    for j in range(TOP_K):
        ix = jnp.sum(jnp.where(hots[j], lane, 0), axis=-1, keepdims=True)
        rk = jnp.sum(jnp.where(hots[j], ahead, 0.0), axis=-1, keepdims=True).astype(I32)
        idx_o = jnp.where(out_lane == j, ix, idx_o)
        wt_o = jnp.where(out_lane == j, ws[j] * norm, wt_o)
        rk_o = jnp.where(out_lane == j, rk, rk_o)
    idx_ref[...] = idx_o
    wt_ref[...] = wt_o
    rank_ref[...] = rk_o


def _route(logits, bias, h, hn, sg, su, sd, tm):
    n, e = logits.shape
    d = h.shape[1]
    row = pl.BlockSpec((tm, LANES), lambda i: (i, 0))
    wide = lambda a: pl.BlockSpec((a.shape[0] // (n // tm), a.shape[1]), lambda i: (i, 0))
    const = lambda a: pl.BlockSpec(a.shape, lambda i: (0, 0))
    bias = bias.reshape(1, e)
    return pl.pallas_call(
        _route_kernel,
        out_shape=[jax.ShapeDtypeStruct((n, LANES), I32), jax.ShapeDtypeStruct((n, LANES), F32),
                   jax.ShapeDtypeStruct((n, LANES), I32), jax.ShapeDtypeStruct((1, e), I32),
                   jax.ShapeDtypeStruct((n, d), F32)],
        grid=(n // tm,),
        in_specs=[wide(logits), const(bias), wide(h), wide(hn), const(sg), const(su), const(sd)],
        out_specs=[row, row, row, pl.BlockSpec((1, e), lambda i: (0, 0)), wide(h)],
        scratch_shapes=[pltpu.VMEM((1, e), F32), _row_tile_scratch(tm)],
        compiler_params=_params("arbitrary"),
        name="route",
    )(logits, bias, h, hn, sg, su, sd)


def _dispatch_kernel(tail_ref, pos_ref, hn_ref, xs_ref, zeros, sem, zsem):
    tm = hn_ref.shape[0] // ROW_TILE
    half = zeros.shape[0]

    def tile_rows(start, size=ROW_TILE):
        return pl.ds(pl.multiple_of(start, ROW_TILE), size)

    @pl.when(pl.program_id(0) == 0)
    def _():
        zeros[...] = jnp.zeros_like(zeros)

        def tail_copies(e):
            off, pad = tail_ref[e], tail_ref[N_EXPERTS + e]
            size = half
            while size >= ROW_TILE:
                yield (pad & size) != 0, pltpu.make_async_copy(
                    zeros.at[pl.ds(0, size), :], xs_ref.at[tile_rows(off, size), :], zsem)
                off = off + (pad & size)
                size //= 2

        def start(e, carry):
            for needed, copy in tail_copies(e):
                pl.when(needed)(copy.start)
            return carry

        def wait(e, carry):
            for needed, copy in tail_copies(e):
                pl.when(needed)(copy.wait)
            return carry

        lax.fori_loop(0, N_EXPERTS, start, 0)
        lax.fori_loop(0, N_EXPERTS, wait, 0)

        def slack_copy(k):
            return pltpu.make_async_copy(
                zeros, xs_ref.at[tile_rows(tail_ref[2 * N_EXPERTS] + k * half, half), :], zsem)

        n_slack = (xs_ref.shape[0] - tail_ref[2 * N_EXPERTS]) // half
        lax.fori_loop(0, n_slack, lambda k, c: (slack_copy(k).start(), c)[1], 0)
        lax.fori_loop(0, n_slack, lambda k, c: (slack_copy(k).wait(), c)[1], 0)

    def row_copy(r, dst_row):
        return pltpu.make_async_copy(hn_ref.at[tile_rows(r * ROW_TILE), :], xs_ref.at[tile_rows(dst_row), :], sem)

    def issue(r, carry):
        for j in range(TOP_K):
            row_copy(r, pos_ref[r * TOP_K + j]).start(priority=j % DMA_THREADS)
        return carry

    lax.fori_loop(0, tm, issue, 0)
    for j in range(TOP_K):
        pltpu.make_async_copy(hn_ref, xs_ref.at[pl.ds(0, tm * ROW_TILE), :], sem).wait()


def _dispatch(hn, pos_rows, tails, n_blocks, tm):
    n = hn.shape[0] // ROW_TILE
    blk = MOE_BLOCK
    return pl.pallas_call(
        _dispatch_kernel,
        out_shape=jax.ShapeDtypeStruct((n_blocks * blk * ROW_TILE, LANES), hn.dtype),
        grid_spec=pltpu.PrefetchScalarGridSpec(
            num_scalar_prefetch=1,
            grid=(n // tm,),
            in_specs=[pl.BlockSpec((tm * TOP_K,), lambda i, tl: (i,), memory_space=pltpu.SMEM),
                      pl.BlockSpec((tm * ROW_TILE, LANES), lambda i, tl: (i, 0))],
            out_specs=pl.BlockSpec(memory_space=pl.ANY),
            scratch_shapes=[pltpu.VMEM((blk // 2 * ROW_TILE, LANES), hn.dtype), pltpu.SemaphoreType.DMA,
                            pltpu.SemaphoreType.DMA]),
        compiler_params=pltpu.CompilerParams(dimension_semantics=("arbitrary",),
                                             vmem_limit_bytes=VMEM_LIMIT, has_side_effects=True),
        name="moe_dispatch",
    )(tails, pos_rows, hn)


def _expert_kernel(be_ref, nu_ref, nxt_ref, valid_ref, x_ref, wg_ref, wu_ref, wd_ref, y_ref, wg_f, wu_f, wd_f,
                   wg_b, wu_b, wd_b, x_scr, y_scr, sem):
    b = pl.program_id(0)

    def fetch(e):
        return [pltpu.make_async_copy(src.at[e], dst, sem)
                for src, dst in ((wg_ref, wg_f), (wu_ref, wu_f), (wd_ref, wd_f))]

    @pl.when(b >= nu_ref[0])
    def _():
        y_ref[...] = jnp.zeros_like(y_ref)

    @pl.when(b < nu_ref[0])
    def _():
        e = be_ref[b]

        @pl.when(b == 0)
        def _():
            for copy in fetch(e):
                copy.start()

        @pl.when((b == 0) | (e != be_ref[jnp.maximum(b - 1, 0)]))
        def _():
            for copy in fetch(e):
                copy.wait()
            wg_b[...] = wg_f[...].astype(BF16)
            wu_b[...] = wu_f[...].astype(BF16)
            wd_b[...] = wd_f[...].astype(BF16)

            @pl.when(nxt_ref[e] >= 0)
            def _():
                for copy in fetch(nxt_ref[e]):
                    copy.start()

        sub = x_scr.shape[2] // SUBLANES
        n_sub = x_ref.shape[0] // (sub * ROW_TILE)

        def run(live):
            for c in range(n_sub):
                rows = pl.ds(c * sub * ROW_TILE, sub * ROW_TILE)
                if c < live:
                    x = _load_row_tiles(x_ref.at[rows, :], x_scr.at[c])
                    _store_row_tiles(y_ref.at[rows, :], _swiglu(x, wg_b, wu_b, wd_b), y_scr.at[c])
                else:
                    y_ref[rows, :] = jnp.zeros((sub * ROW_TILE, LANES), y_ref.dtype)

        live = (valid_ref[b] + sub - 1) // sub
        for k in range(1, n_sub + 1):
            pl.when(live == k)(functools.partial(run, k))


def _experts(xs, block_expert, n_used, next_expert, valid_rows, n_blocks, wg, wu, wd):
    d, ff = wg.shape[1], wg.shape[2]
    rows = MOE_BLOCK * ROW_TILE
    sub = min(MOE_BLOCK, MOE_SUB_ROWS)
    relayout = _row_tile_scratch(sub, MOE_BLOCK // sub)
    used = lambda b, be, nu, nxt, valid: (jnp.minimum(b, nu[0] - 1), 0)
    hbm = pl.BlockSpec(memory_space=pl.ANY)
    return pl.pallas_call(
        _expert_kernel,
        out_shape=jax.ShapeDtypeStruct(xs.shape, xs.dtype),
        grid_spec=pltpu.PrefetchScalarGridSpec(
            num_scalar_prefetch=4,
            grid=(n_blocks,),
            in_specs=[pl.BlockSpec((rows, LANES), used), hbm, hbm, hbm],
            out_specs=pl.BlockSpec((rows, LANES), lambda b, be, nu, nxt, valid: (b, 0)),
            scratch_shapes=[pltpu.VMEM((d, ff), F32), pltpu.VMEM((d, ff), F32), pltpu.VMEM((ff, d), F32),
                            pltpu.VMEM((d, ff), BF16), pltpu.VMEM((d, ff), BF16), pltpu.VMEM((ff, d), BF16),
                            relayout, relayout, pltpu.SemaphoreType.DMA]),
        compiler_params=_params("arbitrary"),
        name="moe_experts",
    )(block_expert, n_used, next_expert, valid_rows, xs, wg, wu, wd)


def _combine_kernel(pos_ref, pos_next_ref, base_ref, wt_ref, y_ref, o_ref, buf, scr, sem):
    i = pl.program_id(0)
    tm = base_ref.shape[0]
    cur = i % 2

    def tile_rows(start):
        return pl.ds(pl.multiple_of(start, ROW_TILE), ROW_TILE)

    def row_copy(half, r, j, src_row):
        return pltpu.make_async_copy(y_ref.at[tile_rows(src_row), :],
                                     buf.at[half, j, tile_rows(r * ROW_TILE), :], sem.at[half])

    def request(slots_ref, half):
        def issue(r, carry):
            for j in range(TOP_K):
                row_copy(half, r, j, slots_ref[r * TOP_K + j]).start(priority=j % DMA_THREADS)
            return carry

        lax.fori_loop(0, tm, issue, 0)

    @pl.when(i == 0)
    def _():
        request(pos_ref, 0)

    @pl.when(i + 1 < pl.num_programs(0))
    def _():
        request(pos_next_ref, 1 - cur)

    for j in range(TOP_K):
        pltpu.make_async_copy(y_ref.at[pl.ds(0, tm * ROW_TILE), :], buf.at[cur, j], sem.at[cur]).wait()

    for j in range(TOP_K):
        _row_tiles_to_scratch(buf.at[cur, j], scr.at[j])
    wt = wt_ref[...]
    for a in range(ROW_TILE):
        cols = slice(a * LANES, (a + 1) * LANES)
        acc = base_ref[:, cols]
        for j in range(TOP_K):
            acc = acc + wt[:, j:j + 1] * _scratch_chunk(scr.at[j], a)
        o_ref[:, cols] = acc


def _combine(base, wts, pos_flat, y, tm):
    n, d = base.shape
    steps = n // tm
    row = pl.BlockSpec((tm, d), lambda i: (i, 0))
    slots = lambda im: pl.BlockSpec((tm * TOP_K,), im, memory_space=pltpu.SMEM)
    return pl.pallas_call(
        _combine_kernel,
        out_shape=jax.ShapeDtypeStruct((n, d), F32),
        grid=(steps,),
        in_specs=[slots(lambda i: (i,)), slots(lambda i: (jnp.minimum(i + 1, steps - 1),)),
                  row, pl.BlockSpec((tm, LANES), lambda i: (i, 0)), pl.BlockSpec(memory_space=pl.ANY)],
        out_specs=row,
        scratch_shapes=[pltpu.VMEM((2, TOP_K, tm * ROW_TILE, LANES), y.dtype), _row_tile_scratch(tm, TOP_K),
                        pltpu.SemaphoreType.DMA((2,))],
        compiler_params=_params("arbitrary"),
        name="moe_combine",
    )(pos_flat, pos_flat, base, wts, y)


def _tile(n, want):
    t = min(n, want)
    assert n % t == 0, (n, want)
    return t


def _layer(layer, h, mem, positions, mix_norm, w_in, a_q_gain, a_k_gain, lower_raw, hg_out_gain, x_q_gain,
           x_k_gain, mem_norm, w_mem_kv, w_branch_a, w_branch_b, w_branch_c, w_out, ffn_norm, w_router,
           router_bias, w_exp_gate, w_exp_up, w_exp_down, w_sh_gate, w_sh_up, w_sh_down):
    b, s, d = h.shape
    n = b * s
    x2 = h.reshape(n, d)
    tm = _tile(n, 1024)
    bf = lambda w: w.astype(BF16)

    xn = _rms_norm_rows(x2, mix_norm, _tile(n, 512))
    cos, sin = _rope_tables(positions, _tile(n, 1024))
    w_in = bf(w_in)
    tmd = _tile(s, tm)
    rope_in = lambda gain: ((cos, (tmd, HEAD_DIM), lambda i, j: (i, 0)),
                            (sin, (tmd, HEAD_DIM), lambda i, j: (i, 0)),
                            (gain.reshape(1, HEAD_DIM), (1, HEAD_DIM), lambda i, j: (0, 0)))
    proj = functools.partial(_proj, xn, w_in, tm=tm)
    dilated = functools.partial(_proj_dilated, xn, w_in, tm=tmd, seq=s)
    hg_w = lower_raw.shape[1]
    off = 0
    aq = dilated(functools.partial(_head_qk, HEAD_DIM ** -0.5), n0=off, extras=rope_in(a_q_gain),
                 name="proj_aq")
    off += A_WIDTH
    ak = dilated(functools.partial(_head_qk, 1.0), n0=off, extras=rope_in(a_k_gain), name="proj_ak")
    off += A_WIDTH
    av = dilated(_head_plain, n0=off, name="proj_av")
    off += A_WIDTH
    hq, = proj(_ep_silu, n0=off, n=hg_w, tn=512, out_dtypes=[BF16], name="proj_hq")
    off += hg_w
    logf, = proj(functools.partial(_ep_log_forget, layer), n0=off, n=hg_w, tn=512, out_dtypes=[F32],
                 extras=((lower_raw, (lower_raw.shape[0], 512), lambda i, j: (0, j)),), name="proj_hf")
    off += hg_w
    hiv, = proj(_ep_plain, n0=off, n=hg_w, tn=512, out_dtypes=[BF16], name="proj_hi")
    off += hg_w
    hgate, = proj(_ep_plain, n0=off, n=hg_w, tn=512, out_dtypes=[BF16], name="proj_hgate")
    off += hg_w
    xq, = _proj(xn, w_in[:, off:off + X_WIDTH], functools.partial(_ep_headnorm, X_HEAD_DIM, X_HEAD_DIM ** -0.5),
                n0=0, n=X_WIDTH, tn=2 * X_HEAD_DIM, tm=tm, out_dtypes=[BF16],
                extras=((x_q_gain.reshape(1, X_HEAD_DIM), (1, X_HEAD_DIM), lambda i, j: (0, 0)),),
                name="proj_xq")
    off += X_WIDTH
    gates, = proj(_ep_sigmoid, n0=off, n=3 * d, tn=1024, out_dtypes=[BF16], name="proj_gates")

    r3 = lambda t: t.reshape(b, s, t.shape[1])
    ya = _dilated_attention(list(zip(aq, ak, av)), b, s).reshape(n, A_OUT)
    yb = _hgrn2(r3(hq), r3(logf), r3(hiv), r3(hgate), hg_out_gain, HG_HEADS_PER_STEP).reshape(n, hg_w)
    nm = mem.shape[0] * mem.shape[1]
    mem_n = _rms_norm_rows(mem.reshape(nm, d), mem_norm, _tile(nm, 512))
    w_kv = bf(w_mem_kv)
    tmm = _tile(nm, 1024)
    kn, = _proj(mem_n, w_kv, functools.partial(_ep_headnorm, X_HEAD_DIM, 1.0), n0=0, n=X_WIDTH,
                tn=2 * X_HEAD_DIM, tm=tmm, out_dtypes=[BF16],
                extras=((x_k_gain.reshape(1, X_HEAD_DIM), (1, X_HEAD_DIM), lambda i, j: (0, 0)),),
                name="proj_mem_k")
    vm, = _proj(mem_n, w_kv, _ep_plain, n0=X_WIDTH, n=X_WIDTH, tn=2 * X_HEAD_DIM, tm=tmm,
                out_dtypes=[BF16], name="proj_mem_v")
    rm = lambda t: t.reshape(b, mem.shape[1], X_WIDTH)
    yc = _memory_attention(r3(xq), rm(kn), rm(vm), _tile(s, 512)).reshape(n, X_WIDTH)

    merged = _merge_branches(ya, yb, yc, bf(w_branch_a), bf(w_branch_b), bf(w_branch_c), gates, tm, 512)
    h1, hn, logits = _out_projection(merged, bf(w_out), x2, ffn_norm, bf(w_router), _tile(n, 512))

    tr = _tile(n, 256)
    idx, wts, rank, counts, base = _route(logits, router_bias, h1, hn, bf(w_sh_gate), bf(w_sh_up),
                                          bf(w_sh_down), _tile(n, 512))
    blk = MOE_BLOCK
    counts = counts[0]
    padded = (counts + blk - 1) // blk * blk
    ends = jnp.cumsum(padded)
    starts = ends - padded
    hit = idx[:, :TOP_K, None] == jnp.arange(N_EXPERTS, dtype=I32)
    pos = (jnp.sum(jnp.where(hit, starts, 0), axis=-1) + rank[:, :TOP_K]).reshape(-1).astype(I32)
    pos = pos * ROW_TILE
    n_blocks = -(-(n * TOP_K + N_EXPERTS * (blk - 1)) // blk)
    n_used = (ends[-1] // blk).astype(I32)
    blocks = jnp.minimum(jnp.arange(n_blocks, dtype=I32), n_used - 1)
    block_expert = jnp.minimum(jnp.sum(ends[None, :] <= blocks[:, None] * blk, axis=1), N_EXPERTS - 1)
    tails = (jnp.concatenate([starts + counts, padded - counts, ends[-1:]]) * ROW_TILE).astype(I32)
    xs = _dispatch(hn, pos, tails, n_blocks, tr)
    later = lax.cummin(jnp.where(counts > 0, jnp.arange(N_EXPERTS, dtype=I32), N_EXPERTS), reverse=True)
    later = jnp.concatenate([later[1:], jnp.full((1,), N_EXPERTS, I32)])
    next_expert = jnp.where(later < N_EXPERTS, later, -1).astype(I32)
    mine = block_expert[:, None] == jnp.arange(N_EXPERTS, dtype=I32)
    real_end = jnp.sum(jnp.where(mine, starts + counts, 0), axis=1)
    valid_rows = jnp.clip(real_end - blocks * blk, 0, blk).astype(I32)
    y = _experts(xs, block_expert.astype(I32), n_used.reshape(1), next_expert, valid_rows, n_blocks,
                 w_exp_gate, w_exp_up, w_exp_down)
    out = _combine(base, wts, pos, y, _tile(n, 128))
    return out.reshape(b, s, d)


def kernel(x, mem, positions, mix_norm, w_in, a_q_gain, a_k_gain, hg_lower_bounds, hg_out_gain, x_q_gain, x_k_gain, mem_norm, w_mem_kv, w_branch_a, w_branch_b, w_branch_c, w_out, ffn_norm, w_router, router_bias, w_exp_gate, w_exp_up, w_exp_down, w_sh_gate, w_sh_up, w_sh_down):
    h = x
    for layer in range(w_in.shape[0]):
        h = _layer(layer, h, mem, positions, mix_norm[layer], w_in[layer], a_q_gain[layer], a_k_gain[layer],
                   hg_lower_bounds, hg_out_gain[layer], x_q_gain[layer], x_k_gain[layer], mem_norm[layer],
                   w_mem_kv[layer], w_branch_a[layer], w_branch_b[layer], w_branch_c[layer], w_out[layer],
                   ffn_norm[layer], w_router[layer], router_bias[layer], w_exp_gate[layer],
                   w_exp_up[layer], w_exp_down[layer], w_sh_gate[layer], w_sh_up[layer], w_sh_down[layer])
    return h
```

```python
import functools
import math

import jax
import jax.numpy as jnp
from jax import lax
from jax.experimental import pallas as pl
from jax.experimental.pallas import tpu as pltpu

F32, BF16, I32 = jnp.float32, jnp.bfloat16, jnp.int32

EPS = 1e-6
ROPE_THETA = 10000.0
HEAD_DIM = 128
DILATIONS = (1, 4, 16)
DIL_BACK = 128
HEADS_PER_GROUP = 4
A_WIDTH = len(DILATIONS) * HEADS_PER_GROUP * HEAD_DIM
A_OUT = HEADS_PER_GROUP * HEAD_DIM
HG_DIM = 128
HG_CHUNK = 64
HG_STRIP = 8
HG_HEADS_PER_STEP = 8
X_HEADS = 4
X_HEAD_DIM = 384
X_WIDTH = X_HEADS * X_HEAD_DIM
N_EXPERTS = 64
TOP_K = 8
ROUTED_SCALE = 2.5
MOE_BLOCK = 512
MOE_SUB_ROWS = 256
PROJ_SUB_ROWS = 256
DMA_THREADS = 2
LANES = 128
ROW_TILE = 16
SUBLANES = 8
VMEM_LIMIT = 56 * 1024 * 1024

NT = (((1,), (1,)), ((), ()))
TN = (((0,), (0,)), ((), ()))


def _params(*sem):
    return pltpu.CompilerParams(dimension_semantics=sem, vmem_limit_bytes=VMEM_LIMIT)


def _rms(x, gain):
    return x * lax.rsqrt(jnp.mean(x * x, axis=-1, keepdims=True) + EPS) * gain


def _silu(x):
    return x * jax.nn.sigmoid(x)


def _bdot(a, b):
    return jnp.dot(a.astype(BF16), b.astype(BF16), preferred_element_type=F32)


def _norm_kernel(x_ref, g_ref, o_ref):
    o_ref[...] = _rms(x_ref[...].astype(F32), g_ref[...]).astype(o_ref.dtype)


def _rms_norm_rows(x, gain, tm):
    m, d = x.shape
    return pl.pallas_call(
        _norm_kernel,
        out_shape=jax.ShapeDtypeStruct((m, d), BF16),
        grid=(m // tm,),
        in_specs=[pl.BlockSpec((tm, d), lambda i: (i, 0)), pl.BlockSpec((1, d), lambda i: (0, 0))],
        out_specs=pl.BlockSpec((tm, d), lambda i: (i, 0)),
        compiler_params=_params("parallel"),
        name="rms_norm_rows",
    )(x, gain.reshape(1, d))


def _rope_table_kernel(pos_ref, invf_ref, cos_ref, sin_ref):
    ang = pos_ref[...].astype(F32) * invf_ref[...]
    lane = lax.broadcasted_iota(I32, ang.shape, 1)
    sin = jnp.sin(ang)
    cos_ref[...] = jnp.cos(ang)
    sin_ref[...] = jnp.where(lane < HEAD_DIM // 2, -sin, sin)


def _norm_rope_kernel(x_ref, g_ref, pos_ref, invf_ref, o_ref, cos_ref, sin_ref):
    _norm_kernel(x_ref, g_ref, o_ref)
    _rope_table_kernel(pos_ref, invf_ref, cos_ref, sin_ref)


def _rms_norm_rows_with_rope(x, gain, positions, tm):
    n, d = x.shape
    half = HEAD_DIM // 2
    inv_freq = ROPE_THETA ** (-jnp.arange(half, dtype=F32) / half)
    invf = jnp.concatenate([inv_freq, inv_freq]).reshape(1, HEAD_DIM)
    table = pl.BlockSpec((tm, HEAD_DIM), lambda i: (i, 0))
    return pl.pallas_call(
        _norm_rope_kernel,
        out_shape=[jax.ShapeDtypeStruct((n, d), BF16)] + [jax.ShapeDtypeStruct((n, HEAD_DIM), F32)] * 2,
        grid=(n // tm,),
        in_specs=[pl.BlockSpec((tm, d), lambda i: (i, 0)), pl.BlockSpec((1, d), lambda i: (0, 0)),
                  pl.BlockSpec((tm, 1), lambda i: (i, 0)), pl.BlockSpec((1, HEAD_DIM), lambda i: (0, 0))],
        out_specs=[pl.BlockSpec((tm, d), lambda i: (i, 0)), table, table],
        compiler_params=_params("parallel"),
        name="rms_norm_rope",
    )(x, gain.reshape(1, d), positions.reshape(n, 1), invf)


def _proj(a, w, epilogue, *, n0, n, tn, tm, out_dtypes, extras=(), name):
    m, k = a.shape
    assert n0 % tn == 0 and n % tn == 0 and m % tm == 0, (n0, n, tn, m, tm)
    j0 = n0 // tn
    n_extra = len(extras)
    sub = min(tm, PROJ_SUB_ROWS)

    def kern(a_ref, w_ref, *refs):
        for c in range(tm // sub):
            rows = slice(c * sub, (c + 1) * sub)
            acc = jnp.dot(a_ref[rows, :], w_ref[...], preferred_element_type=F32)
            epilogue(acc, rows, refs[:n_extra], refs[n_extra:])

    in_specs = [pl.BlockSpec((tm, k), lambda i, j: (i, 0)),
                pl.BlockSpec((k, tn), lambda i, j: (0, j + j0))]
    in_specs += [pl.BlockSpec(bs, im) for _, bs, im in extras]
    return pl.pallas_call(
        kern,
        out_shape=[jax.ShapeDtypeStruct((m, n), dt) for dt in out_dtypes],
        grid=(m // tm, n // tn),
        in_specs=in_specs,
        out_specs=[pl.BlockSpec((tm, tn), lambda i, j: (i, j)) for _ in out_dtypes],
        compiler_params=_params("parallel", "arbitrary"),
        name=name,
    )(a, w, *[e[0] for e in extras])


def _ep_plain(acc, rows, ins, outs):
    outs[0][rows, :] = acc.astype(outs[0].dtype)


def _ep_silu(acc, rows, ins, outs):
    outs[0][rows, :] = _silu(acc).astype(outs[0].dtype)


def _ep_sigmoid(acc, rows, ins, outs):
    outs[0][rows, :] = jax.nn.sigmoid(acc).astype(outs[0].dtype)


def _head_qk(scale, x, rows, ins):
    y = _rms(x, ins[2][...])
    return (y * ins[0][rows, :] + pltpu.roll(y, HEAD_DIM // 2, axis=1) * ins[1][rows, :]) * scale


def _head_plain(x, rows, ins):
    return x


def _proj_dilated(a, w, head_fn, *, n0, tm, seq, extras=(), name):
    m, k = a.shape
    tn = A_OUT
    assert n0 % tn == 0 and seq % tm == 0 and m % seq == 0, (n0, tn, seq, tm, m)
    j0 = n0 // tn
    tiles = seq // tm
    n_extra = len(extras)
    sub = min(tm, PROJ_SUB_ROWS)
    n_groups = len(DILATIONS)

    def kern(a_ref, w_ref, *refs):
        ins, outs, scr = refs[:n_extra], refs[n_extra:n_extra + n_groups], refs[-1]
        for g, d in enumerate(DILATIONS):
            @pl.when(pl.program_id(1) == g)
            def _(g=g, d=d):
                per = sub // d
                for c in range(tm // sub):
                    rows = slice(c * sub, (c + 1) * sub)
                    acc = jnp.dot(a_ref[rows, :], w_ref[...], preferred_element_type=F32)
                    for h in range(HEADS_PER_GROUP):
                        cols = slice(h * HEAD_DIM, (h + 1) * HEAD_DIM)
                        scr[c, h] = head_fn(acc[:, cols], rows, ins)
                        for r in range(d):
                            outs[g][r, c * per:(c + 1) * per, cols] = (
                                scr[c, h, pl.ds(r, per, stride=d), :].astype(outs[g].dtype))

    in_specs = [pl.BlockSpec((tm, k), lambda i, j: (i, 0)),
                pl.BlockSpec((k, tn), lambda i, j: (0, j + j0))]
    in_specs += [pl.BlockSpec(bs, im) for _, bs, im in extras]
    return pl.pallas_call(
        kern,
        out_shape=[jax.ShapeDtypeStruct((m // seq, d, seq // d, tn), BF16) for d in DILATIONS],
        grid=(m // tm, n_groups),
        in_specs=in_specs,
        out_specs=[pl.BlockSpec((None, d, tm // d, tn), lambda i, j: (i // tiles, 0, i % tiles, 0))
                   for d in DILATIONS],
        scratch_shapes=[pltpu.VMEM((tm // sub, HEADS_PER_GROUP, sub, HEAD_DIM), F32)],
        compiler_params=_params("parallel", "arbitrary"),
        name=name,
    )(a, w, *[e[0] for e in extras])


def _ep_headnorm(width, scale, acc, rows, ins, outs):
    gain = ins[0][...]
    for h in range(acc.shape[1] // width):
        sl = slice(h * width, (h + 1) * width)
        outs[0][rows, sl] = (_rms(acc[:, sl], gain) * scale).astype(outs[0].dtype)


def _ep_log_forget(layer, acc, rows, ins, outs):
    raw = ins[0][...]
    mx = jnp.max(raw, axis=0, keepdims=True)
    ex = jnp.exp(raw - mx)
    lb = jnp.sum(ex[:layer + 1], axis=0, keepdims=True) / jnp.sum(ex, axis=0, keepdims=True)
    outs[0][rows, :] = jnp.log(lb + (1.0 - lb) * jax.nn.sigmoid(acc))


def _dil_attn_kernel(*refs, seq):
    n_groups = len(DILATIONS)
    o_ref, o_scr, l_scr = refs[3 * n_groups:]
    blk = DIL_BACK

    for g, d in enumerate(DILATIONS):
        q_ref, k_ref, v_ref = refs[3 * g:3 * g + 3]
        length = seq // d
        nqb = max(length // blk, 1)
        qrows = min(blk, length)
        width = min(2 * blk, length)

        for r in range(d):
            for qb in range(nqb):
                q0 = qb * qrows
                ks = min(max(q0 + qrows - width, 0), length - width)
                q = q_ref[r, q0:q0 + qrows, :]
                k = k_ref[r, ks:ks + width, :]
                v = v_ref[r, ks:ks + width, :]
                s = lax.dot_general(q, k, NT, preferred_element_type=F32)
                rel = ((q0 - ks) + lax.broadcasted_iota(I32, s.shape, 0)
                       - lax.broadcasted_iota(I32, s.shape, 1))
                s = jnp.where((rel >= 0) & (rel <= DIL_BACK), s, -jnp.inf)
                m = jnp.max(s, axis=-1, keepdims=True)
                p = jnp.exp(s - m)
                den = jnp.sum(p, axis=-1, keepdims=True)
                o = jnp.dot((p / den).astype(BF16), v, preferred_element_type=F32)
                rows = pl.ds(r + d * q0, qrows, stride=d)
                o_scr[g, rows, :] = o
                l_scr[g, rows, :] = jnp.broadcast_to(m + jnp.log(den), o.shape)

    def merge(qi, carry):
        rows = pl.ds(pl.multiple_of(qi * blk, blk), blk)
        lses = [l_scr[g, rows, :] for g in range(n_groups)]
        top = functools.reduce(jnp.maximum, lses)
        ws = [jnp.exp(l - top) for l in lses]
        ya = sum(w * o_scr[g, rows, :] for g, w in enumerate(ws)) / sum(ws)
        o_ref[rows, :] = ya.astype(o_ref.dtype)
        return carry

    lax.fori_loop(0, seq // blk, merge, 0)


def _dilated_attention(qkv, b, s):
    in_specs, args = [], []
    for g, d in enumerate(DILATIONS):
        for t in qkv[g]:
            in_specs.append(pl.BlockSpec((None, d, s // d, HEAD_DIM), lambda bi, h: (bi, 0, 0, h)))
            args.append(t)
    return pl.pallas_call(
        functools.partial(_dil_attn_kernel, seq=s),
        out_shape=jax.ShapeDtypeStruct((b, s, A_OUT), BF16),
        grid=(b, HEADS_PER_GROUP),
        in_specs=in_specs,
        out_specs=pl.BlockSpec((None, s, HEAD_DIM), lambda bi, h: (bi, 0, h)),
        scratch_shapes=[pltpu.VMEM((len(DILATIONS), s, HEAD_DIM), F32)] * 2,
        compiler_params=_params("parallel", "parallel"),
        name="dilated_attention",
    )(*args)


def _hgrn_kernel(q_ref, lf_ref, v_ref, gate_ref, gain_ref, o_ref, st_ref, q_scr, v_scr, l_scr, d_scr,
                 g_scr, k_scr, *, seq, heads):
    C, R, W = HG_CHUNK, HG_STRIP, HG_DIM
    ns = C // R
    st_ref[...] = jnp.zeros_like(st_ref)
    gain = gain_ref[...]
    row_s = lax.broadcasted_iota(I32, (ns, W), 0)
    far_rows = R * ns * (ns - 1) // 2
    rr = lax.broadcasted_iota(I32, (far_rows, C), 0)
    cc = lax.broadcasted_iota(I32, (far_rows, C), 1)
    keep, start = None, 0
    for j in range(ns - 1):
        size = (ns - 1 - j) * R
        blk = (rr >= start) & (rr < start + size) & (cc >= j * R) & (cc < (j + 1) * R)
        keep = blk if keep is None else keep | blk
        start += size

    def chunk(hd, r0):
        cols = slice(hd * W, (hd + 1) * W)
        qf = q_ref[pl.ds(r0, C), cols].astype(F32)
        v = v_ref[pl.ds(r0, C), cols]
        vf = v.astype(F32)
        q_scr[hd] = qf
        v_scr[hd] = vf
        l_scr[hd] = lf_ref[pl.ds(r0, C), cols]
        rows_t = lambda t: pl.ds(t, ns, stride=R)
        lt = [l_scr[hd, rows_t(t), :] for t in range(R)]
        ft = [jnp.exp(l) for l in lt]
        vt_ = [v_scr[hd, rows_t(t), :] for t in range(R)]
        pre = [lt[0]]
        for t in range(1, R):
            pre.append(pre[-1] + lt[t])
        total = pre[R - 1]
        before = total
        step = 1
        while step < ns:
            before = before + jnp.where(row_s >= step, pltpu.roll(before, step, axis=0), 0.0)
            step *= 2
        before = before - total
        kt_ = [1.0 - f for f in ft]
        for t in range(R):
            g_scr[hd, rows_t(t), :] = pre[t] + before
            k_scr[hd, rows_t(t), :] = kt_[t]
        G = g_scr[hd]
        kf = k_scr[hd]
        st = st_ref[hd]
        o = lax.dot_general((qf * jnp.exp(G)).astype(BF16), st.astype(BF16), NT,
                            preferred_element_type=F32)
        decay = []
        for t in range(R):
            decay = [dcy * ft[t] for dcy in decay]
            if t > 0:
                decay.append(ft[t])
            qt_ = q_scr[hd, rows_t(t), :]
            acc = jnp.sum(qt_ * kt_[t], axis=-1, keepdims=True) * vt_[t]
            for s in range(t):
                a = jnp.sum(qt_ * kt_[s] * decay[s], axis=-1, keepdims=True)
                acc = acc + a * vt_[s]
            d_scr[hd, rows_t(t), :] = acc
        o = o + d_scr[hd]
        G3, k3 = G.reshape(ns, R, W), kf.reshape(ns, R, W)
        kt = (k3 * jnp.exp(G3[:, R - 1:R] - G3)).reshape(C, W).astype(BF16)
        qt = jnp.concatenate([qf[j * R:] * jnp.exp(G[j * R:] - G[j * R - 1:j * R]) for j in range(1, ns)],
                             axis=0).astype(BF16)
        a = lax.dot_general(qt, kt, NT, preferred_element_type=F32)
        far = jnp.dot(jnp.where(keep, a, 0.0).astype(BF16), v, preferred_element_type=F32)
        pieces = [o[i * R:(i + 1) * R] for i in range(ns)]
        start = 0
        for j in range(ns - 1):
            for i in range(j + 1, ns):
                pieces[i] = pieces[i] + far[start:start + R]
                start += R
        o = jnp.concatenate(pieces, axis=0)
        y = _rms(o, gain) * _silu(gate_ref[pl.ds(r0, C), cols].astype(F32))
        o_ref[pl.ds(r0, C), cols] = y.astype(o_ref.dtype)
        g_last = G[C - 1:C]
        kd = (kf * jnp.exp(g_last - G)).astype(BF16)
        st_ref[hd] = st * jnp.exp(g_last) + lax.dot_general(v, kd, TN, preferred_element_type=F32)

    def body(c, carry):
        r0 = pl.multiple_of(c * C, C)
        for hd in range(heads):
            chunk(hd, r0)
        return carry

    lax.fori_loop(0, seq // C, body, 0)


def _hgrn2(q, logf, v_gate, out_gain, heads):
    b, s, width = q.shape
    groups = width // (heads * HG_DIM)
    spec = pl.BlockSpec((None, s, heads * HG_DIM), lambda bi, h: (bi, 0, h))
    gate_spec = pl.BlockSpec((None, s, heads * HG_DIM), lambda bi, h: (bi, 0, groups + h))
    return pl.pallas_call(
        functools.partial(_hgrn_kernel, seq=s, heads=heads),
        out_shape=jax.ShapeDtypeStruct((b, s, width), BF16),
        grid=(b, groups),
        in_specs=[spec, spec, spec, gate_spec, pl.BlockSpec((1, HG_DIM), lambda bi, h: (0, 0))],
        out_specs=spec,
        scratch_shapes=[pltpu.VMEM((heads, HG_DIM, HG_DIM), F32)]
        + [pltpu.VMEM((heads, HG_CHUNK, HG_DIM), F32)] * 6,
        compiler_params=_params("parallel", "parallel"),
        name="hgrn2",
    )(q, logf, v_gate, v_gate, out_gain.reshape(1, HG_DIM))


def _xattn_kernel(q_ref, k_ref, v_ref, o_ref):
    for h in range(X_HEADS):
        cols = slice(h * X_HEAD_DIM, (h + 1) * X_HEAD_DIM)
        s = lax.dot_general(q_ref[:, cols], k_ref[:, cols], NT, preferred_element_type=F32)
        p = jnp.exp(s - jnp.max(s, axis=-1, keepdims=True))
        p = p / jnp.sum(p, axis=-1, keepdims=True)
        o_ref[:, cols] = jnp.dot(p.astype(BF16), v_ref[:, cols],
                                 preferred_element_type=F32).astype(o_ref.dtype)


def _memory_attention(q, k, v, ts):
    b, s, _ = q.shape
    m = k.shape[1]
    return pl.pallas_call(
        _xattn_kernel,
        out_shape=jax.ShapeDtypeStruct((b, s, X_WIDTH), BF16),
        grid=(b, s // ts),
        in_specs=[pl.BlockSpec((None, ts, X_WIDTH), lambda bi, si: (bi, si, 0)),
                  pl.BlockSpec((None, m, X_WIDTH), lambda bi, si: (bi, 0, 0)),
                  pl.BlockSpec((None, m, X_WIDTH), lambda bi, si: (bi, 0, 0))],
        out_specs=pl.BlockSpec((None, ts, X_WIDTH), lambda bi, si: (bi, si, 0)),
        compiler_params=_params("parallel", "parallel"),
        name="memory_attention",
    )(q, k, v)


def _merge_kernel(ya_ref, yb_ref, yc_ref, wa_ref, wb_ref, wc_ref, ga_ref, gb_ref, gc_ref, o_ref):
    acc = ga_ref[...].astype(F32) * jnp.dot(ya_ref[...], wa_ref[...], preferred_element_type=F32)
    acc += gb_ref[...].astype(F32) * jnp.dot(yb_ref[...], wb_ref[...], preferred_element_type=F32)
    acc += gc_ref[...].astype(F32) * jnp.dot(yc_ref[...], wc_ref[...], preferred_element_type=F32)
    o_ref[...] = acc.astype(o_ref.dtype)


def _merge_branches(ya, yb, yc, wa, wb, wc, gates, tm, tn):
    n, d = yb.shape[0], wa.shape[1]
    nj = d // tn
    row = lambda a: pl.BlockSpec((tm, a.shape[1]), lambda i, j: (i, 0))
    col = lambda w: pl.BlockSpec((w.shape[0], tn), lambda i, j: (0, j))
    gate = lambda br: pl.BlockSpec((tm, tn), lambda i, j: (i, br * nj + j))
    return pl.pallas_call(
        _merge_kernel,
        out_shape=jax.ShapeDtypeStruct((n, d), BF16),
        grid=(n // tm, nj),
        in_specs=[row(ya), row(yb), row(yc), col(wa), col(wb), col(wc), gate(0), gate(1), gate(2)],
        out_specs=pl.BlockSpec((tm, tn), lambda i, j: (i, j)),
        compiler_params=_params("parallel", "arbitrary"),
        name="merge_branches",
    )(ya, yb, yc, wa, wb, wc, gates, gates, gates)


def _store_row_tiles(ref, x, scr):
    m = x.shape[0]
    assert x.shape[1] == ROW_TILE * LANES, x.shape
    for a in range(ROW_TILE):
        scr[a // SUBLANES, pl.ds(a % SUBLANES, m, stride=SUBLANES), :] = x[:, a * LANES:(a + 1) * LANES]
    groups = [scr[t].reshape(m, SUBLANES, LANES) for t in range(ROW_TILE // SUBLANES)]
    ref[...] = jnp.stack(groups, axis=1).reshape(m * ROW_TILE, LANES).astype(ref.dtype)


def _row_tile_scratch(m, *lead):
    return pltpu.VMEM((*lead, ROW_TILE // SUBLANES, m * SUBLANES, LANES), F32)


def _row_tiles_to_scratch(ref, scr):
    m = scr.shape[1] // SUBLANES
    x = ref[...].astype(F32).reshape(m, ROW_TILE // SUBLANES, SUBLANES, LANES)
    for t in range(ROW_TILE // SUBLANES):
        scr[t] = x[:, t].reshape(m * SUBLANES, LANES)


def _scratch_chunk(scr, a):
    m = scr.shape[1] // SUBLANES
    return scr[a // SUBLANES, pl.ds(a % SUBLANES, m, stride=SUBLANES), :]


def _load_row_tiles(ref, scr):
    _row_tiles_to_scratch(ref, scr)
    return jnp.concatenate([_scratch_chunk(scr, a) for a in range(ROW_TILE)], axis=1)


def _swiglu(x, wg_ref, wu_ref, wd_ref):
    x = x.astype(BF16)
    g = jnp.dot(x, wg_ref[...], preferred_element_type=F32)
    u = jnp.dot(x, wu_ref[...], preferred_element_type=F32)
    return _bdot(_silu(g) * u, wd_ref[...])


def _outproj_kernel(m_ref, w_ref, x_ref, g_ref, wr_ref, h_ref, hn_ref, lg_ref, scr):
    tm = m_ref.shape[0]
    sub = scr.shape[2] // SUBLANES
    for c in range(tm // sub):
        rows = slice(c * sub, (c + 1) * sub)
        h = x_ref[rows, :] + jnp.dot(m_ref[rows, :], w_ref[...], preferred_element_type=F32)
        hn = _rms(h, g_ref[...])
        h_ref[rows, :] = h
        _store_row_tiles(hn_ref.at[pl.ds(c * sub * ROW_TILE, sub * ROW_TILE), :], hn, scr.at[c])
        lg_ref[rows, :] = _bdot(hn, wr_ref[...])


def _out_projection(merged, w_out, x, ffn_gain, w_router, tm):
    n, d = x.shape
    e = w_router.shape[1]
    sub = min(tm, PROJ_SUB_ROWS)
    row = pl.BlockSpec((tm, d), lambda i: (i, 0))
    const = lambda a: pl.BlockSpec(a.shape, lambda i: (0, 0))
    g = ffn_gain.reshape(1, d)
    return pl.pallas_call(
        _outproj_kernel,
        out_shape=[jax.ShapeDtypeStruct((n, d), F32), jax.ShapeDtypeStruct((n * ROW_TILE, LANES), BF16),
                   jax.ShapeDtypeStruct((n, e), F32)],
        grid=(n // tm,),
        in_specs=[row, const(w_out), row, const(g), const(w_router)],
        out_specs=[row, pl.BlockSpec((tm * ROW_TILE, LANES), lambda i: (i, 0)),
                   pl.BlockSpec((tm, e), lambda i: (i, 0))],
        scratch_shapes=[_row_tile_scratch(sub, tm // sub)],
        compiler_params=_params("parallel"),
        name="out_projection",
    )(merged, w_out, x, g, w_router)


def _route_kernel(lg_ref, bias_ref, h_ref, hn_ref, sg_ref, su_ref, sd_ref, idx_ref, wt_ref, rank_ref,
                  cnt_ref, base_ref, carry_ref, scr):
    i = pl.program_id(0)

    @pl.when(i == 0)
    def _():
        carry_ref[...] = jnp.zeros_like(carry_ref)

    base_ref[...] = h_ref[...] + _swiglu(_load_row_tiles(hn_ref, scr), sg_ref, su_ref, sd_ref)

    scores = jax.nn.sigmoid(lg_ref[...])
    tm, e = scores.shape
    sel = scores + bias_ref[...]
    lane = lax.broadcasted_iota(I32, (tm, e), 1)
    hots, ws = [], []
    for _ in range(TOP_K):
        mx = jnp.max(sel, axis=-1, keepdims=True)
        ix = jnp.min(jnp.where(sel == mx, lane, e), axis=-1, keepdims=True)
        hot = lane == ix
        hots.append(hot)
        ws.append(jnp.sum(jnp.where(hot, scores, 0.0), axis=-1, keepdims=True))
        sel = jnp.where(hot, -jnp.inf, sel)
    chosen = functools.reduce(jnp.logical_or, hots).astype(F32)
    r = lax.broadcasted_iota(I32, (tm, tm), 0)
    c = lax.broadcasted_iota(I32, (tm, tm), 1)
    before = (c < r).astype(BF16)
    ahead = jnp.dot(before, chosen.astype(BF16), preferred_element_type=F32) + carry_ref[...]
    carry_ref[...] += jnp.sum(chosen, axis=0, keepdims=True)
    cnt_ref[...] = carry_ref[...].astype(I32)

    norm = ROUTED_SCALE / sum(ws)
    out_lane = lax.broadcasted_iota(I32, (tm, LANES), 1)
    idx_o = jnp.zeros((tm, LANES), I32)
    wt_o = jnp.zeros((tm, LANES), F32)
    rk_o = jnp.zeros((tm, LANES), I32)
    for j in range(TOP_K):
        ix = jnp.sum(jnp.where(hots[j], lane, 0), axis=-1, keepdims=True)
        rk = jnp.sum(jnp.where(hots[j], ahead, 0.0), axis=-1, keepdims=True).astype(I32)
        idx_o = jnp.where(out_lane == j, ix, idx_o)
        wt_o = jnp.where(out_lane == j, ws[j] * norm, wt_o)
        rk_o = jnp.where(out_lane == j, rk, rk_o)
    idx_ref[...] = idx_o
    wt_ref[...] = wt_o
    rank_ref[...] = rk_o


def _route(logits, bias, h, hn, sg, su, sd, tm):
    n, e = logits.shape
    d = h.shape[1]
    row = pl.BlockSpec((tm, LANES), lambda i: (i, 0))
    wide = lambda a: pl.BlockSpec((a.shape[0] // (n // tm), a.shape[1]), lambda i: (i, 0))
    const = lambda a: pl.BlockSpec(a.shape, lambda i: (0, 0))
    bias = bias.reshape(1, e)
    return pl.pallas_call(
        _route_kernel,
        out_shape=[jax.ShapeDtypeStruct((n, LANES), I32), jax.ShapeDtypeStruct((n, LANES), F32),
                   jax.ShapeDtypeStruct((n, LANES), I32), jax.ShapeDtypeStruct((1, e), I32),
                   jax.ShapeDtypeStruct((n, d), F32)],
        grid=(n // tm,),
        in_specs=[wide(logits), const(bias), wide(h), wide(hn), const(sg), const(su), const(sd)],
        out_specs=[row, row, row, pl.BlockSpec((1, e), lambda i: (0, 0)), wide(h)],
        scratch_shapes=[pltpu.VMEM((1, e), F32), _row_tile_scratch(tm)],
        compiler_params=_params("arbitrary"),
        name="route",
    )(logits, bias, h, hn, sg, su, sd)


def _dispatch_kernel(tail_ref, pos_ref, hn_ref, xs_ref, zeros, sem, zsem):
    tm = hn_ref.shape[0] // ROW_TILE
    half = zeros.shape[0]

    def tile_rows(start, size=ROW_TILE):
        return pl.ds(pl.multiple_of(start, ROW_TILE), size)

    @pl.when(pl.program_id(0) == 0)
    def _():
        zeros[...] = jnp.zeros_like(zeros)

        def tail_copies(e):
            off, pad = tail_ref[e], tail_ref[N_EXPERTS + e]
            size = half
            while size >= ROW_TILE:
                yield (pad & size) != 0, pltpu.make_async_copy(
                    zeros.at[pl.ds(0, size), :], xs_ref.at[tile_rows(off, size), :], zsem)
                off = off + (pad & size)
                size //= 2

        def start(e, carry):
            for needed, copy in tail_copies(e):
                pl.when(needed)(copy.start)
            return carry

        def wait(e, carry):
            for needed, copy in tail_copies(e):
                pl.when(needed)(copy.wait)
            return carry

        lax.fori_loop(0, N_EXPERTS, start, 0)
        lax.fori_loop(0, N_EXPERTS, wait, 0)

        def slack_copy(k):
            return pltpu.make_async_copy(
                zeros, xs_ref.at[tile_rows(tail_ref[2 * N_EXPERTS] + k * half, half), :], zsem)

        n_slack = (xs_ref.shape[0] - tail_ref[2 * N_EXPERTS]) // half
        lax.fori_loop(0, n_slack, lambda k, c: (slack_copy(k).start(), c)[1], 0)
        lax.fori_loop(0, n_slack, lambda k, c: (slack_copy(k).wait(), c)[1], 0)

    def row_copy(r, dst_row):
        return pltpu.make_async_copy(hn_ref.at[tile_rows(r * ROW_TILE), :], xs_ref.at[tile_rows(dst_row), :], sem)

    def issue(r, carry):
        for j in range(TOP_K):
            row_copy(r, pos_ref[r * TOP_K + j]).start(priority=j % DMA_THREADS)
        return carry

    lax.fori_loop(0, tm, issue, 0)
    for j in range(TOP_K):
        pltpu.make_async_copy(hn_ref, xs_ref.at[pl.ds(0, tm * ROW_TILE), :], sem).wait()


def _dispatch(hn, pos_rows, tails, n_blocks, tm):
    n = hn.shape[0] // ROW_TILE
    blk = MOE_BLOCK
    return pl.pallas_call(
        _dispatch_kernel,
        out_shape=jax.ShapeDtypeStruct((n_blocks * blk * ROW_TILE, LANES), hn.dtype),
        grid_spec=pltpu.PrefetchScalarGridSpec(
            num_scalar_prefetch=1,
            grid=(n // tm,),
            in_specs=[pl.BlockSpec((tm * TOP_K,), lambda i, tl: (i,), memory_space=pltpu.SMEM),
                      pl.BlockSpec((tm * ROW_TILE, LANES), lambda i, tl: (i, 0))],
            out_specs=pl.BlockSpec(memory_space=pl.ANY),
            scratch_shapes=[pltpu.VMEM((blk // 2 * ROW_TILE, LANES), hn.dtype), pltpu.SemaphoreType.DMA,
                            pltpu.SemaphoreType.DMA]),
        compiler_params=pltpu.CompilerParams(dimension_semantics=("arbitrary",),
                                             vmem_limit_bytes=VMEM_LIMIT, has_side_effects=True),
        name="moe_dispatch",
    )(tails, pos_rows, hn)


def _expert_kernel(be_ref, nu_ref, nxt_ref, valid_ref, x_ref, wg_ref, wu_ref, wd_ref, y_ref, wg_f, wu_f, wd_f,
                   wg_b, wu_b, wd_b, x_scr, y_scr, sem):
    b = pl.program_id(0)

    def fetch(e):
        return [pltpu.make_async_copy(src.at[e], dst, sem)
                for src, dst in ((wg_ref, wg_f), (wu_ref, wu_f), (wd_ref, wd_f))]

    @pl.when(b >= nu_ref[0])
    def _():
        y_ref[...] = jnp.zeros_like(y_ref)

    @pl.when(b < nu_ref[0])
    def _():
        e = be_ref[b]

        @pl.when(b == 0)
        def _():
            for copy in fetch(e):
                copy.start()

        @pl.when((b == 0) | (e != be_ref[jnp.maximum(b - 1, 0)]))
        def _():
            for copy in fetch(e):
                copy.wait()
            wg_b[...] = wg_f[...].astype(BF16)
            wu_b[...] = wu_f[...].astype(BF16)
            wd_b[...] = wd_f[...].astype(BF16)

            @pl.when(nxt_ref[e] >= 0)
            def _():
                for copy in fetch(nxt_ref[e]):
                    copy.start()

        sub = x_scr.shape[2] // SUBLANES
        n_sub = x_ref.shape[0] // (sub * ROW_TILE)

        def run(live):
            for c in range(n_sub):
                rows = pl.ds(c * sub * ROW_TILE, sub * ROW_TILE)
                if c < live:
                    x = _load_row_tiles(x_ref.at[rows, :], x_scr.at[c])
                    _store_row_tiles(y_ref.at[rows, :], _swiglu(x, wg_b, wu_b, wd_b), y_scr.at[c])
                else:
                    y_ref[rows, :] = jnp.zeros((sub * ROW_TILE, LANES), y_ref.dtype)

        live = (valid_ref[b] + sub - 1) // sub
        for k in range(1, n_sub + 1):
            pl.when(live == k)(functools.partial(run, k))


def _experts(xs, block_expert, n_used, next_expert, valid_rows, n_blocks, wg, wu, wd):
    d, ff = wg.shape[1], wg.shape[2]
    rows = MOE_BLOCK * ROW_TILE
    sub = min(MOE_BLOCK, MOE_SUB_ROWS)
    relayout = _row_tile_scratch(sub, MOE_BLOCK // sub)
    used = lambda b, be, nu, nxt, valid: (jnp.minimum(b, nu[0] - 1), 0)
    hbm = pl.BlockSpec(memory_space=pl.ANY)
    return pl.pallas_call(
        _expert_kernel,
        out_shape=jax.ShapeDtypeStruct(xs.shape, xs.dtype),
        grid_spec=pltpu.PrefetchScalarGridSpec(
            num_scalar_prefetch=4,
            grid=(n_blocks,),
            in_specs=[pl.BlockSpec((rows, LANES), used), hbm, hbm, hbm],
            out_specs=pl.BlockSpec((rows, LANES), lambda b, be, nu, nxt, valid: (b, 0)),
            scratch_shapes=[pltpu.VMEM((d, ff), F32), pltpu.VMEM((d, ff), F32), pltpu.VMEM((ff, d), F32),
                            pltpu.VMEM((d, ff), BF16), pltpu.VMEM((d, ff), BF16), pltpu.VMEM((ff, d), BF16),
                            relayout, relayout, pltpu.SemaphoreType.DMA]),
        compiler_params=_params("arbitrary"),
        name="moe_experts",
    )(block_expert, n_used, next_expert, valid_rows, xs, wg, wu, wd)


def _combine_kernel(pos_ref, pos_next_ref, base_ref, wt_ref, y_ref, o_ref, buf, scr, sem):
    i = pl.program_id(0)
    tm = base_ref.shape[0]
    cur = i % 2

    def tile_rows(start):
        return pl.ds(pl.multiple_of(start, ROW_TILE), ROW_TILE)

    def row_copy(half, r, j, src_row):
        return pltpu.make_async_copy(y_ref.at[tile_rows(src_row), :],
                                     buf.at[half, j, tile_rows(r * ROW_TILE), :], sem.at[half])

    def request(slots_ref, half):
        def issue(r, carry):
            for j in range(TOP_K):
                row_copy(half, r, j, slots_ref[r * TOP_K + j]).start(priority=j % DMA_THREADS)
            return carry

        lax.fori_loop(0, tm, issue, 0)

    @pl.when(i == 0)
    def _():
        request(pos_ref, 0)

    @pl.when(i + 1 < pl.num_programs(0))
    def _():
        request(pos_next_ref, 1 - cur)

    for j in range(TOP_K):
        pltpu.make_async_copy(y_ref.at[pl.ds(0, tm * ROW_TILE), :], buf.at[cur, j], sem.at[cur]).wait()

    for j in range(TOP_K):
        _row_tiles_to_scratch(buf.at[cur, j], scr.at[j])
    wt = wt_ref[...]
    for a in range(ROW_TILE):
        cols = slice(a * LANES, (a + 1) * LANES)
        acc = base_ref[:, cols]
        for j in range(TOP_K):
            acc = acc + wt[:, j:j + 1] * _scratch_chunk(scr.at[j], a)
        o_ref[:, cols] = acc


def _combine(base, wts, pos_flat, y, tm):
    n, d = base.shape
    steps = n // tm
    row = pl.BlockSpec((tm, d), lambda i: (i, 0))
    slots = lambda im: pl.BlockSpec((tm * TOP_K,), im, memory_space=pltpu.SMEM)
    return pl.pallas_call(
        _combine_kernel,
        out_shape=jax.ShapeDtypeStruct((n, d), F32),
        grid=(steps,),
        in_specs=[slots(lambda i: (i,)), slots(lambda i: (jnp.minimum(i + 1, steps - 1),)),
                  row, pl.BlockSpec((tm, LANES), lambda i: (i, 0)), pl.BlockSpec(memory_space=pl.ANY)],
        out_specs=row,
        scratch_shapes=[pltpu.VMEM((2, TOP_K, tm * ROW_TILE, LANES), y.dtype), _row_tile_scratch(tm, TOP_K),
                        pltpu.SemaphoreType.DMA((2,))],
        compiler_params=_params("arbitrary"),
        name="moe_combine",
    )(pos_flat, pos_flat, base, wts, y)


def _tile(n, want):
    t = min(n, want)
    assert n % t == 0, (n, want)
    return t


def _layer(layer, h, mem, positions, mix_norm, w_in, a_q_gain, a_k_gain, lower_raw, hg_out_gain, x_q_gain,
           x_k_gain, mem_norm, w_mem_kv, w_branch_a, w_branch_b, w_branch_c, w_out, ffn_norm, w_router,
           router_bias, w_exp_gate, w_exp_up, w_exp_down, w_sh_gate, w_sh_up, w_sh_down):
    b, s, d = h.shape
    n = b * s
    x2 = h.reshape(n, d)
    tm = _tile(n, 1024)
    bf = lambda w: w.astype(BF16)

    xn, cos, sin = _rms_norm_rows_with_rope(x2, mix_norm, positions, _tile(n, 512))
    w_in = bf(w_in)
    tmd = _tile(s, tm)
    rope_in = lambda gain: ((cos, (tmd, HEAD_DIM), lambda i, j: (i, 0)),
                            (sin, (tmd, HEAD_DIM), lambda i, j: (i, 0)),
                            (gain.reshape(1, HEAD_DIM), (1, HEAD_DIM), lambda i, j: (0, 0)))
    proj = functools.partial(_proj, xn, w_in, tm=tm)
    dilated = functools.partial(_proj_dilated, xn, w_in, tm=tmd, seq=s)
    hg_w = lower_raw.shape[1]
    off = 0
    aq = dilated(functools.partial(_head_qk, HEAD_DIM ** -0.5), n0=off, extras=rope_in(a_q_gain),
                 name="proj_aq")
    off += A_WIDTH
    ak = dilated(functools.partial(_head_qk, 1.0), n0=off, extras=rope_in(a_k_gain), name="proj_ak")
    off += A_WIDTH
    av = dilated(_head_plain, n0=off, name="proj_av")
    off += A_WIDTH
    hq, = proj(_ep_silu, n0=off, n=hg_w, tn=512, out_dtypes=[BF16], name="proj_hq")
    off += hg_w
    logf, = proj(functools.partial(_ep_log_forget, layer), n0=off, n=hg_w, tn=512, out_dtypes=[F32],
                 extras=((lower_raw, (lower_raw.shape[0], 512), lambda i, j: (0, j)),), name="proj_hf")
    off += hg_w
    hiv_gate, = proj(_ep_plain, n0=off, n=2 * hg_w, tn=512, out_dtypes=[BF16], name="proj_hi_hgate")
    off += 2 * hg_w
    xq, = _proj(xn, w_in[:, off:off + X_WIDTH], functools.partial(_ep_headnorm, X_HEAD_DIM, X_HEAD_DIM ** -0.5),
                n0=0, n=X_WIDTH, tn=2 * X_HEAD_DIM, tm=tm, out_dtypes=[BF16],
                extras=((x_q_gain.reshape(1, X_HEAD_DIM), (1, X_HEAD_DIM), lambda i, j: (0, 0)),),
                name="proj_xq")
    off += X_WIDTH
    gates, = proj(_ep_sigmoid, n0=off, n=3 * d, tn=1024, out_dtypes=[BF16], name="proj_gates")

    r3 = lambda t: t.reshape(b, s, t.shape[1])
    ya = _dilated_attention(list(zip(aq, ak, av)), b, s).reshape(n, A_OUT)
    yb = _hgrn2(r3(hq), r3(logf), r3(hiv_gate), hg_out_gain, HG_HEADS_PER_STEP).reshape(n, hg_w)
    nm = mem.shape[0] * mem.shape[1]
    mem_n = _rms_norm_rows(mem.reshape(nm, d), mem_norm, _tile(nm, 512))
    w_kv = bf(w_mem_kv)
    tmm = _tile(nm, 1024)
    kn, = _proj(mem_n, w_kv, functools.partial(_ep_headnorm, X_HEAD_DIM, 1.0), n0=0, n=X_WIDTH,
                tn=2 * X_HEAD_DIM, tm=tmm, out_dtypes=[BF16],
                extras=((x_k_gain.reshape(1, X_HEAD_DIM), (1, X_HEAD_DIM), lambda i, j: (0, 0)),),
                name="proj_mem_k")
    vm, = _proj(mem_n, w_kv, _ep_plain, n0=X_WIDTH, n=X_WIDTH, tn=2 * X_HEAD_DIM, tm=tmm,
                out_dtypes=[BF16], name="proj_mem_v")
    rm = lambda t: t.reshape(b, mem.shape[1], X_WIDTH)
    yc = _memory_attention(r3(xq), rm(kn), rm(vm), _tile(s, 512)).reshape(n, X_WIDTH)

    merged = _merge_branches(ya, yb, yc, bf(w_branch_a), bf(w_branch_b), bf(w_branch_c), gates, tm, 512)
    h1, hn, logits = _out_projection(merged, bf(w_out), x2, ffn_norm, bf(w_router), _tile(n, 512))

    tr = _tile(n, 256)
    idx, wts, rank, counts, base = _route(logits, router_bias, h1, hn, bf(w_sh_gate), bf(w_sh_up),
                                          bf(w_sh_down), _tile(n, 512))
    blk = MOE_BLOCK
    counts = counts[0]
    padded = (counts + blk - 1) // blk * blk
    ends = jnp.cumsum(padded)
    starts = ends - padded
    hit = idx[:, :TOP_K, None] == jnp.arange(N_EXPERTS, dtype=I32)
    pos = (jnp.sum(jnp.where(hit, starts, 0), axis=-1) + rank[:, :TOP_K]).reshape(-1).astype(I32)
    pos = pos * ROW_TILE
    n_blocks = -(-(n * TOP_K + N_EXPERTS * (blk - 1)) // blk)
    n_used = (ends[-1] // blk).astype(I32)
    blocks = jnp.minimum(jnp.arange(n_blocks, dtype=I32), n_used - 1)
    block_expert = jnp.minimum(jnp.sum(ends[None, :] <= blocks[:, None] * blk, axis=1), N_EXPERTS - 1)
    tails = (jnp.concatenate([starts + counts, padded - counts, ends[-1:]]) * ROW_TILE).astype(I32)
    xs = _dispatch(hn, pos, tails, n_blocks, tr)
    later = lax.cummin(jnp.where(counts > 0, jnp.arange(N_EXPERTS, dtype=I32), N_EXPERTS), reverse=True)
    later = jnp.concatenate([later[1:], jnp.full((1,), N_EXPERTS, I32)])
    next_expert = jnp.where(later < N_EXPERTS, later, -1).astype(I32)
    mine = block_expert[:, None] == jnp.arange(N_EXPERTS, dtype=I32)
    real_end = jnp.sum(jnp.where(mine, starts + counts, 0), axis=1)
    valid_rows = jnp.clip(real_end - blocks * blk, 0, blk).astype(I32)
    y = _experts(xs, block_expert.astype(I32), n_used.reshape(1), next_expert, valid_rows, n_blocks,
                 w_exp_gate, w_exp_up, w_exp_down)
    out = _combine(base, wts, pos, y, _tile(n, 128))
    return out.reshape(b, s, d)


def kernel(x, mem, positions, mix_norm, w_in, a_q_gain, a_k_gain, hg_lower_bounds, hg_out_gain, x_q_gain, x_k_gain, mem_norm, w_mem_kv, w_branch_a, w_branch_b, w_branch_c, w_out, ffn_norm, w_router, router_bias, w_exp_gate, w_exp_up, w_exp_down, w_sh_gate, w_sh_up, w_sh_down):
    h = x
    for layer in range(w_in.shape[0]):
        h = _layer(layer, h, mem, positions, mix_norm[layer], w_in[layer], a_q_gain[layer], a_k_gain[layer],
                   hg_lower_bounds, hg_out_gain[layer], x_q_gain[layer], x_k_gain[layer], mem_norm[layer],
                   w_mem_kv[layer], w_branch_a[layer], w_branch_b[layer], w_branch_c[layer], w_out[layer],
                   ffn_norm[layer], w_router[layer], router_bias[layer], w_exp_gate[layer],
                   w_exp_up[layer], w_exp_down[layer], w_sh_gate[layer], w_sh_up[layer], w_sh_down[layer])
    return h
```

```python
import functools
import math

import jax
import jax.numpy as jnp
from jax import lax
from jax.experimental import pallas as pl
from jax.experimental.pallas import tpu as pltpu

F32, BF16, I32 = jnp.float32, jnp.bfloat16, jnp.int32

EPS = 1e-6
ROPE_THETA = 10000.0
HEAD_DIM = 128
DILATIONS = (1, 4, 16)
DIL_BACK = 128
HEADS_PER_GROUP = 4
A_WIDTH = len(DILATIONS) * HEADS_PER_GROUP * HEAD_DIM
A_OUT = HEADS_PER_GROUP * HEAD_DIM
HG_DIM = 128
HG_CHUNK = 64
HG_STRIP = 8
HG_HEADS_PER_STEP = 8
X_HEADS = 4
X_HEAD_DIM = 384
X_WIDTH = X_HEADS * X_HEAD_DIM
N_EXPERTS = 64
TOP_K = 8
ROUTED_SCALE = 2.5
MOE_BLOCK = 512
MOE_SUB_ROWS = 256
PROJ_SUB_ROWS = 256
DMA_THREADS = 2
LANES = 128
ROW_TILE = 16
SUBLANES = 8
VMEM_LIMIT = 56 * 1024 * 1024

NT = (((1,), (1,)), ((), ()))
TN = (((0,), (0,)), ((), ()))


def _params(*sem):
    return pltpu.CompilerParams(dimension_semantics=sem, vmem_limit_bytes=VMEM_LIMIT)


def _rms(x, gain):
    return x * lax.rsqrt(jnp.mean(x * x, axis=-1, keepdims=True) + EPS) * gain


def _silu(x):
    return x * jax.nn.sigmoid(x)


def _bdot(a, b):
    return jnp.dot(a.astype(BF16), b.astype(BF16), preferred_element_type=F32)


def _norm_kernel(x_ref, g_ref, o_ref):
    o_ref[...] = _rms(x_ref[...].astype(F32), g_ref[...]).astype(o_ref.dtype)


def _rms_norm_rows(x, gain, tm):
    m, d = x.shape
    return pl.pallas_call(
        _norm_kernel,
        out_shape=jax.ShapeDtypeStruct((m, d), BF16),
        grid=(m // tm,),
        in_specs=[pl.BlockSpec((tm, d), lambda i: (i, 0)), pl.BlockSpec((1, d), lambda i: (0, 0))],
        out_specs=pl.BlockSpec((tm, d), lambda i: (i, 0)),
        compiler_params=_params("parallel"),
        name="rms_norm_rows",
    )(x, gain.reshape(1, d))


def _rope_table_kernel(pos_ref, invf_ref, cos_ref, sin_ref):
    ang = pos_ref[...].astype(F32) * invf_ref[...]
    lane = lax.broadcasted_iota(I32, ang.shape, 1)
    sin = jnp.sin(ang)
    cos_ref[...] = jnp.cos(ang)
    sin_ref[...] = jnp.where(lane < HEAD_DIM // 2, -sin, sin)


def _norm_rope_kernel(x_ref, g_ref, pos_ref, invf_ref, o_ref, cos_ref, sin_ref):
    _norm_kernel(x_ref, g_ref, o_ref)
    _rope_table_kernel(pos_ref, invf_ref, cos_ref, sin_ref)


def _rms_norm_rows_with_rope(x, gain, positions, tm):
    n, d = x.shape
    half = HEAD_DIM // 2
    inv_freq = ROPE_THETA ** (-jnp.arange(half, dtype=F32) / half)
    invf = jnp.concatenate([inv_freq, inv_freq]).reshape(1, HEAD_DIM)
    table = pl.BlockSpec((tm, HEAD_DIM), lambda i: (i, 0))
    return pl.pallas_call(
        _norm_rope_kernel,
        out_shape=[jax.ShapeDtypeStruct((n, d), BF16)] + [jax.ShapeDtypeStruct((n, HEAD_DIM), F32)] * 2,
        grid=(n // tm,),
        in_specs=[pl.BlockSpec((tm, d), lambda i: (i, 0)), pl.BlockSpec((1, d), lambda i: (0, 0)),
                  pl.BlockSpec((tm, 1), lambda i: (i, 0)), pl.BlockSpec((1, HEAD_DIM), lambda i: (0, 0))],
        out_specs=[pl.BlockSpec((tm, d), lambda i: (i, 0)), table, table],
        compiler_params=_params("parallel"),
        name="rms_norm_rope",
    )(x, gain.reshape(1, d), positions.reshape(n, 1), invf)


def _proj(a, w, epilogue, *, n0, n, tn, tm, out_dtypes, extras=(), name):
    m, k = a.shape
    assert n0 % tn == 0 and n % tn == 0 and m % tm == 0, (n0, n, tn, m, tm)
    j0 = n0 // tn
    n_extra = len(extras)
    sub = min(tm, PROJ_SUB_ROWS)

    def kern(a_ref, w_ref, *refs):
        for c in range(tm // sub):
            rows = slice(c * sub, (c + 1) * sub)
            acc = jnp.dot(a_ref[rows, :], w_ref[...], preferred_element_type=F32)
            epilogue(acc, rows, refs[:n_extra], refs[n_extra:])

    in_specs = [pl.BlockSpec((tm, k), lambda i, j: (i, 0)),
                pl.BlockSpec((k, tn), lambda i, j: (0, j + j0))]
    in_specs += [pl.BlockSpec(bs, im) for _, bs, im in extras]
    return pl.pallas_call(
        kern,
        out_shape=[jax.ShapeDtypeStruct((m, n), dt) for dt in out_dtypes],
        grid=(m // tm, n // tn),
        in_specs=in_specs,
        out_specs=[pl.BlockSpec((tm, tn), lambda i, j: (i, j)) for _ in out_dtypes],
        compiler_params=_params("parallel", "arbitrary"),
        name=name,
    )(a, w, *[e[0] for e in extras])


def _ep_plain(acc, rows, ins, outs):
    outs[0][rows, :] = acc.astype(outs[0].dtype)


def _ep_silu(acc, rows, ins, outs):
    outs[0][rows, :] = _silu(acc).astype(outs[0].dtype)


def _ep_sigmoid(acc, rows, ins, outs):
    outs[0][rows, :] = jax.nn.sigmoid(acc).astype(outs[0].dtype)


def _head_qk(scale, x, rows, ins):
    y = _rms(x, ins[2][...])
    return (y * ins[0][rows, :] + pltpu.roll(y, HEAD_DIM // 2, axis=1) * ins[1][rows, :]) * scale


def _head_plain(x, rows, ins):
    return x


def _proj_dilated(a, w, head_fn, *, n0, tm, seq, extras=(), name):
    m, k = a.shape
    tn = A_OUT
    assert n0 % tn == 0 and seq % tm == 0 and m % seq == 0, (n0, tn, seq, tm, m)
    j0 = n0 // tn
    tiles = seq // tm
    n_extra = len(extras)
    sub = min(tm, PROJ_SUB_ROWS)
    n_groups = len(DILATIONS)

    def kern(a_ref, w_ref, *refs):
        ins, outs, scr = refs[:n_extra], refs[n_extra:n_extra + n_groups], refs[-1]
        for g, d in enumerate(DILATIONS):
            @pl.when(pl.program_id(1) == g)
            def _(g=g, d=d):
                per = sub // d
                for c in range(tm // sub):
                    rows = slice(c * sub, (c + 1) * sub)
                    acc = jnp.dot(a_ref[rows, :], w_ref[...], preferred_element_type=F32)
                    for h in range(HEADS_PER_GROUP):
                        cols = slice(h * HEAD_DIM, (h + 1) * HEAD_DIM)
                        scr[c, h] = head_fn(acc[:, cols], rows, ins)
                        for r in range(d):
                            outs[g][r, c * per:(c + 1) * per, cols] = (
                                scr[c, h, pl.ds(r, per, stride=d), :].astype(outs[g].dtype))

    in_specs = [pl.BlockSpec((tm, k), lambda i, j: (i, 0)),
                pl.BlockSpec((k, tn), lambda i, j: (0, j + j0))]
    in_specs += [pl.BlockSpec(bs, im) for _, bs, im in extras]
    return pl.pallas_call(
        kern,
        out_shape=[jax.ShapeDtypeStruct((m // seq, d, seq // d, tn), BF16) for d in DILATIONS],
        grid=(m // tm, n_groups),
        in_specs=in_specs,
        out_specs=[pl.BlockSpec((None, d, tm // d, tn), lambda i, j: (i // tiles, 0, i % tiles, 0))
                   for d in DILATIONS],
        scratch_shapes=[pltpu.VMEM((tm // sub, HEADS_PER_GROUP, sub, HEAD_DIM), F32)],
        compiler_params=_params("parallel", "arbitrary"),
        name=name,
    )(a, w, *[e[0] for e in extras])


def _ep_headnorm(width, scale, acc, rows, ins, outs):
    gain = ins[0][...]
    for h in range(acc.shape[1] // width):
        sl = slice(h * width, (h + 1) * width)
        outs[0][rows, sl] = (_rms(acc[:, sl], gain) * scale).astype(outs[0].dtype)


def _ep_log_forget(layer, acc, rows, ins, outs):
    raw = ins[0][...]
    mx = jnp.max(raw, axis=0, keepdims=True)
    ex = jnp.exp(raw - mx)
    lb = jnp.sum(ex[:layer + 1], axis=0, keepdims=True) / jnp.sum(ex, axis=0, keepdims=True)
    outs[0][rows, :] = jnp.log(lb + (1.0 - lb) * jax.nn.sigmoid(acc))


def _dil_attn_kernel(*refs, seq):
    n_groups = len(DILATIONS)
    o_ref, o_scr, l_scr = refs[3 * n_groups:]
    blk = DIL_BACK

    for g, d in enumerate(DILATIONS):
        q_ref, k_ref, v_ref = refs[3 * g:3 * g + 3]
        length = seq // d
        nqb = max(length // blk, 1)
        qrows = min(blk, length)
        width = min(2 * blk, length)

        for r in range(d):
            for qb in range(nqb):
                q0 = qb * qrows
                ks = min(max(q0 + qrows - width, 0), length - width)
                q = q_ref[r, q0:q0 + qrows, :]
                k = k_ref[r, ks:ks + width, :]
                v = v_ref[r, ks:ks + width, :]
                s = lax.dot_general(q, k, NT, preferred_element_type=F32)
                rel = ((q0 - ks) + lax.broadcasted_iota(I32, s.shape, 0)
                       - lax.broadcasted_iota(I32, s.shape, 1))
                s = jnp.where((rel >= 0) & (rel <= DIL_BACK), s, -jnp.inf)
                m = jnp.max(s, axis=-1, keepdims=True)
                p = jnp.exp(s - m)
                den = jnp.sum(p, axis=-1, keepdims=True)
                o = jnp.dot((p / den).astype(BF16), v, preferred_element_type=F32)
                rows = pl.ds(r + d * q0, qrows, stride=d)
                o_scr[g, rows, :] = o
                l_scr[g, rows, :] = jnp.broadcast_to(m + jnp.log(den), o.shape)

    def merge(qi, carry):
        rows = pl.ds(pl.multiple_of(qi * blk, blk), blk)
        lses = [l_scr[g, rows, :] for g in range(n_groups)]
        top = functools.reduce(jnp.maximum, lses)
        ws = [jnp.exp(l - top) for l in lses]
        ya = sum(w * o_scr[g, rows, :] for g, w in enumerate(ws)) / sum(ws)
        o_ref[rows, :] = ya.astype(o_ref.dtype)
        return carry

    lax.fori_loop(0, seq // blk, merge, 0)


def _dilated_attention(qkv, b, s):
    in_specs, args = [], []
    for g, d in enumerate(DILATIONS):
        for t in qkv[g]:
            in_specs.append(pl.BlockSpec((None, d, s // d, HEAD_DIM), lambda bi, h: (bi, 0, 0, h)))
            args.append(t)
    return pl.pallas_call(
        functools.partial(_dil_attn_kernel, seq=s),
        out_shape=jax.ShapeDtypeStruct((b, s, A_OUT), BF16),
        grid=(b, HEADS_PER_GROUP),
        in_specs=in_specs,
        out_specs=pl.BlockSpec((None, s, HEAD_DIM), lambda bi, h: (bi, 0, h)),
        scratch_shapes=[pltpu.VMEM((len(DILATIONS), s, HEAD_DIM), F32)] * 2,
        compiler_params=_params("parallel", "parallel"),
        name="dilated_attention",
    )(*args)


def _hgrn_kernel(q_ref, lf_ref, v_ref, gate_ref, gain_ref, o_ref, st_ref, q_scr, v_scr, l_scr, d_scr,
                 g_scr, k_scr, *, seq, heads):
    C, R, W = HG_CHUNK, HG_STRIP, HG_DIM
    ns = C // R
    st_ref[...] = jnp.zeros_like(st_ref)
    gain = gain_ref[...]
    row_s = lax.broadcasted_iota(I32, (ns, W), 0)
    far_rows = R * ns * (ns - 1) // 2
    rr = lax.broadcasted_iota(I32, (far_rows, C), 0)
    cc = lax.broadcasted_iota(I32, (far_rows, C), 1)
    keep, start = None, 0
    for j in range(ns - 1):
        size = (ns - 1 - j) * R
        blk = (rr >= start) & (rr < start + size) & (cc >= j * R) & (cc < (j + 1) * R)
        keep = blk if keep is None else keep | blk
        start += size

    def chunk(hd, r0):
        cols = slice(hd * W, (hd + 1) * W)
        qf = q_ref[pl.ds(r0, C), cols].astype(F32)
        v = v_ref[pl.ds(r0, C), cols]
        vf = v.astype(F32)
        q_scr[hd] = qf
        v_scr[hd] = vf
        l_scr[hd] = lf_ref[pl.ds(r0, C), cols]
        rows_t = lambda t: pl.ds(t, ns, stride=R)
        lt = [l_scr[hd, rows_t(t), :] for t in range(R)]
        ft = [jnp.exp(l) for l in lt]
        vt_ = [v_scr[hd, rows_t(t), :] for t in range(R)]
        pre = [lt[0]]
        for t in range(1, R):
            pre.append(pre[-1] + lt[t])
        total = pre[R - 1]
        before = total
        step = 1
        while step < ns:
            before = before + jnp.where(row_s >= step, pltpu.roll(before, step, axis=0), 0.0)
            step *= 2
        before = before - total
        kt_ = [1.0 - f for f in ft]
        for t in range(R):
            g_scr[hd, rows_t(t), :] = pre[t] + before
            k_scr[hd, rows_t(t), :] = kt_[t]
        G = g_scr[hd]
        kf = k_scr[hd]
        st = st_ref[hd]
        o = lax.dot_general((qf * jnp.exp(G)).astype(BF16), st.astype(BF16), NT,
                            preferred_element_type=F32)
        decay = []
        for t in range(R):
            decay = [dcy * ft[t] for dcy in decay]
            if t > 0:
                decay.append(ft[t])
            qt_ = q_scr[hd, rows_t(t), :]
            acc = jnp.sum(qt_ * kt_[t], axis=-1, keepdims=True) * vt_[t]
            for s in range(t):
                a = jnp.sum(qt_ * kt_[s] * decay[s], axis=-1, keepdims=True)
                acc = acc + a * vt_[s]
            d_scr[hd, rows_t(t), :] = acc
        o = o + d_scr[hd]
        G3, k3 = G.reshape(ns, R, W), kf.reshape(ns, R, W)
        kt = (k3 * jnp.exp(G3[:, R - 1:R] - G3)).reshape(C, W).astype(BF16)
        qt = jnp.concatenate([qf[j * R:] * jnp.exp(G[j * R:] - G[j * R - 1:j * R]) for j in range(1, ns)],
                             axis=0).astype(BF16)
        a = lax.dot_general(qt, kt, NT, preferred_element_type=F32)
        far = jnp.dot(jnp.where(keep, a, 0.0).astype(BF16), v, preferred_element_type=F32)
        pieces = [o[i * R:(i + 1) * R] for i in range(ns)]
        start = 0
        for j in range(ns - 1):
            for i in range(j + 1, ns):
                pieces[i] = pieces[i] + far[start:start + R]
                start += R
        o = jnp.concatenate(pieces, axis=0)
        y = _rms(o, gain) * _silu(gate_ref[pl.ds(r0, C), cols].astype(F32))
        o_ref[pl.ds(r0, C), cols] = y.astype(o_ref.dtype)
        g_last = G[C - 1:C]
        kd = (kf * jnp.exp(g_last - G)).astype(BF16)
        st_ref[hd] = st * jnp.exp(g_last) + lax.dot_general(v, kd, TN, preferred_element_type=F32)

    def body(c, carry):
        r0 = pl.multiple_of(c * C, C)
        for hd in range(heads):
            chunk(hd, r0)
        return carry

    lax.fori_loop(0, seq // C, body, 0)


def _hgrn2(q, logf, v_gate, out_gain, heads):
    b, s, width = q.shape
    groups = width // (heads * HG_DIM)
    spec = pl.BlockSpec((None, s, heads * HG_DIM), lambda bi, h: (bi, 0, h))
    gate_spec = pl.BlockSpec((None, s, heads * HG_DIM), lambda bi, h: (bi, 0, groups + h))
    return pl.pallas_call(
        functools.partial(_hgrn_kernel, seq=s, heads=heads),
        out_shape=jax.ShapeDtypeStruct((b, s, width), BF16),
        grid=(b, groups),
        in_specs=[spec, spec, spec, gate_spec, pl.BlockSpec((1, HG_DIM), lambda bi, h: (0, 0))],
        out_specs=spec,
        scratch_shapes=[pltpu.VMEM((heads, HG_DIM, HG_DIM), F32)]
        + [pltpu.VMEM((heads, HG_CHUNK, HG_DIM), F32)] * 6,
        compiler_params=_params("parallel", "parallel"),
        name="hgrn2",
    )(q, logf, v_gate, v_gate, out_gain.reshape(1, HG_DIM))


def _xattn_kernel(q_ref, k_ref, v_ref, o_ref):
    for h in range(X_HEADS):
        cols = slice(h * X_HEAD_DIM, (h + 1) * X_HEAD_DIM)
        s = lax.dot_general(q_ref[:, cols], k_ref[:, cols], NT, preferred_element_type=F32)
        p = jnp.exp(s - jnp.max(s, axis=-1, keepdims=True))
        p = p / jnp.sum(p, axis=-1, keepdims=True)
        o_ref[:, cols] = jnp.dot(p.astype(BF16), v_ref[:, cols],
                                 preferred_element_type=F32).astype(o_ref.dtype)


def _memory_attention(q, k, v, ts):
    b, s, _ = q.shape
    m = k.shape[1]
    return pl.pallas_call(
        _xattn_kernel,
        out_shape=jax.ShapeDtypeStruct((b, s, X_WIDTH), BF16),
        grid=(b, s // ts),
        in_specs=[pl.BlockSpec((None, ts, X_WIDTH), lambda bi, si: (bi, si, 0)),
                  pl.BlockSpec((None, m, X_WIDTH), lambda bi, si: (bi, 0, 0)),
                  pl.BlockSpec((None, m, X_WIDTH), lambda bi, si: (bi, 0, 0))],
        out_specs=pl.BlockSpec((None, ts, X_WIDTH), lambda bi, si: (bi, si, 0)),
        compiler_params=_params("parallel", "parallel"),
        name="memory_attention",
    )(q, k, v)


def _merge_kernel(ya_ref, yb_ref, yc_ref, wa_ref, wb_ref, wc_ref, ga_ref, gb_ref, gc_ref, o_ref):
    acc = ga_ref[...].astype(F32) * jnp.dot(ya_ref[...], wa_ref[...], preferred_element_type=F32)
    acc += gb_ref[...].astype(F32) * jnp.dot(yb_ref[...], wb_ref[...], preferred_element_type=F32)
    acc += gc_ref[...].astype(F32) * jnp.dot(yc_ref[...], wc_ref[...], preferred_element_type=F32)
    o_ref[...] = acc.astype(o_ref.dtype)


def _merge_branches(ya, yb, yc, wa, wb, wc, gates, tm, tn):
    n, d = yb.shape[0], wa.shape[1]
    nj = d // tn
    row = lambda a: pl.BlockSpec((tm, a.shape[1]), lambda i, j: (i, 0))
    col = lambda w: pl.BlockSpec((w.shape[0], tn), lambda i, j: (0, j))
    gate = lambda br: pl.BlockSpec((tm, tn), lambda i, j: (i, br * nj + j))
    return pl.pallas_call(
        _merge_kernel,
        out_shape=jax.ShapeDtypeStruct((n, d), BF16),
        grid=(n // tm, nj),
        in_specs=[row(ya), row(yb), row(yc), col(wa), col(wb), col(wc), gate(0), gate(1), gate(2)],
        out_specs=pl.BlockSpec((tm, tn), lambda i, j: (i, j)),
        compiler_params=_params("parallel", "arbitrary"),
        name="merge_branches",
    )(ya, yb, yc, wa, wb, wc, gates, gates, gates)


def _store_row_tiles(ref, x, scr):
    m = x.shape[0]
    assert x.shape[1] == ROW_TILE * LANES, x.shape
    for a in range(ROW_TILE):
        scr[a // SUBLANES, pl.ds(a % SUBLANES, m, stride=SUBLANES), :] = x[:, a * LANES:(a + 1) * LANES]
    groups = [scr[t].reshape(m, SUBLANES, LANES) for t in range(ROW_TILE // SUBLANES)]
    ref[...] = jnp.stack(groups, axis=1).reshape(m * ROW_TILE, LANES).astype(ref.dtype)


def _row_tile_scratch(m, *lead):
    return pltpu.VMEM((*lead, ROW_TILE // SUBLANES, m * SUBLANES, LANES), F32)


def _row_tiles_to_scratch(ref, scr):
    m = scr.shape[1] // SUBLANES
    x = ref[...].astype(F32).reshape(m, ROW_TILE // SUBLANES, SUBLANES, LANES)
    for t in range(ROW_TILE // SUBLANES):
        scr[t] = x[:, t].reshape(m * SUBLANES, LANES)


def _scratch_chunk(scr, a):
    m = scr.shape[1] // SUBLANES
    return scr[a // SUBLANES, pl.ds(a % SUBLANES, m, stride=SUBLANES), :]


def _load_row_tiles(ref, scr):
    _row_tiles_to_scratch(ref, scr)
    return jnp.concatenate([_scratch_chunk(scr, a) for a in range(ROW_TILE)], axis=1)


def _swiglu(x, wg_ref, wu_ref, wd_ref):
    x = x.astype(BF16)
    g = jnp.dot(x, wg_ref[...], preferred_element_type=F32)
    u = jnp.dot(x, wu_ref[...], preferred_element_type=F32)
    return _bdot(_silu(g) * u, wd_ref[...])


def _outproj_kernel(m_ref, w_ref, x_ref, g_ref, wr_ref, h_ref, hn_ref, lg_ref, scr):
    tm = m_ref.shape[0]
    sub = scr.shape[2] // SUBLANES
    for c in range(tm // sub):
        rows = slice(c * sub, (c + 1) * sub)
        h = x_ref[rows, :] + jnp.dot(m_ref[rows, :], w_ref[...], preferred_element_type=F32)
        hn = _rms(h, g_ref[...])
        h_ref[rows, :] = h
        _store_row_tiles(hn_ref.at[pl.ds(c * sub * ROW_TILE, sub * ROW_TILE), :], hn, scr.at[c])
        lg_ref[rows, :] = _bdot(hn, wr_ref[...])


def _out_projection(merged, w_out, x, ffn_gain, w_router, tm):
    n, d = x.shape
    e = w_router.shape[1]
    sub = min(tm, PROJ_SUB_ROWS)
    row = pl.BlockSpec((tm, d), lambda i: (i, 0))
    const = lambda a: pl.BlockSpec(a.shape, lambda i: (0, 0))
    g = ffn_gain.reshape(1, d)
    return pl.pallas_call(
        _outproj_kernel,
        out_shape=[jax.ShapeDtypeStruct((n, d), F32), jax.ShapeDtypeStruct((n * ROW_TILE, LANES), BF16),
                   jax.ShapeDtypeStruct((n, e), F32)],
        grid=(n // tm,),
        in_specs=[row, const(w_out), row, const(g), const(w_router)],
        out_specs=[row, pl.BlockSpec((tm * ROW_TILE, LANES), lambda i: (i, 0)),
                   pl.BlockSpec((tm, e), lambda i: (i, 0))],
        scratch_shapes=[_row_tile_scratch(sub, tm // sub)],
        compiler_params=_params("parallel"),
        name="out_projection",
    )(merged, w_out, x, g, w_router)


def _route_kernel(lg_ref, bias_ref, h_ref, hn_ref, sg_ref, su_ref, sd_ref, idx_ref, wt_ref, rank_ref,
                  cnt_ref, base_ref, carry_ref, scr):
    i = pl.program_id(0)

    @pl.when(i == 0)
    def _():
        carry_ref[...] = jnp.zeros_like(carry_ref)

    base_ref[...] = h_ref[...] + _swiglu(_load_row_tiles(hn_ref, scr), sg_ref, su_ref, sd_ref)

    scores = jax.nn.sigmoid(lg_ref[...])
    tm, e = scores.shape
    sel = scores + bias_ref[...]
    lane = lax.broadcasted_iota(I32, (tm, e), 1)
    hots, ws = [], []
    for _ in range(TOP_K):
        mx = jnp.max(sel, axis=-1, keepdims=True)
        ix = jnp.min(jnp.where(sel == mx, lane, e), axis=-1, keepdims=True)
        hot = lane == ix
        hots.append(hot)
        ws.append(jnp.sum(jnp.where(hot, scores, 0.0), axis=-1, keepdims=True))
        sel = jnp.where(hot, -jnp.inf, sel)
    chosen = functools.reduce(jnp.logical_or, hots).astype(F32)
    r = lax.broadcasted_iota(I32, (tm, tm), 0)
    c = lax.broadcasted_iota(I32, (tm, tm), 1)
    before = (c < r).astype(BF16)
    ahead = jnp.dot(before, chosen.astype(BF16), preferred_element_type=F32) + carry_ref[...]
    carry_ref[...] += jnp.sum(chosen, axis=0, keepdims=True)
    cnt_ref[...] = carry_ref[...].astype(I32)

    norm = ROUTED_SCALE / sum(ws)
    out_lane = lax.broadcasted_iota(I32, (tm, LANES), 1)
    idx_o = jnp.zeros((tm, LANES), I32)
    wt_o = jnp.zeros((tm, LANES), F32)
    rk_o = jnp.zeros((tm, LANES), I32)
    for j in range(TOP_K):
        ix = jnp.sum(jnp.where(hots[j], lane, 0), axis=-1, keepdims=True)
        rk = jnp.sum(jnp.where(hots[j], ahead, 0.0), axis=-1, keepdims=True).astype(I32)
        idx_o = jnp.where(out_lane == j, ix, idx_o)
        wt_o = jnp.where(out_lane == j, ws[j] * norm, wt_o)
        rk_o = jnp.where(out_lane == j, rk, rk_o)
    idx_ref[...] = idx_o
    wt_ref[...] = wt_o
    rank_ref[...] = rk_o


def _route(logits, bias, h, hn, sg, su, sd, tm):
    n, e = logits.shape
    d = h.shape[1]
    row = pl.BlockSpec((tm, LANES), lambda i: (i, 0))
    wide = lambda a: pl.BlockSpec((a.shape[0] // (n // tm), a.shape[1]), lambda i: (i, 0))
    const = lambda a: pl.BlockSpec(a.shape, lambda i: (0, 0))
    bias = bias.reshape(1, e)
    return pl.pallas_call(
        _route_kernel,
        out_shape=[jax.ShapeDtypeStruct((n, LANES), I32), jax.ShapeDtypeStruct((n, LANES), F32),
                   jax.ShapeDtypeStruct((n, LANES), I32), jax.ShapeDtypeStruct((1, e), I32),
                   jax.ShapeDtypeStruct((n, d), F32)],
        grid=(n // tm,),
        in_specs=[wide(logits), const(bias), wide(h), wide(hn), const(sg), const(su), const(sd)],
        out_specs=[row, row, row, pl.BlockSpec((1, e), lambda i: (0, 0)), wide(h)],
        scratch_shapes=[pltpu.VMEM((1, e), F32), _row_tile_scratch(tm)],
        compiler_params=_params("arbitrary"),
        name="route",
    )(logits, bias, h, hn, sg, su, sd)


def _dispatch_kernel(tail_ref, pos_ref, hn_ref, xs_ref, zeros, sem, zsem):
    tm = hn_ref.shape[0] // ROW_TILE
    half = zeros.shape[0]

    def tile_rows(start, size=ROW_TILE):
        return pl.ds(pl.multiple_of(start, ROW_TILE), size)

    @pl.when(pl.program_id(0) == 0)
    def _():
        zeros[...] = jnp.zeros_like(zeros)

        def tail_copies(e):
            off, pad = tail_ref[e], tail_ref[N_EXPERTS + e]
            size = half
            while size >= ROW_TILE:
                yield (pad & size) != 0, pltpu.make_async_copy(
                    zeros.at[pl.ds(0, size), :], xs_ref.at[tile_rows(off, size), :], zsem)
                off = off + (pad & size)
                size //= 2

        def start(e, carry):
            for needed, copy in tail_copies(e):
                pl.when(needed)(copy.start)
            return carry

        def wait(e, carry):
            for needed, copy in tail_copies(e):
                pl.when(needed)(copy.wait)
            return carry

        lax.fori_loop(0, N_EXPERTS, start, 0)
        lax.fori_loop(0, N_EXPERTS, wait, 0)

        def slack_copy(k):
            return pltpu.make_async_copy(
                zeros, xs_ref.at[tile_rows(tail_ref[2 * N_EXPERTS] + k * half, half), :], zsem)

        n_slack = (xs_ref.shape[0] - tail_ref[2 * N_EXPERTS]) // half
        lax.fori_loop(0, n_slack, lambda k, c: (slack_copy(k).start(), c)[1], 0)
        lax.fori_loop(0, n_slack, lambda k, c: (slack_copy(k).wait(), c)[1], 0)

    def row_copy(r, dst_row):
        return pltpu.make_async_copy(hn_ref.at[tile_rows(r * ROW_TILE), :], xs_ref.at[tile_rows(dst_row), :], sem)

    def issue(r, carry):
        for j in range(TOP_K):
            row_copy(r, pos_ref[r * TOP_K + j]).start(priority=j % DMA_THREADS)
        return carry

    lax.fori_loop(0, tm, issue, 0)
    for j in range(TOP_K):
        pltpu.make_async_copy(hn_ref, xs_ref.at[pl.ds(0, tm * ROW_TILE), :], sem).wait()


def _dispatch(hn, pos_rows, tails, n_blocks, tm):
    n = hn.shape[0] // ROW_TILE
    blk = MOE_BLOCK
    return pl.pallas_call(
        _dispatch_kernel,
        out_shape=jax.ShapeDtypeStruct((n_blocks * blk * ROW_TILE, LANES), hn.dtype),
        grid_spec=pltpu.PrefetchScalarGridSpec(
            num_scalar_prefetch=1,
            grid=(n // tm,),
            in_specs=[pl.BlockSpec((tm * TOP_K,), lambda i, tl: (i,), memory_space=pltpu.SMEM),
                      pl.BlockSpec((tm * ROW_TILE, LANES), lambda i, tl: (i, 0))],
            out_specs=pl.BlockSpec(memory_space=pl.ANY),
            scratch_shapes=[pltpu.VMEM((blk // 2 * ROW_TILE, LANES), hn.dtype), pltpu.SemaphoreType.DMA,
                            pltpu.SemaphoreType.DMA]),
        compiler_params=pltpu.CompilerParams(dimension_semantics=("arbitrary",),
                                             vmem_limit_bytes=VMEM_LIMIT, has_side_effects=True),
        name="moe_dispatch",
    )(tails, pos_rows, hn)


def _expert_kernel(be_ref, nu_ref, nxt_ref, valid_ref, x_ref, wg_ref, wu_ref, wd_ref, y_ref, wg_f, wu_f, wd_f,
                   wg_b, wu_b, wd_b, x_scr, y_scr, sem):
    b = pl.program_id(0)

    def fetch(e):
        return [pltpu.make_async_copy(src.at[e], dst, sem)
                for src, dst in ((wg_ref, wg_f), (wu_ref, wu_f), (wd_ref, wd_f))]

    @pl.when(b >= nu_ref[0])
    def _():
        y_ref[...] = jnp.zeros_like(y_ref)

    @pl.when(b < nu_ref[0])
    def _():
        e = be_ref[b]

        @pl.when(b == 0)
        def _():
            for copy in fetch(e):
                copy.start()

        @pl.when((b == 0) | (e != be_ref[jnp.maximum(b - 1, 0)]))
        def _():
            for copy in fetch(e):
                copy.wait()
            wg_b[...] = wg_f[...].astype(BF16)
            wu_b[...] = wu_f[...].astype(BF16)
            wd_b[...] = wd_f[...].astype(BF16)

            @pl.when(nxt_ref[e] >= 0)
            def _():
                for copy in fetch(nxt_ref[e]):
                    copy.start()

        sub = x_scr.shape[2] // SUBLANES
        n_sub = x_ref.shape[0] // (sub * ROW_TILE)

        def run(live):
            for c in range(n_sub):
                rows = pl.ds(c * sub * ROW_TILE, sub * ROW_TILE)
                if c < live:
                    x = _load_row_tiles(x_ref.at[rows, :], x_scr.at[c])
                    _store_row_tiles(y_ref.at[rows, :], _swiglu(x, wg_b, wu_b, wd_b), y_scr.at[c])
                else:
                    y_ref[rows, :] = jnp.zeros((sub * ROW_TILE, LANES), y_ref.dtype)

        live = (valid_ref[b] + sub - 1) // sub
        for k in range(1, n_sub + 1):
            pl.when(live == k)(functools.partial(run, k))


def _experts(xs, block_expert, n_used, next_expert, valid_rows, n_blocks, wg, wu, wd):
    d, ff = wg.shape[1], wg.shape[2]
    rows = MOE_BLOCK * ROW_TILE
    sub = min(MOE_BLOCK, MOE_SUB_ROWS)
    relayout = _row_tile_scratch(sub, MOE_BLOCK // sub)
    used = lambda b, be, nu, nxt, valid: (jnp.minimum(b, nu[0] - 1), 0)
    hbm = pl.BlockSpec(memory_space=pl.ANY)
    return pl.pallas_call(
        _expert_kernel,
        out_shape=jax.ShapeDtypeStruct(xs.shape, xs.dtype),
        grid_spec=pltpu.PrefetchScalarGridSpec(
            num_scalar_prefetch=4,
            grid=(n_blocks,),
            in_specs=[pl.BlockSpec((rows, LANES), used), hbm, hbm, hbm],
            out_specs=pl.BlockSpec((rows, LANES), lambda b, be, nu, nxt, valid: (b, 0)),
            scratch_shapes=[pltpu.VMEM((d, ff), F32), pltpu.VMEM((d, ff), F32), pltpu.VMEM((ff, d), F32),
                            pltpu.VMEM((d, ff), BF16), pltpu.VMEM((d, ff), BF16), pltpu.VMEM((ff, d), BF16),
                            relayout, relayout, pltpu.SemaphoreType.DMA]),
        compiler_params=_params("arbitrary"),
        name="moe_experts",
    )(block_expert, n_used, next_expert, valid_rows, xs, wg, wu, wd)


def _combine_kernel(pos_ref, pos_next_ref, base_ref, wt_ref, y_ref, o_ref, buf, scr, sem):
    i = pl.program_id(0)
    tm = base_ref.shape[0]
    cur = i % 2

    def tile_rows(start):
        return pl.ds(pl.multiple_of(start, ROW_TILE), ROW_TILE)

    def row_copy(half, r, j, src_row):
        return pltpu.make_async_copy(y_ref.at[tile_rows(src_row), :],
                                     buf.at[half, j, tile_rows(r * ROW_TILE), :], sem.at[half])

    def request(slots_ref, half):
        def issue(r, carry):
            for j in range(TOP_K):
                row_copy(half, r, j, slots_ref[r * TOP_K + j]).start(priority=j % DMA_THREADS)
            return carry

        lax.fori_loop(0, tm, issue, 0)

    @pl.when(i == 0)
    def _():
        request(pos_ref, 0)

    @pl.when(i + 1 < pl.num_programs(0))
    def _():
        request(pos_next_ref, 1 - cur)

    for j in range(TOP_K):
        pltpu.make_async_copy(y_ref.at[pl.ds(0, tm * ROW_TILE), :], buf.at[cur, j], sem.at[cur]).wait()

    for j in range(TOP_K):
        _row_tiles_to_scratch(buf.at[cur, j], scr.at[j])
    wt = wt_ref[...]
    for a in range(ROW_TILE):
        cols = slice(a * LANES, (a + 1) * LANES)
        acc = base_ref[:, cols]
        for j in range(TOP_K):
            acc = acc + wt[:, j:j + 1] * _scratch_chunk(scr.at[j], a)
        o_ref[:, cols] = acc


def _combine(base, wts, pos_flat, y, tm):
    n, d = base.shape
    steps = n // tm
    row = pl.BlockSpec((tm, d), lambda i: (i, 0))
    slots = lambda im: pl.BlockSpec((tm * TOP_K,), im, memory_space=pltpu.SMEM)
    return pl.pallas_call(
        _combine_kernel,
        out_shape=jax.ShapeDtypeStruct((n, d), F32),
        grid=(steps,),
        in_specs=[slots(lambda i: (i,)), slots(lambda i: (jnp.minimum(i + 1, steps - 1),)),
                  row, pl.BlockSpec((tm, LANES), lambda i: (i, 0)), pl.BlockSpec(memory_space=pl.ANY)],
        out_specs=row,
        scratch_shapes=[pltpu.VMEM((2, TOP_K, tm * ROW_TILE, LANES), y.dtype), _row_tile_scratch(tm, TOP_K),
                        pltpu.SemaphoreType.DMA((2,))],
        compiler_params=_params("arbitrary"),
        name="moe_combine",
    )(pos_flat, pos_flat, base, wts, y)


def _tile(n, want):
    t = min(n, want)
    assert n % t == 0, (n, want)
    return t


def _layer(layer, h, mem, positions, mix_norm, w_in, a_q_gain, a_k_gain, lower_raw, hg_out_gain, x_q_gain,
           x_k_gain, mem_norm, w_mem_kv, w_branch_a, w_branch_b, w_branch_c, w_out, ffn_norm, w_router,
           router_bias, w_exp_gate, w_exp_up, w_exp_down, w_sh_gate, w_sh_up, w_sh_down):
    b, s, d = h.shape
    n = b * s
    x2 = h.reshape(n, d)
    tm = _tile(n, 2048)
    bf = lambda w: w.astype(BF16)

    xn, cos, sin = _rms_norm_rows_with_rope(x2, mix_norm, positions, _tile(n, 512))
    w_in = bf(w_in)
    tmd = _tile(s, tm)
    rope_in = lambda gain: ((cos, (tmd, HEAD_DIM), lambda i, j: (i, 0)),
                            (sin, (tmd, HEAD_DIM), lambda i, j: (i, 0)),
                            (gain.reshape(1, HEAD_DIM), (1, HEAD_DIM), lambda i, j: (0, 0)))
    proj = functools.partial(_proj, xn, w_in, tm=tm)
    dilated = functools.partial(_proj_dilated, xn, w_in, tm=tmd, seq=s)
    hg_w = lower_raw.shape[1]
    off = 0
    aq = dilated(functools.partial(_head_qk, HEAD_DIM ** -0.5), n0=off, extras=rope_in(a_q_gain),
                 name="proj_aq")
    off += A_WIDTH
    ak = dilated(functools.partial(_head_qk, 1.0), n0=off, extras=rope_in(a_k_gain), name="proj_ak")
    off += A_WIDTH
    av = dilated(_head_plain, n0=off, name="proj_av")
    off += A_WIDTH
    hq, = proj(_ep_silu, n0=off, n=hg_w, tn=512, out_dtypes=[BF16], name="proj_hq")
    off += hg_w
    logf, = proj(functools.partial(_ep_log_forget, layer), n0=off, n=hg_w, tn=512, out_dtypes=[F32],
                 extras=((lower_raw, (lower_raw.shape[0], 512), lambda i, j: (0, j)),), name="proj_hf")
    off += hg_w
    hiv_gate, = proj(_ep_plain, n0=off, n=2 * hg_w, tn=512, out_dtypes=[BF16], name="proj_hi_hgate")
    off += 2 * hg_w
    xq, = _proj(xn, w_in[:, off:off + X_WIDTH], functools.partial(_ep_headnorm, X_HEAD_DIM, X_HEAD_DIM ** -0.5),
                n0=0, n=X_WIDTH, tn=2 * X_HEAD_DIM, tm=tm, out_dtypes=[BF16],
                extras=((x_q_gain.reshape(1, X_HEAD_DIM), (1, X_HEAD_DIM), lambda i, j: (0, 0)),),
                name="proj_xq")
    off += X_WIDTH
    gates, = proj(_ep_sigmoid, n0=off, n=3 * d, tn=1024, out_dtypes=[BF16], name="proj_gates")

    r3 = lambda t: t.reshape(b, s, t.shape[1])
    ya = _dilated_attention(list(zip(aq, ak, av)), b, s).reshape(n, A_OUT)
    yb = _hgrn2(r3(hq), r3(logf), r3(hiv_gate), hg_out_gain, HG_HEADS_PER_STEP).reshape(n, hg_w)
    nm = mem.shape[0] * mem.shape[1]
    mem_n = _rms_norm_rows(mem.reshape(nm, d), mem_norm, _tile(nm, 512))
    w_kv = bf(w_mem_kv)
    tmm = _tile(nm, 1024)
    kn, = _proj(mem_n, w_kv, functools.partial(_ep_headnorm, X_HEAD_DIM, 1.0), n0=0, n=X_WIDTH,
                tn=2 * X_HEAD_DIM, tm=tmm, out_dtypes=[BF16],
                extras=((x_k_gain.reshape(1, X_HEAD_DIM), (1, X_HEAD_DIM), lambda i, j: (0, 0)),),
                name="proj_mem_k")
    vm, = _proj(mem_n, w_kv, _ep_plain, n0=X_WIDTH, n=X_WIDTH, tn=2 * X_HEAD_DIM, tm=tmm,
                out_dtypes=[BF16], name="proj_mem_v")
    rm = lambda t: t.reshape(b, mem.shape[1], X_WIDTH)
    yc = _memory_attention(r3(xq), rm(kn), rm(vm), _tile(s, 512)).reshape(n, X_WIDTH)

    merged = _merge_branches(ya, yb, yc, bf(w_branch_a), bf(w_branch_b), bf(w_branch_c), gates, _tile(n, 1024), 512)
    h1, hn, logits = _out_projection(merged, bf(w_out), x2, ffn_norm, bf(w_router), _tile(n, 512))

    tr = _tile(n, 256)
    idx, wts, rank, counts, base = _route(logits, router_bias, h1, hn, bf(w_sh_gate), bf(w_sh_up),
                                          bf(w_sh_down), _tile(n, 512))
    blk = MOE_BLOCK
    counts = counts[0]
    padded = (counts + blk - 1) // blk * blk
    ends = jnp.cumsum(padded)
    starts = ends - padded
    hit = idx[:, :TOP_K, None] == jnp.arange(N_EXPERTS, dtype=I32)
    pos = (jnp.sum(jnp.where(hit, starts, 0), axis=-1) + rank[:, :TOP_K]).reshape(-1).astype(I32)
    pos = pos * ROW_TILE
    n_blocks = -(-(n * TOP_K + N_EXPERTS * (blk - 1)) // blk)
    n_used = (ends[-1] // blk).astype(I32)
    blocks = jnp.minimum(jnp.arange(n_blocks, dtype=I32), n_used - 1)
    block_expert = jnp.minimum(jnp.sum(ends[None, :] <= blocks[:, None] * blk, axis=1), N_EXPERTS - 1)
    tails = (jnp.concatenate([starts + counts, padded - counts, ends[-1:]]) * ROW_TILE).astype(I32)
    xs = _dispatch(hn, pos, tails, n_blocks, tr)
    later = lax.cummin(jnp.where(counts > 0, jnp.arange(N_EXPERTS, dtype=I32), N_EXPERTS), reverse=True)
    later = jnp.concatenate([later[1:], jnp.full((1,), N_EXPERTS, I32)])
    next_expert = jnp.where(later < N_EXPERTS, later, -1).astype(I32)
    mine = block_expert[:, None] == jnp.arange(N_EXPERTS, dtype=I32)
    real_end = jnp.sum(jnp.where(mine, starts + counts, 0), axis=1)
    valid_rows = jnp.clip(real_end - blocks * blk, 0, blk).astype(I32)
    y = _experts(xs, block_expert.astype(I32), n_used.reshape(1), next_expert, valid_rows, n_blocks,
                 w_exp_gate, w_exp_up, w_exp_down)
    out = _combine(base, wts, pos, y, _tile(n, 128))
    return out.reshape(b, s, d)


def kernel(x, mem, positions, mix_norm, w_in, a_q_gain, a_k_gain, hg_lower_bounds, hg_out_gain, x_q_gain, x_k_gain, mem_norm, w_mem_kv, w_branch_a, w_branch_b, w_branch_c, w_out, ffn_norm, w_router, router_bias, w_exp_gate, w_exp_up, w_exp_down, w_sh_gate, w_sh_up, w_sh_down):
    h = x
    for layer in range(w_in.shape[0]):
        h = _layer(layer, h, mem, positions, mix_norm[layer], w_in[layer], a_q_gain[layer], a_k_gain[layer],
                   hg_lower_bounds, hg_out_gain[layer], x_q_gain[layer], x_k_gain[layer], mem_norm[layer],
                   w_mem_kv[layer], w_branch_a[layer], w_branch_b[layer], w_branch_c[layer], w_out[layer],
                   ffn_norm[layer], w_router[layer], router_bias[layer], w_exp_gate[layer],
                   w_exp_up[layer], w_exp_down[layer], w_sh_gate[layer], w_sh_up[layer], w_sh_down[layer])
    return h
```
